```python
import jax
import jax.numpy as jnp
from jax import lax
import numpy as np

D_MODEL = 1024
BATCH = 8
SEQ = 4096
DEPTH = 2

HEAD_DIM = 64
BLOCK = 128
SWA_HEADS = D_MODEL // (2 * HEAD_DIM)
SWA_KV_HEADS = SWA_HEADS // 4
SWA_GROUP = SWA_HEADS // SWA_KV_HEADS
WINDOW = 128
RWKV_HEADS = D_MODEL // (2 * HEAD_DIM)
RWKV_DIM = RWKV_HEADS * HEAD_DIM
DECAY_LORA = 64
ICLR_LORA = 64
GATE_LORA = 128
FOX_HEADS = D_MODEL // HEAD_DIM
FOX_DIM = FOX_HEADS * HEAD_DIM
D_FF = 2816
PLE_DIM = 256
N_EVEN = (DEPTH + 1) // 2
N_ODD = DEPTH // 2
NORM_EPS = 1e-6
GN_EPS = 64e-5
L2_EPS = 1e-12

SWA_Q = SWA_HEADS * HEAD_DIM
SWA_KV = SWA_KV_HEADS * HEAD_DIM
SWA_COLS = SWA_Q + 2 * SWA_KV
RWKV_COLS = 3 * RWKV_DIM + DECAY_LORA + ICLR_LORA + GATE_LORA
RWKV_SPLITS = [RWKV_DIM, 2 * RWKV_DIM, 3 * RWKV_DIM,
               3 * RWKV_DIM + DECAY_LORA, 3 * RWKV_DIM + DECAY_LORA + ICLR_LORA]
EVEN_IN = SWA_COLS + RWKV_COLS
EVEN_OUT = SWA_Q + RWKV_DIM
FOX_IN = 3 * FOX_DIM + FOX_HEADS

kernel_name = "hybrid_swa_rwkv7_fox_macaron"


def rms_norm(x, g):
    xf = x.astype(jnp.float32)
    y = xf * lax.rsqrt(jnp.mean(xf * xf, axis=-1, keepdims=True) + NORM_EPS)
    return (y * g.astype(jnp.float32)).astype(x.dtype)


def swiglu(h, w_gu, w_down):
    g, u = jnp.split(h @ w_gu, 2, axis=-1)
    return (jax.nn.silu(g) * u) @ w_down


def alibi_slopes(n):
    return 2.0 ** (-8.0 * jnp.arange(1, n + 1, dtype=jnp.float32) / n)


def sliding_window_attention(q, k, v, sinks):
    b, s = q.shape[:2]
    nb = s // BLOCK
    scale = HEAD_DIM ** -0.5
    qb = q.reshape(b, nb, BLOCK, SWA_KV_HEADS, SWA_GROUP, HEAD_DIM)
    pad = ((0, 0), (BLOCK, 0), (0, 0), (0, 0))
    kp = jnp.pad(k, pad).reshape(b, nb + 1, BLOCK, SWA_KV_HEADS, HEAD_DIM)
    vp = jnp.pad(v, pad).reshape(b, nb + 1, BLOCK, SWA_KV_HEADS, HEAD_DIM)
    kb = jnp.concatenate([kp[:, :-1], kp[:, 1:]], axis=2)
    vb = jnp.concatenate([vp[:, :-1], vp[:, 1:]], axis=2)
    logits = jnp.einsum('bnqhgd,bnkhd->bnhgqk', qb, kb).astype(jnp.float32) * scale
    qi = jnp.arange(BLOCK)[:, None]
    ki = jnp.arange(2 * BLOCK)[None, :]
    dist = qi + BLOCK - ki
    blk = jnp.arange(nb)[:, None, None]
    valid = (dist >= 0) & (dist < WINDOW) & (blk * BLOCK - BLOCK + ki >= 0)
    slopes = alibi_slopes(SWA_HEADS).reshape(SWA_KV_HEADS, SWA_GROUP)
    logits = logits - slopes[:, :, None, None] * dist.astype(jnp.float32)
    logits = jnp.where(valid[None, :, None, None], logits, -jnp.inf)
    sink = sinks.astype(jnp.float32).reshape(SWA_KV_HEADS, SWA_GROUP)[:, :, None]
    m = jnp.maximum(logits.max(axis=-1), sink)
    pr = jnp.exp(logits - m[..., None])
    denom = pr.sum(axis=-1) + jnp.exp(sink - m)
    pr = pr / denom[..., None]
    out = jnp.einsum('bnhgqk,bnkhd->bnqhgd', pr.astype(v.dtype), vb)
    return out.reshape(b, s, SWA_Q)


def rwkv7_time_mix(h, mu, w0, w2, a0, a2, g2, k_k, k_a, r_k, ln_w, ln_b):
    b, s, _ = h.shape
    f32 = jnp.float32
    shifted = jnp.pad(h[:, :-1], ((0, 0), (1, 0), (0, 0)))
    h = h + (shifted - h) * mu
    r, k, v, xw, xa, xg = jnp.split(h, RWKV_SPLITS, axis=-1)
    wlog = -jax.nn.softplus(-(w0 + jnp.tanh(xw) @ w2)) - 0.5
    a = jax.nn.sigmoid(a0 + xa @ a2)
    g = jax.nn.sigmoid(xg) @ g2
    hd = lambda t: t.astype(f32).reshape(b, s, RWKV_HEADS, HEAD_DIM)
    r, k, v, wlog, a = hd(r), hd(k), hd(v), hd(wlog), hd(a)
    kk = k * k_k.astype(f32).reshape(RWKV_HEADS, HEAD_DIM)
    kk = kk / jnp.maximum(jnp.sqrt(jnp.sum(kk * kk, axis=-1, keepdims=True)), L2_EPS)
    k = k * (1.0 + (a - 1.0) * k_a.astype(f32).reshape(RWKV_HEADS, HEAD_DIM))
    decay = jnp.exp(-jnp.exp(wlog))

    def step(state, inp):
        r_t, w_t, k_t, v_t, kk_t, a_t = inp
        s_kk = jnp.einsum('bhij,bhj->bhi', state, kk_t)
        state = (state * w_t[:, :, None, :]
                 - s_kk[..., None] * (kk_t * a_t)[:, :, None, :]
                 + v_t[..., None] * k_t[:, :, None, :])
        return state, jnp.einsum('bhij,bhj->bhi', state, r_t)

    xs = tuple(jnp.moveaxis(t, 1, 0) for t in (r, decay, k, v, kk, a))
    state0 = jnp.zeros((b, RWKV_HEADS, HEAD_DIM, HEAD_DIM), f32)
    _, y = lax.scan(step, state0, xs)
    y = jnp.moveaxis(y, 0, 1)
    mean = jnp.mean(y, axis=-1, keepdims=True)
    var = jnp.mean(jnp.square(y - mean), axis=-1, keepdims=True)
    y = ((y - mean) * lax.rsqrt(var + GN_EPS) * ln_w.astype(f32).reshape(RWKV_HEADS, HEAD_DIM)
         + ln_b.astype(f32).reshape(RWKV_HEADS, HEAD_DIM))
    y = y + jnp.sum(r * k * r_k.astype(f32), axis=-1, keepdims=True) * v
    return (y.reshape(b, s, RWKV_DIM) * g.astype(f32)).astype(h.dtype)


def forgetting_attention(q, k, v, log_f):
    b, s = q.shape[:2]
    nb = s // BLOCK
    scale = HEAD_DIM ** -0.5
    c = jnp.moveaxis(jnp.cumsum(log_f, axis=1), 1, 2)
    key_pos = jnp.arange(s)

    def one_block(n):
        start = n * BLOCK
        qb = lax.dynamic_slice_in_dim(q, start, BLOCK, axis=1)
        cq = lax.dynamic_slice_in_dim(c, start, BLOCK, axis=2)
        logits = (jnp.einsum('bqhd,bkhd->bhqk', qb, k).astype(jnp.float32) * scale
                  + cq[..., None] - c[:, :, None, :])
        qpos = start + jnp.arange(BLOCK)
        logits = jnp.where(key_pos[None, :] <= qpos[:, None], logits, -jnp.inf)
        pr = jax.nn.softmax(logits, axis=-1)
        return jnp.einsum('bhqk,bkhd->bqhd', pr.astype(v.dtype), v)

    out = lax.map(one_block, jnp.arange(nb))
    return jnp.moveaxis(out, 0, 1).reshape(b, s, FOX_DIM)


def _fwd_setup_inputs(seed: int = 0) -> dict:
    key = jax.random.key(seed)
    k = jax.random.split(key, 30)
    f32 = jnp.float32
    nrm = lambda kk, shape, scale: jax.random.normal(kk, shape, f32) * scale
    gain = lambda kk, shape: 1.0 + 0.05 * jax.random.normal(kk, shape, f32)
    unif = lambda kk, shape, lo, hi: jax.random.uniform(kk, shape, f32, lo, hi)
    return {
        'x': nrm(k[0], (BATCH, SEQ, D_MODEL), 1.0),
        'p': nrm(k[1], (DEPTH, BATCH, SEQ, PLE_DIM), 1.0),
        'ffn1_norm': gain(k[2], (DEPTH, D_MODEL)),
        'ffn1_w_gu': nrm(k[3], (DEPTH, D_MODEL, 2 * D_FF), D_MODEL ** -0.5),
        'ffn1_w_down': nrm(k[4], (DEPTH, D_FF, D_MODEL), D_FF ** -0.5),
        'mix_norm': gain(k[5], (DEPTH, D_MODEL)),
        'ffn2_norm': gain(k[6], (DEPTH, D_MODEL)),
        'ffn2_w_gu': nrm(k[7], (DEPTH, D_MODEL, 2 * D_FF), D_MODEL ** -0.5),
        'ffn2_w_down': nrm(k[8], (DEPTH, D_FF, D_MODEL), D_FF ** -0.5),
        'ple_norm': gain(k[9], (DEPTH, D_MODEL)),
        'ple_w_gate': nrm(k[10], (DEPTH, D_MODEL, D_MODEL), D_MODEL ** -0.5),
        'ple_w_proj': nrm(k[11], (DEPTH, PLE_DIM, D_MODEL), 0.5 * PLE_DIM ** -0.5),
        'even_w_in': nrm(k[12], (N_EVEN, D_MODEL, EVEN_IN), D_MODEL ** -0.5),
        'even_w_out': nrm(k[13], (N_EVEN, EVEN_OUT, D_MODEL), EVEN_OUT ** -0.5),
        'swa_sinks': nrm(k[14], (N_EVEN, SWA_HEADS), 0.5),
        'rwkv_mu': unif(k[15], (N_EVEN, RWKV_COLS), 0.0, 1.0),
        'rwkv_w0': unif(k[16], (N_EVEN, RWKV_DIM), -5.0, -0.5),
        'rwkv_w2': nrm(k[17], (N_EVEN, DECAY_LORA, RWKV_DIM), 0.5 * DECAY_LORA ** -0.5),
        'rwkv_a0': nrm(k[18], (N_EVEN, RWKV_DIM), 0.1),
        'rwkv_a2': nrm(k[19], (N_EVEN, ICLR_LORA, RWKV_DIM), 0.5 * ICLR_LORA ** -0.5),
        'rwkv_g2': nrm(k[20], (N_EVEN, GATE_LORA, RWKV_DIM), GATE_LORA ** -0.5),
        'rwkv_k_k': 0.85 + nrm(k[21], (N_EVEN, RWKV_DIM), 0.05),
        'rwkv_k_a': 1.0 + nrm(k[22], (N_EVEN, RWKV_DIM), 0.05),
        'rwkv_r_k': nrm(k[23], (N_EVEN, RWKV_HEADS, HEAD_DIM), 0.1),
        'rwkv_ln_w': gain(k[24], (N_EVEN, RWKV_DIM)),
        'rwkv_ln_b': nrm(k[25], (N_EVEN, RWKV_DIM), 0.01),
        'fox_w_in': nrm(k[26], (N_ODD, D_MODEL, FOX_IN), D_MODEL ** -0.5),
        'fox_b_f': unif(k[27], (N_ODD, FOX_HEADS), 1.0, 6.0),
        'fox_w_out': nrm(k[28], (N_ODD, FOX_DIM, D_MODEL), FOX_DIM ** -0.5),
        'final_norm': gain(k[29], (D_MODEL,)),
    }


def _fwd_reference(x, p, ffn1_norm, ffn1_w_gu, ffn1_w_down, mix_norm, ffn2_norm, ffn2_w_gu,
              ffn2_w_down, ple_norm, ple_w_gate, ple_w_proj, even_w_in, even_w_out,
              swa_sinks, rwkv_mu, rwkv_w0, rwkv_w2, rwkv_a0, rwkv_a2, rwkv_g2, rwkv_k_k,
              rwkv_k_a, rwkv_r_k, rwkv_ln_w, rwkv_ln_b, fox_w_in, fox_b_f, fox_w_out,
              final_norm):
    b, s, _ = x.shape
    for i in range(DEPTH):
        j = i // 2
        x = x + 0.5 * swiglu(rms_norm(x, ffn1_norm[i]), ffn1_w_gu[i], ffn1_w_down[i])
        hn = rms_norm(x, mix_norm[i])
        if i % 2 == 0:
            proj = hn @ even_w_in[j]
            qa, ka, va, hb = jnp.split(proj, [SWA_Q, SWA_Q + SWA_KV, SWA_COLS], axis=-1)
            ya = sliding_window_attention(
                qa.reshape(b, s, SWA_HEADS, HEAD_DIM),
                ka.reshape(b, s, SWA_KV_HEADS, HEAD_DIM),
                va.reshape(b, s, SWA_KV_HEADS, HEAD_DIM),
                swa_sinks[j])
            yb = rwkv7_time_mix(hb, rwkv_mu[j], rwkv_w0[j], rwkv_w2[j], rwkv_a0[j], rwkv_a2[j],
                                rwkv_g2[j], rwkv_k_k[j], rwkv_k_a[j], rwkv_r_k[j],
                                rwkv_ln_w[j], rwkv_ln_b[j])
            mixed = jnp.concatenate([ya, yb], axis=-1) @ even_w_out[j]
        else:
            proj = hn @ fox_w_in[j]
            qc, kc, vc, fz = jnp.split(proj, [FOX_DIM, 2 * FOX_DIM, 3 * FOX_DIM], axis=-1)
            log_f = jax.nn.log_sigmoid(fz.astype(jnp.float32) + fox_b_f[j].astype(jnp.float32))
            yc = forgetting_attention(
                qc.reshape(b, s, FOX_HEADS, HEAD_DIM),
                kc.reshape(b, s, FOX_HEADS, HEAD_DIM),
                vc.reshape(b, s, FOX_HEADS, HEAD_DIM),
                log_f)
            mixed = yc @ fox_w_out[j]
        x = x + mixed
        x = x + 0.5 * swiglu(rms_norm(x, ffn2_norm[i]), ffn2_w_gu[i], ffn2_w_down[i])
        gate = jax.nn.sigmoid(rms_norm(x, ple_norm[i]) @ ple_w_gate[i])
        x = x + gate * (p[i] @ ple_w_proj[i])
    return rms_norm(x, final_norm)


import jax as _jax
import jax.numpy as _jnp

TWIN_FORMAT = 'train_step'
FWD_PARAMS = ['x', 'p', 'ffn1_norm', 'ffn1_w_gu', 'ffn1_w_down', 'mix_norm', 'ffn2_norm', 'ffn2_w_gu', 'ffn2_w_down', 'ple_norm', 'ple_w_gate', 'ple_w_proj', 'even_w_in', 'even_w_out', 'swa_sinks', 'rwkv_mu', 'rwkv_w0', 'rwkv_w2', 'rwkv_a0', 'rwkv_a2', 'rwkv_g2', 'rwkv_k_k', 'rwkv_k_a', 'rwkv_r_k', 'rwkv_ln_w', 'rwkv_ln_b', 'fox_w_in', 'fox_b_f', 'fox_w_out', 'final_norm']
TWIN_WEIGHTS = ['ffn1_norm', 'ffn1_w_gu', 'ffn1_w_down', 'mix_norm', 'ffn2_norm', 'ffn2_w_gu', 'ffn2_w_down', 'ple_norm', 'ple_w_gate', 'ple_w_proj', 'even_w_in', 'even_w_out', 'swa_sinks', 'rwkv_mu', 'rwkv_w0', 'rwkv_w2', 'rwkv_a0', 'rwkv_a2', 'rwkv_g2', 'rwkv_k_k', 'rwkv_k_a', 'rwkv_r_k', 'rwkv_ln_w', 'rwkv_ln_b', 'fox_w_in', 'fox_b_f', 'fox_w_out', 'final_norm']
TWIN_DIFF_INPUT = 'x'
TWIN_INPUTS = ['x', 'p', 'ffn1_norm', 'ffn1_w_gu', 'ffn1_w_down', 'mix_norm', 'ffn2_norm', 'ffn2_w_gu', 'ffn2_w_down', 'ple_norm', 'ple_w_gate', 'ple_w_proj', 'even_w_in', 'even_w_out', 'swa_sinks', 'rwkv_mu', 'rwkv_w0', 'rwkv_w2', 'rwkv_a0', 'rwkv_a2', 'rwkv_g2', 'rwkv_k_k', 'rwkv_k_a', 'rwkv_r_k', 'rwkv_ln_w', 'rwkv_ln_b', 'fox_w_in', 'fox_b_f', 'fox_w_out', 'final_norm', 'loss_target', 'm_ffn1_norm', 'm_ffn1_w_gu', 'm_ffn1_w_down', 'm_mix_norm', 'm_ffn2_norm', 'm_ffn2_w_gu', 'm_ffn2_w_down', 'm_ple_norm', 'm_ple_w_gate', 'm_ple_w_proj', 'm_even_w_in', 'm_even_w_out', 'm_swa_sinks', 'm_rwkv_mu', 'm_rwkv_w0', 'm_rwkv_w2', 'm_rwkv_a0', 'm_rwkv_a2', 'm_rwkv_g2', 'm_rwkv_k_k', 'm_rwkv_k_a', 'm_rwkv_r_k', 'm_rwkv_ln_w', 'm_rwkv_ln_b', 'm_fox_w_in', 'm_fox_b_f', 'm_fox_w_out', 'm_final_norm', 'v_ffn1_norm', 'v_ffn1_w_gu', 'v_ffn1_w_down', 'v_mix_norm', 'v_ffn2_norm', 'v_ffn2_w_gu', 'v_ffn2_w_down', 'v_ple_norm', 'v_ple_w_gate', 'v_ple_w_proj', 'v_even_w_in', 'v_even_w_out', 'v_swa_sinks', 'v_rwkv_mu', 'v_rwkv_w0', 'v_rwkv_w2', 'v_rwkv_a0', 'v_rwkv_a2', 'v_rwkv_g2', 'v_rwkv_k_k', 'v_rwkv_k_a', 'v_rwkv_r_k', 'v_rwkv_ln_w', 'v_rwkv_ln_b', 'v_fox_w_in', 'v_fox_b_f', 'v_fox_w_out', 'v_final_norm']
TWIN_OUTPUTS = ['loss', 'grad_x', 'grad_ffn1_norm', 'grad_ffn1_w_gu', 'grad_ffn1_w_down', 'grad_mix_norm', 'grad_ffn2_norm', 'grad_ffn2_w_gu', 'grad_ffn2_w_down', 'grad_ple_norm', 'grad_ple_w_gate', 'grad_ple_w_proj', 'grad_even_w_in', 'grad_even_w_out', 'grad_swa_sinks', 'grad_rwkv_mu', 'grad_rwkv_w0', 'grad_rwkv_w2', 'grad_rwkv_a0', 'grad_rwkv_a2', 'grad_rwkv_g2', 'grad_rwkv_k_k', 'grad_rwkv_k_a', 'grad_rwkv_r_k', 'grad_rwkv_ln_w', 'grad_rwkv_ln_b', 'grad_fox_w_in', 'grad_fox_b_f', 'grad_fox_w_out', 'grad_final_norm', 'delta_ffn1_norm', 'delta_ffn1_w_gu', 'delta_ffn1_w_down', 'delta_mix_norm', 'delta_ffn2_norm', 'delta_ffn2_w_gu', 'delta_ffn2_w_down', 'delta_ple_norm', 'delta_ple_w_gate', 'delta_ple_w_proj', 'delta_even_w_in', 'delta_even_w_out', 'delta_swa_sinks', 'delta_rwkv_mu', 'delta_rwkv_w0', 'delta_rwkv_w2', 'delta_rwkv_a0', 'delta_rwkv_a2', 'delta_rwkv_g2', 'delta_rwkv_k_k', 'delta_rwkv_k_a', 'delta_rwkv_r_k', 'delta_rwkv_ln_w', 'delta_rwkv_ln_b', 'delta_fox_w_in', 'delta_fox_b_f', 'delta_fox_w_out', 'delta_final_norm', 'new_m_ffn1_norm', 'new_m_ffn1_w_gu', 'new_m_ffn1_w_down', 'new_m_mix_norm', 'new_m_ffn2_norm', 'new_m_ffn2_w_gu', 'new_m_ffn2_w_down', 'new_m_ple_norm', 'new_m_ple_w_gate', 'new_m_ple_w_proj', 'new_m_even_w_in', 'new_m_even_w_out', 'new_m_swa_sinks', 'new_m_rwkv_mu', 'new_m_rwkv_w0', 'new_m_rwkv_w2', 'new_m_rwkv_a0', 'new_m_rwkv_a2', 'new_m_rwkv_g2', 'new_m_rwkv_k_k', 'new_m_rwkv_k_a', 'new_m_rwkv_r_k', 'new_m_rwkv_ln_w', 'new_m_rwkv_ln_b', 'new_m_fox_w_in', 'new_m_fox_b_f', 'new_m_fox_w_out', 'new_m_final_norm', 'new_v_ffn1_norm', 'new_v_ffn1_w_gu', 'new_v_ffn1_w_down', 'new_v_mix_norm', 'new_v_ffn2_norm', 'new_v_ffn2_w_gu', 'new_v_ffn2_w_down', 'new_v_ple_norm', 'new_v_ple_w_gate', 'new_v_ple_w_proj', 'new_v_even_w_in', 'new_v_even_w_out', 'new_v_swa_sinks', 'new_v_rwkv_mu', 'new_v_rwkv_w0', 'new_v_rwkv_w2', 'new_v_rwkv_a0', 'new_v_rwkv_a2', 'new_v_rwkv_g2', 'new_v_rwkv_k_k', 'new_v_rwkv_k_a', 'new_v_rwkv_r_k', 'new_v_rwkv_ln_w', 'new_v_rwkv_ln_b', 'new_v_fox_w_in', 'new_v_fox_b_f', 'new_v_fox_w_out', 'new_v_final_norm']
TWIN_LEAF_KINDS = {'loss': 'loss', 'grad_x': 'grad_x', 'grad_ffn1_norm': 'grad_w', 'grad_ffn1_w_gu': 'grad_w', 'grad_ffn1_w_down': 'grad_w', 'grad_mix_norm': 'grad_w', 'grad_ffn2_norm': 'grad_w', 'grad_ffn2_w_gu': 'grad_w', 'grad_ffn2_w_down': 'grad_w', 'grad_ple_norm': 'grad_w', 'grad_ple_w_gate': 'grad_w', 'grad_ple_w_proj': 'grad_w', 'grad_even_w_in': 'grad_w', 'grad_even_w_out': 'grad_w', 'grad_swa_sinks': 'grad_w', 'grad_rwkv_mu': 'grad_w', 'grad_rwkv_w0': 'grad_w', 'grad_rwkv_w2': 'grad_w', 'grad_rwkv_a0': 'grad_w', 'grad_rwkv_a2': 'grad_w', 'grad_rwkv_g2': 'grad_w', 'grad_rwkv_k_k': 'grad_w', 'grad_rwkv_k_a': 'grad_w', 'grad_rwkv_r_k': 'grad_w', 'grad_rwkv_ln_w': 'grad_w', 'grad_rwkv_ln_b': 'grad_w', 'grad_fox_w_in': 'grad_w', 'grad_fox_b_f': 'grad_w', 'grad_fox_w_out': 'grad_w', 'grad_final_norm': 'grad_w', 'delta_ffn1_norm': 'delta_w', 'delta_ffn1_w_gu': 'delta_w', 'delta_ffn1_w_down': 'delta_w', 'delta_mix_norm': 'delta_w', 'delta_ffn2_norm': 'delta_w', 'delta_ffn2_w_gu': 'delta_w', 'delta_ffn2_w_down': 'delta_w', 'delta_ple_norm': 'delta_w', 'delta_ple_w_gate': 'delta_w', 'delta_ple_w_proj': 'delta_w', 'delta_even_w_in': 'delta_w', 'delta_even_w_out': 'delta_w', 'delta_swa_sinks': 'delta_w', 'delta_rwkv_mu': 'delta_w', 'delta_rwkv_w0': 'delta_w', 'delta_rwkv_w2': 'delta_w', 'delta_rwkv_a0': 'delta_w', 'delta_rwkv_a2': 'delta_w', 'delta_rwkv_g2': 'delta_w', 'delta_rwkv_k_k': 'delta_w', 'delta_rwkv_k_a': 'delta_w', 'delta_rwkv_r_k': 'delta_w', 'delta_rwkv_ln_w': 'delta_w', 'delta_rwkv_ln_b': 'delta_w', 'delta_fox_w_in': 'delta_w', 'delta_fox_b_f': 'delta_w', 'delta_fox_w_out': 'delta_w', 'delta_final_norm': 'delta_w', 'new_m_ffn1_norm': 'new_m', 'new_m_ffn1_w_gu': 'new_m', 'new_m_ffn1_w_down': 'new_m', 'new_m_mix_norm': 'new_m', 'new_m_ffn2_norm': 'new_m', 'new_m_ffn2_w_gu': 'new_m', 'new_m_ffn2_w_down': 'new_m', 'new_m_ple_norm': 'new_m', 'new_m_ple_w_gate': 'new_m', 'new_m_ple_w_proj': 'new_m', 'new_m_even_w_in': 'new_m', 'new_m_even_w_out': 'new_m', 'new_m_swa_sinks': 'new_m', 'new_m_rwkv_mu': 'new_m', 'new_m_rwkv_w0': 'new_m', 'new_m_rwkv_w2': 'new_m', 'new_m_rwkv_a0': 'new_m', 'new_m_rwkv_a2': 'new_m', 'new_m_rwkv_g2': 'new_m', 'new_m_rwkv_k_k': 'new_m', 'new_m_rwkv_k_a': 'new_m', 'new_m_rwkv_r_k': 'new_m', 'new_m_rwkv_ln_w': 'new_m', 'new_m_rwkv_ln_b': 'new_m', 'new_m_fox_w_in': 'new_m', 'new_m_fox_b_f': 'new_m', 'new_m_fox_w_out': 'new_m', 'new_m_final_norm': 'new_m', 'new_v_ffn1_norm': 'new_v', 'new_v_ffn1_w_gu': 'new_v', 'new_v_ffn1_w_down': 'new_v', 'new_v_mix_norm': 'new_v', 'new_v_ffn2_norm': 'new_v', 'new_v_ffn2_w_gu': 'new_v', 'new_v_ffn2_w_down': 'new_v', 'new_v_ple_norm': 'new_v', 'new_v_ple_w_gate': 'new_v', 'new_v_ple_w_proj': 'new_v', 'new_v_even_w_in': 'new_v', 'new_v_even_w_out': 'new_v', 'new_v_swa_sinks': 'new_v', 'new_v_rwkv_mu': 'new_v', 'new_v_rwkv_w0': 'new_v', 'new_v_rwkv_w2': 'new_v', 'new_v_rwkv_a0': 'new_v', 'new_v_rwkv_a2': 'new_v', 'new_v_rwkv_g2': 'new_v', 'new_v_rwkv_k_k': 'new_v', 'new_v_rwkv_k_a': 'new_v', 'new_v_rwkv_r_k': 'new_v', 'new_v_rwkv_ln_w': 'new_v', 'new_v_rwkv_ln_b': 'new_v', 'new_v_fox_w_in': 'new_v', 'new_v_fox_b_f': 'new_v', 'new_v_fox_w_out': 'new_v', 'new_v_final_norm': 'new_v'}


def _forward(args):
    return _fwd_reference(*[args[k] for k in FWD_PARAMS])


def _output_shape():
    def fwd():
        inp = _fwd_setup_inputs(0)
        return _fwd_reference(*[inp[k] for k in FWD_PARAMS])
    out = _jax.eval_shape(fwd)
    return out.shape, out.dtype

N_MICROBATCH = 1
ADAM_LR = 0.001
ADAM_B1 = 0.9
ADAM_B2 = 0.999
ADAM_EPS = 1e-08
ADAM_WD = 0.01
ADAM_STEP = 10
PER_EXAMPLE_BATCH_AXIS = {'x': 0, 'p': 1, 'loss_target': 0}
SHARED_INPUTS = []
_WEIGHT_DTYPES = {'ffn1_norm': _jnp.float32, 'ffn1_w_gu': _jnp.float32, 'ffn1_w_down': _jnp.float32, 'mix_norm': _jnp.float32, 'ffn2_norm': _jnp.float32, 'ffn2_w_gu': _jnp.float32, 'ffn2_w_down': _jnp.float32, 'ple_norm': _jnp.float32, 'ple_w_gate': _jnp.float32, 'ple_w_proj': _jnp.float32, 'even_w_in': _jnp.float32, 'even_w_out': _jnp.float32, 'swa_sinks': _jnp.float32, 'rwkv_mu': _jnp.float32, 'rwkv_w0': _jnp.float32, 'rwkv_w2': _jnp.float32, 'rwkv_a0': _jnp.float32, 'rwkv_a2': _jnp.float32, 'rwkv_g2': _jnp.float32, 'rwkv_k_k': _jnp.float32, 'rwkv_k_a': _jnp.float32, 'rwkv_r_k': _jnp.float32, 'rwkv_ln_w': _jnp.float32, 'rwkv_ln_b': _jnp.float32, 'fox_w_in': _jnp.float32, 'fox_b_f': _jnp.float32, 'fox_w_out': _jnp.float32, 'final_norm': _jnp.float32}
MOMENT_SCALE = {'ffn1_norm': 7.959767e-02, 'ffn1_w_gu': 3.402264e-02, 'ffn1_w_down': 5.545018e-02, 'mix_norm': 1.091191e-01, 'ffn2_norm': 6.267011e-02, 'ffn2_w_gu': 2.706399e-02, 'ffn2_w_down': 4.431986e-02, 'ple_norm': 1.555282e-02, 'ple_w_gate': 1.554133e-02, 'ple_w_proj': 8.000625e-02, 'even_w_in': 8.722213e-02, 'even_w_out': 7.837723e-02, 'swa_sinks': 5.645719e-02, 'rwkv_mu': 1.525013e-01, 'rwkv_w0': 4.606317e-02, 'rwkv_w2': 5.792709e-03, 'rwkv_a0': 4.005328e-02, 'rwkv_a2': 3.774783e-02, 'rwkv_g2': 1.003692e-01, 'rwkv_k_k': 1.029260e-01, 'rwkv_k_a': 1.007291e-01, 'rwkv_r_k': 2.205637e-01, 'rwkv_ln_w': 1.175452e-01, 'rwkv_ln_b': 1.204079e-01, 'fox_w_in': 4.024675e-02, 'fox_b_f': 1.933428e-01, 'fox_w_out': 4.795258e-02, 'final_norm': 3.201645e+01}


def _to_microbatches(a, axis):
    t = _jnp.moveaxis(a, axis, 0)
    t = t.reshape((N_MICROBATCH, t.shape[0] // N_MICROBATCH) + t.shape[1:])
    return _jnp.moveaxis(t, 1, axis + 1)


def setup_inputs(seed: int = 0) -> dict:
    inp = _fwd_setup_inputs(seed)
    key = _jax.random.fold_in(_jax.random.key(seed), 7919)
    shape, _ = _output_shape()
    out = dict(inp)
    out["loss_target"] = _jax.random.normal(_jax.random.fold_in(key, 0), shape, _jnp.float32)
    for i, name in enumerate(TWIN_WEIGHTS):
        w = inp[name].astype(_jnp.float32)
        if MOMENT_SCALE is None:
            s = _jnp.sqrt(_jnp.mean(_jnp.square(w)) + 1e-30)
        else:
            s = MOMENT_SCALE[name]
        km, kv = _jax.random.split(_jax.random.fold_in(key, i + 1))
        out[name] = w
        out["m_" + name] = s * _jax.random.normal(km, w.shape, _jnp.float32)
        out["v_" + name] = (s * s) * _jax.random.uniform(kv, w.shape, _jnp.float32, 0.5, 1.5)
    if N_MICROBATCH > 1:
        for name, axis in PER_EXAMPLE_BATCH_AXIS.items():
            out[name] = _to_microbatches(out[name], axis)
    return {'x': out['x'], 'p': out['p'], 'ffn1_norm': out['ffn1_norm'], 'ffn1_w_gu': out['ffn1_w_gu'], 'ffn1_w_down': out['ffn1_w_down'], 'mix_norm': out['mix_norm'], 'ffn2_norm': out['ffn2_norm'], 'ffn2_w_gu': out['ffn2_w_gu'], 'ffn2_w_down': out['ffn2_w_down'], 'ple_norm': out['ple_norm'], 'ple_w_gate': out['ple_w_gate'], 'ple_w_proj': out['ple_w_proj'], 'even_w_in': out['even_w_in'], 'even_w_out': out['even_w_out'], 'swa_sinks': out['swa_sinks'], 'rwkv_mu': out['rwkv_mu'], 'rwkv_w0': out['rwkv_w0'], 'rwkv_w2': out['rwkv_w2'], 'rwkv_a0': out['rwkv_a0'], 'rwkv_a2': out['rwkv_a2'], 'rwkv_g2': out['rwkv_g2'], 'rwkv_k_k': out['rwkv_k_k'], 'rwkv_k_a': out['rwkv_k_a'], 'rwkv_r_k': out['rwkv_r_k'], 'rwkv_ln_w': out['rwkv_ln_w'], 'rwkv_ln_b': out['rwkv_ln_b'], 'fox_w_in': out['fox_w_in'], 'fox_b_f': out['fox_b_f'], 'fox_w_out': out['fox_w_out'], 'final_norm': out['final_norm'], 'loss_target': out['loss_target'], 'm_ffn1_norm': out['m_ffn1_norm'], 'm_ffn1_w_gu': out['m_ffn1_w_gu'], 'm_ffn1_w_down': out['m_ffn1_w_down'], 'm_mix_norm': out['m_mix_norm'], 'm_ffn2_norm': out['m_ffn2_norm'], 'm_ffn2_w_gu': out['m_ffn2_w_gu'], 'm_ffn2_w_down': out['m_ffn2_w_down'], 'm_ple_norm': out['m_ple_norm'], 'm_ple_w_gate': out['m_ple_w_gate'], 'm_ple_w_proj': out['m_ple_w_proj'], 'm_even_w_in': out['m_even_w_in'], 'm_even_w_out': out['m_even_w_out'], 'm_swa_sinks': out['m_swa_sinks'], 'm_rwkv_mu': out['m_rwkv_mu'], 'm_rwkv_w0': out['m_rwkv_w0'], 'm_rwkv_w2': out['m_rwkv_w2'], 'm_rwkv_a0': out['m_rwkv_a0'], 'm_rwkv_a2': out['m_rwkv_a2'], 'm_rwkv_g2': out['m_rwkv_g2'], 'm_rwkv_k_k': out['m_rwkv_k_k'], 'm_rwkv_k_a': out['m_rwkv_k_a'], 'm_rwkv_r_k': out['m_rwkv_r_k'], 'm_rwkv_ln_w': out['m_rwkv_ln_w'], 'm_rwkv_ln_b': out['m_rwkv_ln_b'], 'm_fox_w_in': out['m_fox_w_in'], 'm_fox_b_f': out['m_fox_b_f'], 'm_fox_w_out': out['m_fox_w_out'], 'm_final_norm': out['m_final_norm'], 'v_ffn1_norm': out['v_ffn1_norm'], 'v_ffn1_w_gu': out['v_ffn1_w_gu'], 'v_ffn1_w_down': out['v_ffn1_w_down'], 'v_mix_norm': out['v_mix_norm'], 'v_ffn2_norm': out['v_ffn2_norm'], 'v_ffn2_w_gu': out['v_ffn2_w_gu'], 'v_ffn2_w_down': out['v_ffn2_w_down'], 'v_ple_norm': out['v_ple_norm'], 'v_ple_w_gate': out['v_ple_w_gate'], 'v_ple_w_proj': out['v_ple_w_proj'], 'v_even_w_in': out['v_even_w_in'], 'v_even_w_out': out['v_even_w_out'], 'v_swa_sinks': out['v_swa_sinks'], 'v_rwkv_mu': out['v_rwkv_mu'], 'v_rwkv_w0': out['v_rwkv_w0'], 'v_rwkv_w2': out['v_rwkv_w2'], 'v_rwkv_a0': out['v_rwkv_a0'], 'v_rwkv_a2': out['v_rwkv_a2'], 'v_rwkv_g2': out['v_rwkv_g2'], 'v_rwkv_k_k': out['v_rwkv_k_k'], 'v_rwkv_k_a': out['v_rwkv_k_a'], 'v_rwkv_r_k': out['v_rwkv_r_k'], 'v_rwkv_ln_w': out['v_rwkv_ln_w'], 'v_rwkv_ln_b': out['v_rwkv_ln_b'], 'v_fox_w_in': out['v_fox_w_in'], 'v_fox_b_f': out['v_fox_b_f'], 'v_fox_w_out': out['v_fox_w_out'], 'v_final_norm': out['v_final_norm']}


def _loss(weights, diff, rest, loss_target):
    with _jax.named_scope("forward"):
        args = {**rest, TWIN_DIFF_INPUT: diff, **{k: w.astype(_WEIGHT_DTYPES[k]) for k, w in weights.items()}}
        y = _forward(args)
    with _jax.named_scope("loss_head"):
        err = _jnp.square(y.astype(_jnp.float32) - loss_target)
        return 0.5 * _jnp.sum(_jnp.mean(err, axis=-1)) if err.ndim else 0.5 * err


def _adamw(w, g, m, v):
    m = ADAM_B1 * m + (1.0 - ADAM_B1) * g
    v = ADAM_B2 * v + (1.0 - ADAM_B2) * _jnp.square(g)
    m_hat = m / (1.0 - ADAM_B1 ** ADAM_STEP)
    v_hat = v / (1.0 - ADAM_B2 ** ADAM_STEP)
    delta = -ADAM_LR * (m_hat / (_jnp.sqrt(v_hat) + ADAM_EPS) + ADAM_WD * w)
    return delta, m, v


def reference(x, p, ffn1_norm, ffn1_w_gu, ffn1_w_down, mix_norm, ffn2_norm, ffn2_w_gu, ffn2_w_down, ple_norm, ple_w_gate, ple_w_proj, even_w_in, even_w_out, swa_sinks, rwkv_mu, rwkv_w0, rwkv_w2, rwkv_a0, rwkv_a2, rwkv_g2, rwkv_k_k, rwkv_k_a, rwkv_r_k, rwkv_ln_w, rwkv_ln_b, fox_w_in, fox_b_f, fox_w_out, final_norm, loss_target, m_ffn1_norm, m_ffn1_w_gu, m_ffn1_w_down, m_mix_norm, m_ffn2_norm, m_ffn2_w_gu, m_ffn2_w_down, m_ple_norm, m_ple_w_gate, m_ple_w_proj, m_even_w_in, m_even_w_out, m_swa_sinks, m_rwkv_mu, m_rwkv_w0, m_rwkv_w2, m_rwkv_a0, m_rwkv_a2, m_rwkv_g2, m_rwkv_k_k, m_rwkv_k_a, m_rwkv_r_k, m_rwkv_ln_w, m_rwkv_ln_b, m_fox_w_in, m_fox_b_f, m_fox_w_out, m_final_norm, v_ffn1_norm, v_ffn1_w_gu, v_ffn1_w_down, v_mix_norm, v_ffn2_norm, v_ffn2_w_gu, v_ffn2_w_down, v_ple_norm, v_ple_w_gate, v_ple_w_proj, v_even_w_in, v_even_w_out, v_swa_sinks, v_rwkv_mu, v_rwkv_w0, v_rwkv_w2, v_rwkv_a0, v_rwkv_a2, v_rwkv_g2, v_rwkv_k_k, v_rwkv_k_a, v_rwkv_r_k, v_rwkv_ln_w, v_rwkv_ln_b, v_fox_w_in, v_fox_b_f, v_fox_w_out, v_final_norm):
    given = dict(x=x, p=p, ffn1_norm=ffn1_norm, ffn1_w_gu=ffn1_w_gu, ffn1_w_down=ffn1_w_down, mix_norm=mix_norm, ffn2_norm=ffn2_norm, ffn2_w_gu=ffn2_w_gu, ffn2_w_down=ffn2_w_down, ple_norm=ple_norm, ple_w_gate=ple_w_gate, ple_w_proj=ple_w_proj, even_w_in=even_w_in, even_w_out=even_w_out, swa_sinks=swa_sinks, rwkv_mu=rwkv_mu, rwkv_w0=rwkv_w0, rwkv_w2=rwkv_w2, rwkv_a0=rwkv_a0, rwkv_a2=rwkv_a2, rwkv_g2=rwkv_g2, rwkv_k_k=rwkv_k_k, rwkv_k_a=rwkv_k_a, rwkv_r_k=rwkv_r_k, rwkv_ln_w=rwkv_ln_w, rwkv_ln_b=rwkv_ln_b, fox_w_in=fox_w_in, fox_b_f=fox_b_f, fox_w_out=fox_w_out, final_norm=final_norm, loss_target=loss_target, m_ffn1_norm=m_ffn1_norm, m_ffn1_w_gu=m_ffn1_w_gu, m_ffn1_w_down=m_ffn1_w_down, m_mix_norm=m_mix_norm, m_ffn2_norm=m_ffn2_norm, m_ffn2_w_gu=m_ffn2_w_gu, m_ffn2_w_down=m_ffn2_w_down, m_ple_norm=m_ple_norm, m_ple_w_gate=m_ple_w_gate, m_ple_w_proj=m_ple_w_proj, m_even_w_in=m_even_w_in, m_even_w_out=m_even_w_out, m_swa_sinks=m_swa_sinks, m_rwkv_mu=m_rwkv_mu, m_rwkv_w0=m_rwkv_w0, m_rwkv_w2=m_rwkv_w2, m_rwkv_a0=m_rwkv_a0, m_rwkv_a2=m_rwkv_a2, m_rwkv_g2=m_rwkv_g2, m_rwkv_k_k=m_rwkv_k_k, m_rwkv_k_a=m_rwkv_k_a, m_rwkv_r_k=m_rwkv_r_k, m_rwkv_ln_w=m_rwkv_ln_w, m_rwkv_ln_b=m_rwkv_ln_b, m_fox_w_in=m_fox_w_in, m_fox_b_f=m_fox_b_f, m_fox_w_out=m_fox_w_out, m_final_norm=m_final_norm, v_ffn1_norm=v_ffn1_norm, v_ffn1_w_gu=v_ffn1_w_gu, v_ffn1_w_down=v_ffn1_w_down, v_mix_norm=v_mix_norm, v_ffn2_norm=v_ffn2_norm, v_ffn2_w_gu=v_ffn2_w_gu, v_ffn2_w_down=v_ffn2_w_down, v_ple_norm=v_ple_norm, v_ple_w_gate=v_ple_w_gate, v_ple_w_proj=v_ple_w_proj, v_even_w_in=v_even_w_in, v_even_w_out=v_even_w_out, v_swa_sinks=v_swa_sinks, v_rwkv_mu=v_rwkv_mu, v_rwkv_w0=v_rwkv_w0, v_rwkv_w2=v_rwkv_w2, v_rwkv_a0=v_rwkv_a0, v_rwkv_a2=v_rwkv_a2, v_rwkv_g2=v_rwkv_g2, v_rwkv_k_k=v_rwkv_k_k, v_rwkv_k_a=v_rwkv_k_a, v_rwkv_r_k=v_rwkv_r_k, v_rwkv_ln_w=v_rwkv_ln_w, v_rwkv_ln_b=v_rwkv_ln_b, v_fox_w_in=v_fox_w_in, v_fox_b_f=v_fox_b_f, v_fox_w_out=v_fox_w_out, v_final_norm=v_final_norm)
    weights = {n: given[n] for n in TWIN_WEIGHTS}
    shared = {n: given[n] for n in SHARED_INPUTS}
    per_example = {n: given[n] for n in ['x', 'p']}
    grad_fn = _jax.value_and_grad(_loss, argnums=(0, 1))

    def one_microbatch(ex, loss_target):
        ex = dict(ex)
        diff = ex.pop(TWIN_DIFF_INPUT)
        return grad_fn(weights, diff, {**shared, **ex}, loss_target)

    if N_MICROBATCH == 1:
        loss, (grad_w, grad_x) = one_microbatch(per_example, given["loss_target"])
    else:
        def body(carry, xs):
            loss_sum, grad_sum = carry
            l_k, (gw_k, gx_k) = one_microbatch(xs[0], xs[1])
            with _jax.named_scope("update"):
                return (loss_sum + l_k, _jax.tree.map(_jnp.add, grad_sum, gw_k)), gx_k

        init = (_jnp.zeros((), _jnp.float32), _jax.tree.map(_jnp.zeros_like, weights))
        (loss, grad_w), grad_x = _jax.lax.scan(body, init, (per_example, given["loss_target"]))
    with _jax.named_scope("update"):
        delta_w, new_m, new_v = {}, {}, {}
        for n in TWIN_WEIGHTS:
            delta_w[n], new_m[n], new_v[n] = _adamw(weights[n], grad_w[n], given["m_" + n], given["v_" + n])
    return (loss, grad_x, *[grad_w[n] for n in TWIN_WEIGHTS], *[delta_w[n] for n in TWIN_WEIGHTS],
            *[new_m[n] for n in TWIN_WEIGHTS], *[new_v[n] for n in TWIN_WEIGHTS])
```

```python
import functools
import math

import jax
import jax.numpy as jnp
from jax import lax
from jax.experimental import pallas as pl
from jax.experimental.pallas import tpu as pltpu

F32 = jnp.float32
BF16 = jnp.bfloat16
MESH = pl.DeviceIdType.MESH

HEAD = 64
SWA_BLOCK = 128
SWA_GROUP = 4
DECAY_LORA = 64
ICLR_LORA = 64
GATE_LORA = 128
NORM_EPS = 1e-6
GN_EPS = 64e-5
L2_EPS = 1e-12
NEG = -1e30

ADAM_LR = 0.001
ADAM_B1 = 0.9
ADAM_B2 = 0.999
ADAM_EPS = 1e-08
ADAM_WD = 0.01
ADAM_STEP = 10

VMEM_LIMIT = 48 * 1024 * 1024
LANES = 128
SEG = 256
MM_TILE = 1408

WEIGHTS = ['ffn1_norm', 'ffn1_w_gu', 'ffn1_w_down', 'mix_norm', 'ffn2_norm', 'ffn2_w_gu', 'ffn2_w_down',
           'ple_norm', 'ple_w_gate', 'ple_w_proj', 'even_w_in', 'even_w_out', 'swa_sinks', 'rwkv_mu',
           'rwkv_w0', 'rwkv_w2', 'rwkv_a0', 'rwkv_a2', 'rwkv_g2', 'rwkv_k_k', 'rwkv_k_a', 'rwkv_r_k',
           'rwkv_ln_w', 'rwkv_ln_b', 'fox_w_in', 'fox_b_f', 'fox_w_out', 'final_norm']
SHARDED = {'ffn1_w_gu': 2, 'ffn1_w_down': 1, 'ffn2_w_gu': 2, 'ffn2_w_down': 1, 'ple_w_gate': 1,
           'ple_w_proj': 2, 'even_w_in': 2, 'even_w_out': 1, 'fox_w_in': 2, 'fox_w_out': 1,
           'rwkv_w2': 2, 'rwkv_a2': 2, 'rwkv_g2': 2}
LORA = ['rwkv_w2', 'rwkv_a2', 'rwkv_g2']
BIG = [n for n in WEIGHTS if n in SHARDED and n not in LORA]
SMALL = [n for n in WEIGHTS if n not in SHARDED]
N_CHIPS = 4
N_DEV = 8


def _tile(dim, target, align):
    best = None
    t = align
    while t <= min(dim, target):
        if dim % t == 0:
            best = t
        t += align
    return best if best is not None else dim


def _params(sem=None):
    return pltpu.CompilerParams(dimension_semantics=sem, vmem_limit_bytes=VMEM_LIMIT)


def _matmul(a, b, *, ta=False, tb=False, alpha=1.0, res=None, out_dtype=F32, name):
    if ta:
        kdim, m = a.shape
    else:
        m, kdim = a.shape
    if tb:
        n, kb = b.shape
    else:
        kb, n = b.shape
    assert kdim == kb, (a.shape, b.shape, ta, tb)
    tm = _tile(m, MM_TILE, 128 if ta else 16)
    tn = _tile(n, MM_TILE, 128)
    tk = _tile(kdim, MM_TILE, 128)
    nk = kdim // tk
    a_spec = pl.BlockSpec((tk, tm), lambda j, i, k: (k, i)) if ta else pl.BlockSpec((tm, tk), lambda j, i, k: (i, k))
    b_spec = pl.BlockSpec((tn, tk), lambda j, i, k: (j, k)) if tb else pl.BlockSpec((tk, tn), lambda j, i, k: (k, j))
    o_spec = pl.BlockSpec((tm, tn), lambda j, i, k: (i, j))
    dims = (((0 if ta else 1,), (1 if tb else 0,)), ((), ()))
    has_res = res is not None

    def finish(acc, r_ref, o_ref):
        o = acc * alpha
        if has_res:
            o = o + r_ref[...]
        o_ref[...] = o.astype(out_dtype)

    def product(a_ref, b_ref):
        return lax.dot_general(a_ref[...].astype(BF16), b_ref[...].astype(BF16), dims, preferred_element_type=F32)

    def body_one(*refs):
        a_ref, b_ref = refs[:2]
        finish(product(a_ref, b_ref), refs[2] if has_res else None, refs[-1])

    def body_acc(*refs):
        a_ref, b_ref = refs[:2]
        o_ref, acc_ref = refs[-2:]
        k = pl.program_id(2)

        @pl.when(k == 0)
        def _():
            acc_ref[...] = jnp.zeros_like(acc_ref)

        acc_ref[...] += product(a_ref, b_ref)

        @pl.when(k == nk - 1)
        def _():
            finish(acc_ref[...], refs[2] if has_res else None, o_ref)

    ins = [a, b] + ([res] if has_res else [])
    in_specs = [a_spec, b_spec] + ([o_spec] if has_res else [])
    return pl.pallas_call(
        body_one if nk == 1 else body_acc, name=name, grid=(n // tn, m // tm, nk), in_specs=in_specs, out_specs=o_spec,
        out_shape=jax.ShapeDtypeStruct((m, n), out_dtype),
        scratch_shapes=[] if nk == 1 else [pltpu.VMEM((tm, tn), F32)],
        compiler_params=_params(("parallel", "parallel", "arbitrary")),
    )(*ins)


def _rowwise(fn, tiled, full, tiled_out, acc_out, *, tm, name):
    rows = tiled[0].shape[0]
    tm = _tile(rows, tm, 16)
    nt, nf, no, na = len(tiled), len(full), len(tiled_out), len(acc_out)

    def body(*refs):
        ins = [r[...] for r in refs[:nt + nf]]
        outs = fn(*ins)
        if not isinstance(outs, (tuple, list)):
            outs = (outs,)
        assert len(outs) == no + na, (name, len(outs))
        for r, o in zip(refs[nt + nf:nt + nf + no], outs[:no]):
            r[...] = o.astype(r.dtype)
        if na:
            first = pl.program_id(0) == 0
            for r, o in zip(refs[nt + nf + no:], outs[no:]):
                @pl.when(first)
                def _(r=r):
                    r[...] = jnp.zeros_like(r)
                r[...] += o.astype(F32)

    def whole(shape):
        nd = len(shape)
        return pl.BlockSpec(tuple(shape), lambda i, nd=nd: (0,) * nd)

    in_specs = [pl.BlockSpec((tm, t.shape[1]), lambda i: (i, 0)) for t in tiled] + [whole(f.shape) for f in full]
    out_specs = [pl.BlockSpec((tm, w), lambda i: (i, 0)) for w, _ in tiled_out] + [whole(s) for s in acc_out]
    out_shape = [jax.ShapeDtypeStruct((rows, w), d) for w, d in tiled_out] + [jax.ShapeDtypeStruct(tuple(s), F32) for s in acc_out]
    res = pl.pallas_call(
        body, name=name, grid=(rows // tm,), in_specs=in_specs, out_specs=out_specs, out_shape=out_shape,
        compiler_params=_params(("arbitrary",) if na else ("parallel",)),
    )(*tiled, *full)
    return res


def _sigmoid(x):
    return 1.0 / (1.0 + jnp.exp(-x))


def _rms_fwd(x, g, name):
    def fn(x, g):
        return x * lax.rsqrt(jnp.mean(x * x, axis=-1, keepdims=True) + NORM_EPS) * g
    return _rowwise(fn, [x], [g], [(x.shape[1], BF16)], [], tm=512, name=name)[0]


def _rms_bwd_math(x, g, dh):
    rstd = lax.rsqrt(jnp.mean(x * x, axis=-1, keepdims=True) + NORM_EPS)
    xhat = x * rstd
    dxhat = dh * g
    dx = rstd * (dxhat - xhat * jnp.mean(dxhat * xhat, axis=-1, keepdims=True))
    dg = jnp.sum(dh * xhat, axis=0, keepdims=True)
    return dx, dg


def _rms_bwd(x, g, dh, dx_in, name):
    def fn(x, dh, dx_in, g):
        dx, dg = _rms_bwd_math(x, g, dh)
        return dx_in + dx, dg
    d = x.shape[1]
    return _rowwise(fn, [x, dh, dx_in], [g], [(d, F32)], [(1, d)], tm=256, name=name)


def _swiglu_fwd(gu, name):
    f = gu.shape[1] // 2

    def fn(gu):
        g, u = gu[:, :f], gu[:, f:]
        return g * _sigmoid(g) * u
    return _rowwise(fn, [gu], [], [(f, BF16)], [], tm=256, name=name)[0]


def _swiglu_bwd(gu, dact, name):
    f = gu.shape[1] // 2

    def fn(gu, dact):
        g, u = gu[:, :f], gu[:, f:]
        s = _sigmoid(g)
        dg = dact * u * (s * (1.0 + g * (1.0 - s)))
        du = dact * (g * s)
        return jnp.concatenate([dg, du], axis=1)
    return _rowwise(fn, [gu, dact], [], [(2 * f, BF16)], [], tm=256, name=name)[0]


def _ffn_fwd(x, g, w_gu, w_down, tag):
    h = _rms_fwd(x, g, f"{tag}_rms")
    gu = _matmul(h, w_gu, name=f"{tag}_gu")
    act = _swiglu_fwd(gu, f"{tag}_act")
    x2 = _matmul(act, w_down, alpha=0.5, res=x, name=f"{tag}_down")
    return x2, (x, h, gu, act)


def _ffn_bwd(dx2, saved, g, w_gu, w_down, tag):
    x, h, gu, act = saved
    dact = _matmul(dx2, w_down, tb=True, alpha=0.5, name=f"{tag}_dact")
    d_down = _matmul(act, dx2, ta=True, alpha=0.5, name=f"{tag}_dwdown")
    dgu = _swiglu_bwd(gu, dact, f"{tag}_dgu")
    d_gu = _matmul(h, dgu, ta=True, name=f"{tag}_dwgu")
    dh = _matmul(dgu, w_gu, tb=True, name=f"{tag}_dh")
    dx, dg = _rms_bwd(x, g, dh, dx2, f"{tag}_drms")
    return dx, dg, d_gu, d_down


def _ple_fwd(x, g, w_gate, p, w_proj, tag):
    h = _rms_fwd(x, g, f"{tag}_rms")
    z = _matmul(h, w_gate, name=f"{tag}_gate")
    pp = _matmul(p, w_proj, name=f"{tag}_proj")
    d = x.shape[1]
    x2 = _rowwise(lambda x, z, pp: x + _sigmoid(z) * pp, [x, z, pp], [], [(d, F32)], [], tm=512, name=f"{tag}_comb")[0]
    return x2, (x, h, z, pp)


def _ple_bwd(dx2, saved, g, w_gate, p, tag):
    x, h, z, pp = saved
    d = x.shape[1]

    def fn(dx2, z, pp):
        s = _sigmoid(z)
        return dx2 * pp * s * (1.0 - s), dx2 * s
    dz, dpp = _rowwise(fn, [dx2, z, pp], [], [(d, BF16), (d, BF16)], [], tm=512, name=f"{tag}_dcomb")
    d_gate = _matmul(h, dz, ta=True, name=f"{tag}_dwgate")
    d_proj = _matmul(p, dpp, ta=True, name=f"{tag}_dwproj")
    dh = _matmul(dz, w_gate, tb=True, name=f"{tag}_dh")
    dx, dg = _rms_bwd(x, g, dh, dx2, f"{tag}_drms")
    return dx, dg, d_gate, d_proj


def _final_loss(x, g, tgt):
    d = x.shape[1]

    def fn(x, tgt, g):
        rstd = lax.rsqrt(jnp.mean(x * x, axis=-1, keepdims=True) + NORM_EPS)
        err = x * rstd * g - tgt
        loss = 0.5 * jnp.sum(jnp.mean(err * err, axis=-1, keepdims=True), axis=0, keepdims=True)
        dx, dg = _rms_bwd_math(x, g, err * (1.0 / d))
        return dx, jnp.zeros((8, LANES), F32) + loss, dg
    return _rowwise(fn, [x, tgt], [g], [(d, F32)], [(8, LANES), (1, d)], tm=256, name="final_loss")


def _swa_masks(n):
    qi = lax.broadcasted_iota(jnp.int32, (SWA_BLOCK, 2 * SWA_BLOCK), 0)
    ki = lax.broadcasted_iota(jnp.int32, (SWA_BLOCK, 2 * SWA_BLOCK), 1)
    dist = qi + SWA_BLOCK - ki
    valid = (dist >= 0) & (dist < SWA_BLOCK) & ((ki >= SWA_BLOCK) | (n > 0))
    return dist.astype(F32), valid


def _dot_nt(a, b):
    return lax.dot_general(a, b, (((1,), (1,)), ((), ())), preferred_element_type=F32)


def _dot_tn(a, b):
    return lax.dot_general(a, b, (((0,), (0,)), ((), ())), preferred_element_type=F32)


def _dot(a, b):
    return jnp.dot(a, b, preferred_element_type=F32)


def _swa_specs(kvh, t):
    nb = t // SWA_BLOCK
    q_spec = pl.BlockSpec((SWA_GROUP, SWA_BLOCK, HEAD), lambda h, n: (h, n, 0))
    cur = pl.BlockSpec((1, SWA_BLOCK, HEAD), lambda h, n: (h, n, 0))
    prev = pl.BlockSpec((1, SWA_BLOCK, HEAD), lambda h, n: (h, jnp.maximum(n - 1, 0), 0))
    smem = pl.BlockSpec(memory_space=pltpu.SMEM)
    stat = pl.BlockSpec((SWA_GROUP, SWA_BLOCK, 1), lambda h, n: (h, n, 0))
    return nb, q_spec, cur, prev, smem, stat


def _swa_fwd(q, k, v, sinks, slopes):
    nh, t, _ = q.shape
    kvh = nh // SWA_GROUP
    nb, q_spec, cur, prev, smem, stat = _swa_specs(kvh, t)
    scale = HEAD ** -0.5

    def body(q_ref, kp_ref, kc_ref, vp_ref, vc_ref, sink_ref, slope_ref, o_ref, lse_ref):
        hk, n = pl.program_id(0), pl.program_id(1)
        dist, valid = _swa_masks(n)
        kk = jnp.concatenate([kp_ref[0], kc_ref[0]], axis=0)
        vv = jnp.concatenate([vp_ref[0], vc_ref[0]], axis=0)
        for g in range(SWA_GROUP):
            h = hk * SWA_GROUP + g
            s = _dot_nt(q_ref[g], kk) * scale - slope_ref[h] * dist
            s = jnp.where(valid, s, NEG)
            m = jnp.maximum(jnp.max(s, axis=-1, keepdims=True), sink_ref[h])
            p = jnp.exp(s - m)
            den = jnp.sum(p, axis=-1, keepdims=True) + jnp.exp(sink_ref[h] - m)
            o_ref[g] = _dot(p.astype(BF16), vv) / den
            lse_ref[g] = m + jnp.log(den)

    return pl.pallas_call(
        body, name="swa_fwd", grid=(kvh, nb),
        in_specs=[q_spec, prev, cur, prev, cur, smem, smem],
        out_specs=[q_spec, stat],
        out_shape=[jax.ShapeDtypeStruct((nh, t, HEAD), F32), jax.ShapeDtypeStruct((nh, t, 1), F32)],
        compiler_params=_params(("parallel", "parallel")),
    )(q, k, k, v, v, sinks, slopes)


def _swa_bwd(q, k, v, sinks, slopes, o, lse, do):
    nh, t, _ = q.shape
    kvh = nh // SWA_GROUP
    nb, q_spec, cur, prev, smem, stat = _swa_specs(kvh, t)
    scale = HEAD ** -0.5
    kv2 = pl.BlockSpec((1, 1, 2 * SWA_BLOCK, HEAD), lambda h, n: (h, n, 0, 0))
    sk = pl.BlockSpec((1, 1, 8, LANES), lambda h, n: (h, n, 0, 0))

    def body(q_ref, kp_ref, kc_ref, vp_ref, vc_ref, sink_ref, slope_ref, o_ref, lse_ref, do_ref,
             dq_ref, dk_ref, dv_ref, ds_ref):
        hk, n = pl.program_id(0), pl.program_id(1)
        dist, valid = _swa_masks(n)
        kk = jnp.concatenate([kp_ref[0], kc_ref[0]], axis=0)
        vv = jnp.concatenate([vp_ref[0], vc_ref[0]], axis=0)
        dk = jnp.zeros((2 * SWA_BLOCK, HEAD), F32)
        dv = jnp.zeros((2 * SWA_BLOCK, HEAD), F32)
        row = lax.broadcasted_iota(jnp.int32, (8, LANES), 0)
        dsink = jnp.zeros((8, LANES), F32)
        for g in range(SWA_GROUP):
            h = hk * SWA_GROUP + g
            qg = q_ref[g]
            s = _dot_nt(qg, kk) * scale - slope_ref[h] * dist
            p = jnp.where(valid, jnp.exp(s - lse_ref[g]), 0.0)
            dog = do_ref[g]
            delta = jnp.sum(dog * o_ref[g], axis=-1, keepdims=True)
            dob = dog.astype(BF16)
            dv = dv + _dot_tn(p.astype(BF16), dob)
            dp = _dot_nt(dob, vv)
            dsc = (p * (dp - delta) * scale).astype(BF16)
            dq_ref[g] = _dot(dsc, kk)
            dk = dk + _dot_tn(dsc, qg)
            dsk = -jnp.sum(jnp.exp(sink_ref[h] - lse_ref[g]) * delta, axis=0, keepdims=True)
            dsink = dsink + jnp.where(row == g, dsk, 0.0)
        dk_ref[0, 0] = dk
        dv_ref[0, 0] = dv
        ds_ref[0, 0] = dsink

    return pl.pallas_call(
        body, name="swa_bwd", grid=(kvh, nb),
        in_specs=[q_spec, prev, cur, prev, cur, smem, smem, q_spec, stat, q_spec],
        out_specs=[q_spec, kv2, kv2, sk],
        out_shape=[jax.ShapeDtypeStruct((nh, t, HEAD), F32),
                   jax.ShapeDtypeStruct((kvh, nb, 2 * SWA_BLOCK, HEAD), F32),
                   jax.ShapeDtypeStruct((kvh, nb, 2 * SWA_BLOCK, HEAD), F32),
                   jax.ShapeDtypeStruct((kvh, nb, 8, LANES), F32)],
        compiler_params=_params(("parallel", "parallel")),
    )(q, k, k, v, v, sinks, slopes, o, lse, do)


def _heads(a):
    t, w = a.shape
    return a.reshape(t, w // HEAD, HEAD).transpose(1, 0, 2)


def _unheads(a):
    h, t, _ = a.shape
    return a.transpose(1, 0, 2).reshape(t, h * HEAD)


def _fold_kv(d2):
    kvh, nb = d2.shape[:2]
    own = d2[:, :, SWA_BLOCK:]
    prev = d2[:, :, :SWA_BLOCK]
    nxt = jnp.concatenate([prev[:, 1:], jnp.zeros_like(prev[:, :1])], axis=1)
    return (own + nxt).reshape(kvh, nb * SWA_BLOCK, HEAD)


FOX_BLOCK = 512
GATE_BLOCK = 256


def _tri3(tri, x):
    hi = x.astype(BF16)
    r1 = x - hi.astype(F32)
    mid = r1.astype(BF16)
    lo = (r1 - mid.astype(F32)).astype(BF16)
    return _dot(tri, hi) + _dot(tri, mid) + _dot(tri, lo)


def _fox_gate_fwd(fz, b_f):
    t, nh = fz.shape
    blk = _tile(t, GATE_BLOCK, 16)
    nblk = t // blk

    def body(fz_ref, b_ref, c_ref):
        ri = lax.broadcasted_iota(jnp.int32, (blk, blk), 0)
        ci = lax.broadcasted_iota(jnp.int32, (blk, blk), 1)
        tri = (ci <= ri).astype(BF16)

        def step(j, carry):
            rows = pl.ds(j * blk, blk)
            z = fz_ref[rows, :] + b_ref[...]
            lf = jnp.minimum(z, 0.0) - jnp.log(1.0 + jnp.exp(-jnp.abs(z)))
            c_ref[rows, :] = carry + _tri3(tri, lf)
            return carry + jnp.sum(lf, axis=0, keepdims=True)
        lax.fori_loop(0, nblk, step, jnp.zeros((1, nh), F32))

    return pl.pallas_call(body, name="fox_gate_fwd", out_shape=jax.ShapeDtypeStruct((t, nh), F32),
                          compiler_params=_params())(fz, b_f)


def _fox_gate_bwd(fz, b_f, dc):
    t, nh = fz.shape
    blk = _tile(t, GATE_BLOCK, 16)
    nblk = t // blk

    def body(fz_ref, b_ref, dc_ref, dfz_ref, db_ref):
        ri = lax.broadcasted_iota(jnp.int32, (blk, blk), 0)
        ci = lax.broadcasted_iota(jnp.int32, (blk, blk), 1)
        tri = (ci >= ri).astype(BF16)

        def step(i, carry):
            acc, db = carry
            rows = pl.ds((nblk - 1 - i) * blk, blk)
            d = dc_ref[rows, :]
            dlf = acc + _tri3(tri, d)
            z = fz_ref[rows, :] + b_ref[...]
            dz = dlf * _sigmoid(-z)
            dfz_ref[rows, :] = dz
            return acc + jnp.sum(d, axis=0, keepdims=True), db + jnp.sum(dz, axis=0, keepdims=True)
        _, db = lax.fori_loop(0, nblk, step, (jnp.zeros((1, nh), F32), jnp.zeros((1, nh), F32)))
        db_ref[...] = db

    return pl.pallas_call(body, name="fox_gate_bwd",
                          out_shape=[jax.ShapeDtypeStruct((t, nh), F32), jax.ShapeDtypeStruct((1, nh), F32)],
                          compiler_params=_params())(fz, b_f, dc)


LOG2E = 1.4426950408889634
FOX_QSCALE = HEAD ** -0.5 * LOG2E


def _lower_triangle(blk):
    return lax.broadcasted_iota(jnp.int32, (blk, blk), 1) <= lax.broadcasted_iota(jnp.int32, (blk, blk), 0)


def _fox_fwd(q2, k, v, c_row2):
    nh, t, _ = q2.shape
    blk = c_row2.shape[-1]
    nb = t // blk

    def body(q_ref, k_ref, v_ref, ck_ref, o_ref, lse_ref):
        qi = pl.program_id(1)
        q = q_ref[0]

        def step(j, carry, diagonal):
            m, l, acc = carry
            ks = pl.ds(j * blk, blk)
            s = _dot_nt(q, k_ref[0, ks, :]) - ck_ref[0, j]
            if diagonal:
                s = jnp.where(_lower_triangle(blk), s, NEG)
            m2 = jnp.maximum(m, jnp.max(s, axis=-1, keepdims=True))
            a = jnp.exp2(m - m2)
            p = jnp.exp2(s - m2)
            l = a * l + jnp.sum(p, axis=-1, keepdims=True)
            acc = a * acc + _dot(p.astype(BF16), v_ref[0, ks, :])
            return m2, l, acc
        init = (jnp.full((blk, 1), NEG, F32), jnp.zeros((blk, 1), F32), jnp.zeros((blk, HEAD), F32))
        carry = lax.fori_loop(0, qi, lambda j, c: step(j, c, False), init)
        m, l, acc = step(qi, carry, True)
        o_ref[0] = acc / l
        lse_ref[0] = m + jnp.log(l) * LOG2E

    qb = pl.BlockSpec((1, blk, HEAD), lambda h, i: (h, i, 0))
    full = pl.BlockSpec((1, t, HEAD), lambda h, i: (h, 0, 0))
    colb = pl.BlockSpec((1, blk, 1), lambda h, i: (h, i, 0))
    rowf = pl.BlockSpec((1, nb, 1, blk), lambda h, i: (h, 0, 0, 0))
    return pl.pallas_call(
        body, name="fox_fwd", grid=(nh, nb), in_specs=[qb, full, full, rowf], out_specs=[qb, colb],
        out_shape=[jax.ShapeDtypeStruct((nh, t, HEAD), F32), jax.ShapeDtypeStruct((nh, t, 1), F32)],
        compiler_params=_params(("parallel", "parallel")),
    )(q2, k, v, c_row2)


def _fox_bwd(q2, k, v, c_row2, lse2, delta, do):
    nh, t, _ = q2.shape
    blk = c_row2.shape[-1]
    nb = t // blk
    scale = HEAD ** -0.5

    def body(q_ref, do_ref, lse_ref, dl_ref, k_ref, v_ref, ck_ref, dq_ref, dk_ref, dv_ref, dc_ref, dcq_ref):
        kb = pl.program_id(1)

        @pl.when(kb == 0)
        def _():
            dq_ref[...] = jnp.zeros_like(dq_ref)
            dcq_ref[...] = jnp.zeros_like(dcq_ref)

        k = k_ref[0]
        v = v_ref[0]
        ck = ck_ref[0, 0]

        def step(i, carry, diagonal):
            dk, dv, dck = carry
            rs = pl.ds(i * blk, blk)
            q = q_ref[0, rs, :]
            do = do_ref[0, rs, :]
            p = jnp.exp2(_dot_nt(q, k) - ck - lse_ref[0, rs, :])
            if diagonal:
                p = jnp.where(_lower_triangle(blk), p, 0.0)
            dv = dv + _dot_tn(p.astype(BF16), do)
            ds = p * (_dot_nt(do, v) - dl_ref[0, rs, :])
            dck = dck - jnp.sum(ds, axis=0, keepdims=True)
            dcq_ref[0, rs, :] += jnp.sum(ds, axis=1, keepdims=True)
            dsb = ds.astype(BF16)
            dk = dk + _dot_tn(dsb, q)
            dq_ref[0, rs, :] += _dot(dsb, k) * scale
            return dk, dv, dck
        init = (jnp.zeros((blk, HEAD), F32), jnp.zeros((blk, HEAD), F32), jnp.zeros((1, blk), F32))
        carry = step(kb, init, True)
        dk, dv, dck = lax.fori_loop(kb + 1, nb, lambda i, c: step(i, c, False), carry)
        dk_ref[0] = dk * (1.0 / LOG2E)
        dv_ref[0] = dv
        dc_ref[0, 0] = dck

    full = pl.BlockSpec((1, t, HEAD), lambda h, j: (h, 0, 0))
    colf = pl.BlockSpec((1, t, 1), lambda h, j: (h, 0, 0))
    kb_spec = pl.BlockSpec((1, blk, HEAD), lambda h, j: (h, j, 0))
    rowb = pl.BlockSpec((1, 1, 1, blk), lambda h, j: (h, j, 0, 0))
    return pl.pallas_call(
        body, name="fox_bwd", grid=(nh, nb),
        in_specs=[full, full, colf, colf, kb_spec, kb_spec, rowb],
        out_specs=[full, kb_spec, kb_spec, rowb, colf],
        out_shape=[jax.ShapeDtypeStruct((nh, t, HEAD), F32), jax.ShapeDtypeStruct((nh, t, HEAD), F32),
                   jax.ShapeDtypeStruct((nh, t, HEAD), F32), jax.ShapeDtypeStruct((nh, nb, 1, blk), F32),
                   jax.ShapeDtypeStruct((nh, t, 1), F32)],
        compiler_params=_params(("parallel", "arbitrary")),
    )(q2, do, lse2, delta, k, v, c_row2)


def _split3(x):
    hi = x.astype(BF16)
    r1 = x - hi.astype(F32)
    mid = r1.astype(BF16)
    lo = (r1 - mid.astype(F32)).astype(BF16)
    return hi, mid, lo


def _segsum_raw(a, bm, parts=3):
    outs = []
    for s in range(a.shape[-1] // SEG):
        x = a[:, s * SEG:(s + 1) * SEG]
        if parts == 3:
            hi, mid, lo = _split3(x)
            outs.append(_dot(hi, bm) + _dot(mid, bm) + _dot(lo, bm))
        else:
            hi = x.astype(BF16)
            lo = (x - hi.astype(F32)).astype(BF16)
            outs.append(_dot(hi, bm) + _dot(lo, bm))
    return outs[0] if len(outs) == 1 else jnp.concatenate(outs, axis=-1)


@jax.custom_vjp
def _segsum(a, bm):
    return _segsum_raw(a, bm)


def _segsum_f(a, bm):
    return _segsum_raw(a, bm), bm


def _segsum_b(bm, ct):
    return _segsum_raw(ct, bm), jnp.zeros_like(bm)


_segsum.defvjp(_segsum_f, _segsum_b)


@jax.custom_vjp
def _bdot(a, w):
    return _dot(a.astype(BF16), w.astype(BF16))


def _bdot_f(a, w):
    return _bdot(a, w), (a, w)


def _bdot_b(saved, ct):
    a, w = saved
    ctb = ct.astype(BF16)
    return _dot_nt(ctb, w.astype(BF16)), _dot_tn(a.astype(BF16), ctb)


_bdot.defvjp(_bdot_f, _bdot_b)


def _softplus(z):
    return jnp.maximum(z, 0.0) + jnp.log(1.0 + jnp.exp(-jnp.abs(z)))


def _rwkv_pre_math(hb, hbp, mu, w0, a0, k_k, k_a, w2p, a2p, g2, bm):
    rd = w0.shape[-1]
    m = hb + (hbp - hb) * mu
    r, k, v = m[:, :rd], m[:, rd:2 * rd], m[:, 2 * rd:3 * rd]
    xwa = m[:, 3 * rd:3 * rd + LANES]
    xg = m[:, 3 * rd + LANES:]
    wlog = -_softplus(-(w0 + _bdot(jnp.tanh(xwa), w2p))) - 0.5
    decay = jnp.exp(-jnp.exp(wlog))
    a = _sigmoid(a0 + _bdot(xwa, a2p))
    g = _bdot(_sigmoid(xg), g2)
    kk0 = k * k_k
    kk = kk0 / jnp.maximum(jnp.sqrt(_segsum(kk0 * kk0, bm)), L2_EPS)
    kp = k * (1.0 + (a - 1.0) * k_a)
    return r, decay, kp, v, kk, kk * a, g


def _rwkv_post_math(y, r, kp, v, g, ln_w, ln_b, r_k, bm):
    mean = _segsum(y, bm) * (1.0 / HEAD)
    yc = y - mean
    var = _segsum(yc * yc, bm) * (1.0 / HEAD)
    yn = yc * lax.rsqrt(var + GN_EPS) * ln_w + ln_b
    bonus = _segsum(r * kp * r_k, bm) * v
    return (yn + bonus) * g


def _rwkv_pre(hb, hbp, prm, bm):
    rd = prm[1].shape[-1]
    outs = [(rd, F32)] * 7
    return _rowwise(_rwkv_pre_math, [hb, hbp], list(prm) + [bm], outs, [], tm=256, name="rwkv_pre")


def _rwkv_pre_bwd(hb, hbp, cts, prm, bm):
    n_in = hb.shape[1]

    def fn(hb, hbp, *rest):
        ct, full = rest[:7], rest[7:]
        prm_v, bm_v = full[:-1], full[-1]
        _, vjp = jax.vjp(lambda hb, hbp, *p: _rwkv_pre_math(hb, hbp, *p, bm_v), hb, hbp, *prm_v)
        g = vjp(tuple(ct))
        return g
    acc = [p.shape for p in prm]
    res = _rowwise(fn, [hb, hbp] + list(cts), list(prm) + [bm], [(n_in, F32)] * 2, acc, tm=256, name="rwkv_pre_bwd")
    return res[:2], res[2:]


def _rwkv_post(y, r, kp, v, g, prm, bm):
    rd = y.shape[1]
    return _rowwise(_rwkv_post_math, [y, r, kp, v, g], list(prm) + [bm], [(rd, F32)], [], tm=256, name="rwkv_post")[0]


def _rwkv_post_bwd(y, r, kp, v, g, dout, prm, bm):
    rd = y.shape[1]

    def fn(y, r, kp, v, g, dout, ln_w, ln_b, r_k, bm_v):
        _, vjp = jax.vjp(lambda *a: _rwkv_post_math(*a, bm_v), y, r, kp, v, g, ln_w, ln_b, r_k)
        return vjp(dout)
    acc = [p.shape for p in prm]
    res = _rowwise(fn, [y, r, kp, v, g, dout], list(prm) + [bm], [(rd, F32)] * 5, acc, tm=256, name="rwkv_post_bwd")
    return res[:5], res[5:]


SCAN_FWD_CHUNK = 32
SCAN_BWD_CHUNK = 16
SCAN_PARTS = 2


def _seg3d(x, bm):
    c, n, rd = x.shape
    return _segsum_raw(x.reshape(c * n, rd), bm, SCAN_PARTS).reshape(c, n, rd)


def _head_dots(x, bm):
    n, _, rd = x.shape
    y = _segsum_raw(jnp.broadcast_to(x, (n, 8, rd)).reshape(n * 8, rd), bm, 3).reshape(n, 8, rd)
    return jnp.sum(y, axis=1, keepdims=True) * 0.125


def _rwkv_scan_fwd(w, kk, b, k, v, r, bm, mk, side=()):
    t, _, rd = w.shape
    c = _tile(t, SCAN_FWD_CHUNK, 8)
    assert c % 2 == 0
    ns = len(side)
    steps = t // c

    def body(*refs):
        w_ref, kk_ref, b_ref, k_ref, v_ref, r_ref, bm_ref, mk_ref = refs[:8]
        side_in = refs[8:8 + ns]
        y_ref, s_ref = refs[8 + ns:10 + ns]
        side_out = refs[10 + ns:10 + 2 * ns]
        state, vb, beta, gamma = refs[10 + 2 * ns:14 + 2 * ns]
        if ns:
            start, forward, finish = _gather_phases(side, side_in, side_out, refs[14 + 2 * ns:])
            pl.when(pl.program_id(0) == 0)(start)
            pl.when(pl.program_id(0) == steps // 2)(forward)
            pl.when(pl.program_id(0) == steps - 1)(finish)

        @pl.when(pl.program_id(0) == 0)
        def _():
            state[...] = jnp.zeros_like(state)

        bmv = bm_ref[...]
        mkv = mk_ref[...]
        vb[...] = _seg3d(v_ref[...] * mkv[None], bmv)
        kk_next = kk_ref[pl.ds(1, c - 1)]
        beta[pl.ds(0, c - 1)] = _head_dots(b_ref[pl.ds(0, c - 1)] * kk_next, bmv)
        gamma[pl.ds(0, c - 1)] = _head_dots(k_ref[pl.ds(0, c - 1)] * kk_next, bmv)

        def pair(p, s):
            ia, ib = 2 * p, 2 * p + 1
            sk_a = _segsum_raw(s * kk_ref[ia], bmv, SCAN_PARTS)
            through = _segsum_raw(s * (w_ref[ia] * kk_ref[ib]), bmv, SCAN_PARTS)
            va = vb[ia]
            s = s * w_ref[ia] - sk_a * b_ref[ia] + va * k_ref[ia]
            s_ref[ia] = s
            sk_b = through - sk_a * beta[ia] + va * gamma[ia]
            s = s * w_ref[ib] - sk_b * b_ref[ib] + vb[ib] * k_ref[ib]
            s_ref[ib] = s
            return s
        state[...] = lax.fori_loop(0, c // 2, pair, state[...])
        yb = _seg3d(s_ref[...] * r_ref[...], bmv)
        y_ref[...] = jnp.sum(yb * mkv[None], axis=1, keepdims=True)

    vec = pl.BlockSpec((c, 1, rd), lambda i: (i, 0, 0))
    res = pl.pallas_call(
        body, name="rwkv_scan_fwd", grid=(steps,),
        in_specs=[vec] * 6 + [pl.BlockSpec((SEG, SEG), lambda i: (0, 0)), pl.BlockSpec((HEAD, rd), lambda i: (0, 0))]
        + [HBM_SPEC] * ns,
        out_specs=[vec, pl.BlockSpec((c, HEAD, rd), lambda i: (i, 0, 0))] + [HBM_SPEC] * ns,
        out_shape=[jax.ShapeDtypeStruct((t, 1, rd), F32), jax.ShapeDtypeStruct((t, HEAD, rd), F32)] + _gather_shapes(side),
        scratch_shapes=[pltpu.VMEM((HEAD, rd), F32), pltpu.VMEM((c, HEAD, rd), F32),
                        pltpu.VMEM((c, 1, rd), F32), pltpu.VMEM((c, 1, rd), F32)] + (_gather_sems(ns) if ns else []),
        compiler_params=_params(("arbitrary",)),
    )(w, kk, b, k, v, r, bm, mk, *side)
    return res[0], res[1], list(res[2:])


def _rwkv_scan_bwd(w, kk, b, k, v, r, dy, states, bm, mk):
    t, _, rd = w.shape
    c = _tile(t, SCAN_BWD_CHUNK, 8)
    nc = t // c
    assert c % 2 == 0

    def body(w_ref, kk_ref, b_ref, k_ref, v_ref, r_ref, dy_ref, s_ref, sp_ref, bm_ref, mk_ref,
             dr_ref, dw_ref, dk_ref, dv_ref, dkk_ref, db_ref, gstate, sp, vb, dyb, skb, gall, gball, delta, eps):
        step_id = pl.program_id(0)

        @pl.when(step_id == 0)
        def _():
            gstate[...] = jnp.zeros_like(gstate)

        bmv = bm_ref[...]
        mkv = mk_ref[...]
        sp[0] = jnp.where(step_id == nc - 1, 0.0, sp_ref[0])
        sp[1:c] = s_ref[0:c - 1]
        vb[...] = _seg3d(v_ref[...] * mkv[None], bmv)
        dyb[...] = _seg3d(dy_ref[...] * mkv[None], bmv)
        skb[...] = _seg3d(sp[...] * kk_ref[...], bmv)
        dr_ref[...] = jnp.sum(s_ref[...] * dyb[...], axis=1, keepdims=True)
        delta[pl.ds(0, c - 1)] = _head_dots(kk_ref[pl.ds(1, c - 1)] * b_ref[pl.ds(0, c - 1)], bmv)
        eps[...] = _head_dots(r_ref[...] * b_ref[...], bmv)

        def pair(p, g):
            ib = c - 1 - 2 * p
            ia = ib - 1
            g = g + dyb[ib] * r_ref[ib]
            gall[ib] = g
            gb_b = _segsum_raw(g * b_ref[ib], bmv, SCAN_PARTS)
            through = _segsum_raw(g * (w_ref[ib] * b_ref[ia]), bmv, SCAN_PARTS)
            gball[ib] = gb_b
            dya = dyb[ia]
            g = g * w_ref[ib] - gb_b * kk_ref[ib] + dya * r_ref[ia]
            gall[ia] = g
            gb_a = through - gb_b * delta[ia] + dya * eps[ia]
            gball[ia] = gb_a
            return g * w_ref[ia] - gb_a * kk_ref[ia]
        gstate[...] = lax.fori_loop(0, c // 2, pair, gstate[...])
        ga = gall[...]
        dv_ref[...] = jnp.sum(_seg3d(ga * k_ref[...], bmv) * mkv[None], axis=1, keepdims=True)
        dk_ref[...] = jnp.sum(ga * vb[...], axis=1, keepdims=True)
        dw_ref[...] = jnp.sum(ga * sp[...], axis=1, keepdims=True)
        db_ref[...] = -jnp.sum(ga * skb[...], axis=1, keepdims=True)
        dkk_ref[...] = -jnp.sum(sp[...] * gball[...], axis=1, keepdims=True)

    vec = pl.BlockSpec((c, 1, rd), lambda i: (nc - 1 - i, 0, 0))
    st = pl.BlockSpec((c, HEAD, rd), lambda i: (nc - 1 - i, 0, 0))
    st_prev = pl.BlockSpec((1, HEAD, rd), lambda i: (jnp.maximum((nc - 1 - i) * c - 1, 0), 0, 0))
    big = pltpu.VMEM((c, HEAD, rd), F32)
    return pl.pallas_call(
        body, name="rwkv_scan_bwd", grid=(nc,),
        in_specs=[vec] * 7 + [st, st_prev, pl.BlockSpec((SEG, SEG), lambda i: (0, 0)),
                              pl.BlockSpec((HEAD, rd), lambda i: (0, 0))],
        out_specs=[vec] * 6,
        out_shape=[jax.ShapeDtypeStruct((t, 1, rd), F32)] * 6,
        scratch_shapes=[pltpu.VMEM((HEAD, rd), F32), big, big, big, big, big, big,
                        pltpu.VMEM((c, 1, rd), F32), pltpu.VMEM((c, 1, rd), F32)],
        compiler_params=_params(("arbitrary",)),
    )(w, kk, b, k, v, r, dy, states, states, bm, mk)


def _shift_down(a):
    return jnp.concatenate([jnp.zeros_like(a[:1]), a[:-1]], axis=0)


def _rwkv_consts(rd):
    i = jnp.arange(SEG) // HEAD
    bm = (i[:, None] == i[None, :]).astype(BF16)
    mk = (jnp.arange(HEAD)[:, None] == (jnp.arange(rd) % HEAD)[None, :]).astype(F32)
    return bm, mk


def _lora_pad(w2, a2):
    z = jnp.zeros_like(w2)
    return jnp.concatenate([w2, z], axis=0), jnp.concatenate([jnp.zeros_like(a2), a2], axis=0)


def _rwkv_fwd(hb, prm, side=()):
    rd = prm['w0'].shape[-1]
    t = hb.shape[0]
    bm, mk = _rwkv_consts(rd)
    hbp = _shift_down(hb)
    pre_prm = (prm['mu'], prm['w0'], prm['a0'], prm['k_k'], prm['k_a'], prm['w2p'], prm['a2p'], prm['g2'])
    r, w, kp, v, kk, b, g = _rwkv_pre(hb, hbp, pre_prm, bm)
    to3 = lambda a: a.reshape(t, 1, rd)
    y3, states, gathered = _rwkv_scan_fwd(to3(w), to3(kk), to3(b), to3(kp), to3(v), to3(r), bm, mk, side)
    y = y3.reshape(t, rd)
    post_prm = (prm['ln_w'], prm['ln_b'], prm['r_k'])
    out = _rwkv_post(y, r, kp, v, g, post_prm, bm)
    return out, (hb, hbp, r, w, kp, v, kk, b, g, y, states), gathered


def _rwkv_bwd(dout, saved, prm):
    hb, hbp, r, w, kp, v, kk, b, g, y, states = saved
    rd = prm['w0'].shape[-1]
    t = hb.shape[0]
    bm, mk = _rwkv_consts(rd)
    post_prm = (prm['ln_w'], prm['ln_b'], prm['r_k'])
    (dy, dr1, dkp1, dv1, dg), (d_ln_w, d_ln_b, d_r_k) = _rwkv_post_bwd(y, r, kp, v, g, dout, post_prm, bm)
    to3 = lambda a: a.reshape(t, 1, rd)
    dr2, dw, dk2, dv2, dkk, db = _rwkv_scan_bwd(to3(w), to3(kk), to3(b), to3(kp), to3(v), to3(r), to3(dy), states, bm, mk)
    to2 = lambda a: a.reshape(t, rd)
    cts = [dr1 + to2(dr2), to2(dw), dkp1 + to2(dk2), dv1 + to2(dv2), to2(dkk), to2(db), dg]
    pre_prm = (prm['mu'], prm['w0'], prm['a0'], prm['k_k'], prm['k_a'], prm['w2p'], prm['a2p'], prm['g2'])
    (dhb, dhbp), gp = _rwkv_pre_bwd(hb, hbp, cts, pre_prm, bm)
    dhb = dhb + jnp.concatenate([dhbp[1:], jnp.zeros_like(dhbp[:1])], axis=0)
    d_mu, d_w0, d_a0, d_k_k, d_k_a, d_w2p, d_a2p, d_g2 = gp
    grads = {'rwkv_mu': d_mu, 'rwkv_w0': d_w0, 'rwkv_a0': d_a0, 'rwkv_k_k': d_k_k, 'rwkv_k_a': d_k_a,
             'rwkv_w2': d_w2p[:DECAY_LORA], 'rwkv_a2': d_a2p[DECAY_LORA:], 'rwkv_g2': d_g2,
             'rwkv_ln_w': d_ln_w, 'rwkv_ln_b': d_ln_b, 'rwkv_r_k': d_r_k}
    return dhb, grads


def _even_fwd(x, g, w_in, w_out, sinks, slopes, rprm, side=()):
    d = x.shape[1]
    q_w, kv_w = d // 2, d // 8
    hn = _rms_fwd(x, g, "even_rms")
    proj = _matmul(hn, w_in, name="even_in")
    qa, ka, va, hb = proj[:, :q_w], proj[:, q_w:q_w + kv_w], proj[:, q_w + kv_w:q_w + 2 * kv_w], proj[:, q_w + 2 * kv_w:]
    qh, kh, vh = _heads(qa).astype(BF16), _heads(ka).astype(BF16), _heads(va).astype(BF16)
    oa, lse = _swa_fwd(qh, kh, vh, sinks, slopes)
    yb, rsaved, gathered = _rwkv_fwd(hb, rprm, side)
    cat = jnp.concatenate([_unheads(oa), yb], axis=1)
    x2 = _matmul(cat, w_out, res=x, name="even_out")
    return x2, (x, hn, qh, kh, vh, oa, lse, cat, rsaved), gathered


def _even_bwd(dx2, saved, g, w_in, w_out, sinks, slopes, rprm):
    x, hn, qh, kh, vh, oa, lse, cat, rsaved = saved
    d = x.shape[1]
    dcat = _matmul(dx2, w_out, tb=True, name="even_dcat")
    d_out = _matmul(cat, dx2, ta=True, name="even_dwout")
    dya, dyb = dcat[:, :d // 2], dcat[:, d // 2:]
    dq, dk2, dv2, dsk = _swa_bwd(qh, kh, vh, sinks, slopes, oa, lse, _heads(dya))
    d_sinks = jnp.sum(dsk[:, :, :SWA_GROUP, 0], axis=1).reshape(1, -1)
    dhb, rgrads = _rwkv_bwd(dyb, rsaved, rprm)
    dproj = jnp.concatenate([_unheads(dq), _unheads(_fold_kv(dk2)), _unheads(_fold_kv(dv2)), dhb], axis=1)
    d_in = _matmul(hn, dproj, ta=True, name="even_dwin")
    dhn = _matmul(dproj, w_in, tb=True, name="even_dhn")
    dx, dg = _rms_bwd(x, g, dhn, dx2, "even_drms")
    return dx, dg, d_in, d_out, d_sinks, rgrads


def _odd_fwd(x, g, w_in, w_out, b_f):
    d = x.shape[1]
    t = x.shape[0]
    nh = d // HEAD
    hn = _rms_fwd(x, g, "odd_rms")
    qkv = _matmul(hn, w_in[:, :3 * d], name="odd_in")
    fz = _matmul(hn, w_in[:, 3 * d:], name="odd_fz")
    c = _fox_gate_fwd(fz, b_f)
    blk = _tile(t, FOX_BLOCK, 128)
    c_row = (c.T * LOG2E).reshape(nh, t // blk, 1, blk)
    qh = _heads(qkv[:, :d] * FOX_QSCALE).astype(BF16)
    kh, vh = (_heads(qkv[:, i * d:(i + 1) * d]).astype(BF16) for i in (1, 2))
    o, lse = _fox_fwd(qh, kh, vh, c_row)
    y = _unheads(o)
    x2 = _matmul(y, w_out, res=x, name="odd_out")
    return x2, (x, hn, fz, qh, kh, vh, c_row, o, lse, y)


def _odd_bwd(dx2, saved, g, w_in, w_out, b_f):
    x, hn, fz, qh, kh, vh, c_row, o, lse, y = saved
    d = x.shape[1]
    t = x.shape[0]
    nh = d // HEAD
    dy = _matmul(dx2, w_out, tb=True, name="odd_dy")
    d_out = _matmul(y, dx2, ta=True, name="odd_dwout")
    doh = _heads(dy)
    delta = _rowwise(lambda a, b: jnp.sum(a * b, axis=-1, keepdims=True),
                     [o.reshape(nh * t, HEAD), doh.reshape(nh * t, HEAD)], [], [(1, F32)], [], tm=1024,
                     name="fox_delta")[0].reshape(nh, t, 1)
    dq, dk, dv, dcr, dcc = _fox_bwd(qh, kh, vh, c_row, lse, delta, doh.astype(BF16))
    dfz, d_bf = _fox_gate_bwd(fz, b_f, (dcr.reshape(nh, t) + dcc.reshape(nh, t)).T)
    dqkv = jnp.concatenate([_unheads(dq), _unheads(dk), _unheads(dv)], axis=1)
    d_in = jnp.concatenate([_matmul(hn, dqkv, ta=True, name="odd_dwin"), _matmul(hn, dfz, ta=True, name="odd_dwin_fz")], axis=1)
    dhn_fz = _matmul(dfz, w_in[:, 3 * d:], tb=True, name="odd_dhn_fz")
    dhn = _matmul(dqkv, w_in[:, :3 * d], tb=True, res=dhn_fz, name="odd_dhn")
    dx, dg = _rms_bwd(x, g, dhn, dx2, "odd_drms")
    return dx, dg, d_in, d_out, d_bf


def _place():
    return lax.axis_index("x"), lax.axis_index("y"), lax.axis_index("c")


def _other_chips(x, y):
    return [(1 - x, y), (x, 1 - y), (1 - x, 1 - y)]


HBM_SPEC = pl.BlockSpec(memory_space=pltpu.HBM)


def _rows(ref, which, h):
    return ref.at[pl.ds(which * h, h)]


def _gather_phases(shards, ins, outs, sems):
    ici_send, ici_recv, d2d_send, d2d_recv = sems
    x, y, c = _place()
    me = 2 * x + y
    sibling = (x, y, 1 - c)
    pairs = [(i, j, px, py) for i in range(len(shards)) for j, (px, py) in enumerate(_other_chips(x, y))]
    half = lambda i, ref, which: _rows(ref, which, shards[i].shape[0] // 2)

    def over_ici(i, j, px, py, slot):
        return pltpu.make_async_remote_copy(
            src_ref=half(i, ins[i], c), dst_ref=half(i, outs[i].at[slot], c), send_sem=ici_send.at[3 * i + j],
            recv_sem=ici_recv.at[3 * i + j], device_id=(px, py, c), device_id_type=MESH)

    def over_d2d(i, j, px, py, which):
        part = half(i, outs[i].at[2 * px + py], which)
        return pltpu.make_async_remote_copy(src_ref=part, dst_ref=part, send_sem=d2d_send.at[3 * i + j],
                                            recv_sem=d2d_recv.at[3 * i + j], device_id=sibling, device_id_type=MESH)

    def start():
        for i, j, px, py in pairs:
            over_ici(i, j, px, py, me).start()

    def forward():
        for i, j, px, py in pairs:
            over_ici(i, j, px, py, 2 * px + py).wait_recv()
            over_d2d(i, j, px, py, c).start()

    def finish():
        for i, j, px, py in pairs:
            over_d2d(i, j, px, py, 1 - c).wait_recv()
        for i, j, px, py in pairs:
            over_ici(i, j, px, py, me).wait_send()
            over_d2d(i, j, px, py, c).wait_send()

    return start, forward, finish


def _gather_sems(n):
    return [pltpu.SemaphoreType.DMA((3 * n,))] * 4


def _gather_shapes(shards):
    return [jax.ShapeDtypeStruct((N_CHIPS,) + s.shape, s.dtype) for s in shards]


def _gather_chips(shards):
    n = len(shards)

    def body(*refs):
        start, forward, finish = _gather_phases(shards, refs[:n], refs[n:2 * n], refs[2 * n:])
        start()
        forward()
        finish()

    return pl.pallas_call(
        body, name="gather_weights", in_specs=[HBM_SPEC] * n, out_specs=[HBM_SPEC] * n,
        out_shape=_gather_shapes(shards), scratch_shapes=_gather_sems(n),
    )(*shards)


def _swap_halves(stacked):
    n = len(stacked)
    halves = [s.shape[1] // 2 for s in stacked]

    def body(*refs):
        ins, outs = refs[:n], refs[n:2 * n]
        send_sems, recv_sems = refs[2 * n:]
        x, y, c = _place()
        sends = []
        for i in range(n):
            cp = pltpu.make_async_remote_copy(
                src_ref=ins[i].at[:, pl.ds((1 - c) * halves[i], halves[i])], dst_ref=outs[i], send_sem=send_sems.at[i],
                recv_sem=recv_sems.at[i], device_id=(x, y, 1 - c), device_id_type=MESH)
            cp.start()
            sends.append(cp)
        for cp in sends:
            cp.wait_recv()
        for cp in sends:
            cp.wait_send()

    return pl.pallas_call(
        body, name="swap_halves", in_specs=[HBM_SPEC] * n, out_specs=[HBM_SPEC] * n,
        out_shape=[jax.ShapeDtypeStruct((N_CHIPS, h) + s.shape[2:], s.dtype) for s, h in zip(stacked, halves)],
        scratch_shapes=[pltpu.SemaphoreType.DMA((n,)), pltpu.SemaphoreType.DMA((n,))],
    )(*stacked)


def _scatter_chips(stacked):
    n = len(stacked)

    def body(*refs):
        ins, outs = refs[:n], refs[n:2 * n]
        send_sems, recv_sems = refs[2 * n:]
        x, y, c = _place()
        chips = _other_chips(x, y)
        sends = []
        for i in range(n):
            for j, (px, py) in enumerate(chips):
                cp = pltpu.make_async_remote_copy(src_ref=ins[i].at[2 * px + py], dst_ref=outs[i].at[j], send_sem=send_sems.at[3 * i + j],
                                                  recv_sem=recv_sems.at[3 * i + j], device_id=(px, py, c), device_id_type=MESH)
                cp.start()
                sends.append(cp)
        for cp in sends:
            cp.wait_recv()
        for cp in sends:
            cp.wait_send()

    return pl.pallas_call(
        body, name="scatter_grads", in_specs=[HBM_SPEC] * n, out_specs=[HBM_SPEC] * n,
        out_shape=[jax.ShapeDtypeStruct((3,) + s.shape[1:], s.dtype) for s in stacked],
        scratch_shapes=[pltpu.SemaphoreType.DMA((3 * n,)), pltpu.SemaphoreType.DMA((3 * n,))],
    )(*stacked)


def _swap_cores(arrs):
    n = len(arrs)

    def body(*refs):
        ins, outs = refs[:n], refs[n:2 * n]
        send_sems, recv_sems = refs[2 * n:]
        x, y, c = _place()
        sends = []
        for i in range(n):
            cp = pltpu.make_async_remote_copy(src_ref=ins[i], dst_ref=outs[i], send_sem=send_sems.at[i], recv_sem=recv_sems.at[i],
                                              device_id=(x, y, 1 - c), device_id_type=MESH)
            cp.start()
            sends.append(cp)
        for cp in sends:
            cp.wait_recv()
        for cp in sends:
            cp.wait_send()

    sem = pltpu.SemaphoreType.DMA((n,))
    return pl.pallas_call(
        body, name="swap_cores", in_specs=[HBM_SPEC] * n, out_specs=[HBM_SPEC] * n,
        out_shape=[jax.ShapeDtypeStruct(s.shape, s.dtype) for s in arrs],
        scratch_shapes=[sem, sem],
    )(*arrs)


def _allreduce_small(buf):
    rows = buf.shape[0]

    def body(in_ref, out_ref, gat, send_sems, recv_sems):
        x, y, c = _place()
        me = 4 * x + 2 * y + c
        gat[me] = in_ref[...]
        sends = []
        for k in range(1, N_DEV):
            bx, by, bc = (k >> 2) & 1, (k >> 1) & 1, k & 1
            peer = (x ^ bx, y ^ by, c ^ bc)
            cp = pltpu.make_async_remote_copy(src_ref=in_ref, dst_ref=gat.at[me], send_sem=send_sems.at[k - 1],
                                              recv_sem=recv_sems.at[k - 1], device_id=peer, device_id_type=MESH)
            cp.start()
            sends.append((cp, 4 * peer[0] + 2 * peer[1] + peer[2]))
        for k, (cp, slot) in enumerate(sends):
            pltpu.make_async_remote_copy(src_ref=in_ref, dst_ref=gat.at[slot], send_sem=send_sems.at[k], recv_sem=recv_sems.at[k],
                                         device_id=(x, y, c), device_id_type=MESH).wait_recv()
        for cp, _ in sends:
            cp.wait_send()
        acc = gat[0]
        for k in range(1, N_DEV):
            acc = acc + gat[k]
        out_ref[...] = acc

    vm = pl.BlockSpec(memory_space=pltpu.VMEM)
    return pl.pallas_call(
        body, name="allreduce_small", in_specs=[vm], out_specs=vm, out_shape=jax.ShapeDtypeStruct(buf.shape, F32),
        scratch_shapes=[pltpu.VMEM((N_DEV, rows, LANES), F32), pltpu.SemaphoreType.DMA((N_DEV - 1,)), pltpu.SemaphoreType.DMA((N_DEV - 1,))],
        compiler_params=_params(),
    )(buf)


def _as2d(a):
    return a.reshape(-1, a.shape[-1])


def _cast_bf16(a, name):
    a2 = _as2d(a)
    out = _rowwise(lambda v: v, [a2], [], [(a2.shape[1], BF16)], [], tm=512, name=name)[0]
    return out.reshape(a.shape)


def _assemble(gathered, axis):
    _, l, r, c = gathered.shape
    if axis == 1:
        return gathered.transpose(1, 0, 2, 3).reshape(l, N_CHIPS * r, c)
    return gathered.transpose(1, 2, 0, 3).reshape(l, r, N_CHIPS * c)


def _split_shards(full, axis):
    l, r, c = full.shape
    if axis == 1:
        return full.reshape(l, N_CHIPS, r // N_CHIPS, c).transpose(1, 0, 2, 3)
    return full.reshape(l, r, N_CHIPS, c // N_CHIPS).transpose(2, 0, 1, 3)


def _adamw_math(w, g, m, v):
    m2 = ADAM_B1 * m + (1.0 - ADAM_B1) * g
    v2 = ADAM_B2 * v + (1.0 - ADAM_B2) * (g * g)
    m_hat = m2 / (1.0 - ADAM_B1 ** ADAM_STEP)
    v_hat = v2 / (1.0 - ADAM_B2 ** ADAM_STEP)
    delta = -ADAM_LR * (m_hat / (jnp.sqrt(v_hat) + ADAM_EPS) + ADAM_WD * w)
    return delta, m2, v2


def _adamw_big(w, m, v, mine, other, core, name):
    shape = w.shape
    wd = shape[-1]
    h = mine.shape[0]
    tm = _tile(h, 256, 16)
    nh = h // tm

    def body(core_ref, w_ref, m_ref, v_ref, a_ref, b_ref, g_ref, d_ref, mo_ref, vo_ref):
        g = jnp.where(pl.program_id(0) // nh == core_ref[0], a_ref[...], b_ref[...])
        g_ref[...] = g
        d_ref[...], mo_ref[...], vo_ref[...] = _adamw_math(w_ref[...], g, m_ref[...], v_ref[...])

    rows = pl.BlockSpec((tm, wd), lambda i: (i, 0))
    half = pl.BlockSpec((tm, wd), lambda i: (i % nh, 0))
    outs = pl.pallas_call(
        body, name=name, grid=(2 * nh,),
        in_specs=[pl.BlockSpec(memory_space=pltpu.SMEM), rows, rows, rows, half, half], out_specs=[rows] * 4,
        out_shape=[jax.ShapeDtypeStruct((2 * h, wd), F32)] * 4,
        compiler_params=_params(("parallel",)),
    )(core, _as2d(w), _as2d(m), _as2d(v), mine, other)
    return [o.reshape(shape) for o in outs]


def _add2(a, b, name):
    wd = a.shape[-1]
    out = _rowwise(lambda p, q: p + q, [_as2d(a), _as2d(b)], [], [(wd, BF16)], [], tm=256, name=name)[0]
    return out.reshape(a.shape)


def _sum4(mine, recv, name):
    wd = mine.shape[-1]
    up = lambda v: v.astype(F32)
    return _rowwise(lambda a, b, c, d: ((up(a) + up(b)) + up(c)) + up(d), [mine, recv[0], recv[1], recv[2]], [], [(wd, F32)], [],
                    tm=256, name=name)[0]


def _pack(arrs):
    parts = []
    for a in arrs:
        f = a.reshape(-1).astype(F32)
        parts.append(jnp.pad(f, (0, (-f.shape[0]) % LANES)))
    flat = jnp.concatenate(parts)
    flat = jnp.pad(flat, (0, (-flat.shape[0]) % (8 * LANES)))
    return flat.reshape(-1, LANES)


def _unpack(buf, like):
    flat = buf.reshape(-1)
    out, off = [], 0
    for a in like:
        n = math.prod(a.shape)
        out.append(flat[off:off + n].reshape(a.shape))
        off += n + (-n) % LANES
    return out


def kernel(x, p, ffn1_norm, ffn1_w_gu, ffn1_w_down, mix_norm, ffn2_norm, ffn2_w_gu, ffn2_w_down, ple_norm, ple_w_gate, ple_w_proj, even_w_in, even_w_out, swa_sinks, rwkv_mu, rwkv_w0, rwkv_w2, rwkv_a0, rwkv_a2, rwkv_g2, rwkv_k_k, rwkv_k_a, rwkv_r_k, rwkv_ln_w, rwkv_ln_b, fox_w_in, fox_b_f, fox_w_out, final_norm, loss_target, m_ffn1_norm, m_ffn1_w_gu, m_ffn1_w_down, m_mix_norm, m_ffn2_norm, m_ffn2_w_gu, m_ffn2_w_down, m_ple_norm, m_ple_w_gate, m_ple_w_proj, m_even_w_in, m_even_w_out, m_swa_sinks, m_rwkv_mu, m_rwkv_w0, m_rwkv_w2, m_rwkv_a0, m_rwkv_a2, m_rwkv_g2, m_rwkv_k_k, m_rwkv_k_a, m_rwkv_r_k, m_rwkv_ln_w, m_rwkv_ln_b, m_fox_w_in, m_fox_b_f, m_fox_w_out, m_final_norm, v_ffn1_norm, v_ffn1_w_gu, v_ffn1_w_down, v_mix_norm, v_ffn2_norm, v_ffn2_w_gu, v_ffn2_w_down, v_ple_norm, v_ple_w_gate, v_ple_w_proj, v_even_w_in, v_even_w_out, v_swa_sinks, v_rwkv_mu, v_rwkv_w0, v_rwkv_w2, v_rwkv_a0, v_rwkv_a2, v_rwkv_g2, v_rwkv_k_k, v_rwkv_k_a, v_rwkv_r_k, v_rwkv_ln_w, v_rwkv_ln_b, v_fox_w_in, v_fox_b_f, v_fox_w_out, v_final_norm):
    args = locals()
    wts = {n: args[n] for n in WEIGHTS}
    mom = {n: args['m_' + n] for n in WEIGHTS}
    var = {n: args['v_' + n] for n in WEIGHTS}
    xs = x[0]
    tgt = loss_target[0]
    t, d = xs.shape
    depth = ffn1_norm.shape[0]
    rd = d // 2
    row = lambda a: a.reshape(1, -1)

    names = BIG + LORA
    chip = 2 * lax.axis_index("x") + lax.axis_index("y")
    cast = {n: _cast_bf16(wts[n], f"cast_{n}") for n in names}
    items = [(n, i) for n in names for i in range(wts[n].shape[0])]
    early = lambda n, i: (n in ('ffn1_w_gu', 'ffn1_w_down') and i == 0) or n in ('even_w_in', 'even_w_out') or n in LORA
    first = [it for it in items if early(*it)]
    later = [it for it in items if not early(*it)]
    full = {n: [None] * wts[n].shape[0] for n in names}

    def place(group, gathered):
        for (n, i), g in zip(group, gathered):
            g = lax.dynamic_update_index_in_dim(g, cast[n][i], chip, 0)
            full[n][i] = _assemble(g[:, None], SHARDED[n])[0]

    place(first, _gather_chips([cast[n][i] for n, i in first]))

    n_swa = d // (2 * HEAD)
    slopes = 2.0 ** (-8.0 * jnp.arange(1, n_swa + 1, dtype=F32) / n_swa)
    w2p, a2p = _lora_pad(full['rwkv_w2'][0].astype(F32), full['rwkv_a2'][0].astype(F32))
    rprm = {'mu': rwkv_mu, 'w0': rwkv_w0, 'a0': rwkv_a0, 'k_k': rwkv_k_k, 'k_a': rwkv_k_a, 'w2p': w2p, 'a2p': a2p,
            'g2': full['rwkv_g2'][0].astype(F32), 'ln_w': rwkv_ln_w, 'ln_b': rwkv_ln_b, 'r_k': rwkv_r_k.reshape(1, rd)}

    saved = []
    h = xs
    for i in range(depth):
        h, s1 = _ffn_fwd(h, row(ffn1_norm[i]), full['ffn1_w_gu'][i], full['ffn1_w_down'][i], f"l{i}_ffn1")
        if i % 2 == 0:
            h, sm, gathered = _even_fwd(h, row(mix_norm[i]), full['even_w_in'][i // 2], full['even_w_out'][i // 2],
                                        swa_sinks[i // 2], slopes, rprm, [cast[n][k] for n, k in later])
            place(later, gathered)
        else:
            h, sm = _odd_fwd(h, row(mix_norm[i]), full['fox_w_in'][i // 2], full['fox_w_out'][i // 2], row(fox_b_f[i // 2]))
        h, s2 = _ffn_fwd(h, row(ffn2_norm[i]), full['ffn2_w_gu'][i], full['ffn2_w_down'][i], f"l{i}_ffn2")
        h, sp = _ple_fwd(h, row(ple_norm[i]), full['ple_w_gate'][i], p[i, 0], full['ple_w_proj'][i], f"l{i}_ple")
        saved.append((s1, sm, s2, sp))
    dx, loss_tile, d_final = _final_loss(h, row(final_norm), tgt)

    gl = {n: [None] * depth for n in ['ffn1_norm', 'ffn1_w_gu', 'ffn1_w_down', 'mix_norm', 'ffn2_norm', 'ffn2_w_gu',
                                      'ffn2_w_down', 'ple_norm', 'ple_w_gate', 'ple_w_proj']}
    g1 = {}
    for i in reversed(range(depth)):
        s1, sm, s2, sp = saved[i]
        dx, gl['ple_norm'][i], gl['ple_w_gate'][i], gl['ple_w_proj'][i] = _ple_bwd(
            dx, sp, row(ple_norm[i]), full['ple_w_gate'][i], p[i, 0], f"l{i}_ple")
        dx, gl['ffn2_norm'][i], gl['ffn2_w_gu'][i], gl['ffn2_w_down'][i] = _ffn_bwd(
            dx, s2, row(ffn2_norm[i]), full['ffn2_w_gu'][i], full['ffn2_w_down'][i], f"l{i}_ffn2")
        if i % 2 == 0:
            dx, gl['mix_norm'][i], g1['even_w_in'], g1['even_w_out'], g1['swa_sinks'], rg = _even_bwd(
                dx, sm, row(mix_norm[i]), full['even_w_in'][i // 2], full['even_w_out'][i // 2], swa_sinks[i // 2], slopes, rprm)
            g1.update(rg)
        else:
            dx, gl['mix_norm'][i], g1['fox_w_in'], g1['fox_w_out'], g1['fox_b_f'] = _odd_bwd(
                dx, sm, row(mix_norm[i]), full['fox_w_in'][i // 2], full['fox_w_out'][i // 2], row(fox_b_f[i // 2]))
        dx, gl['ffn1_norm'][i], gl['ffn1_w_gu'][i], gl['ffn1_w_down'][i] = _ffn_bwd(
            dx, s1, row(ffn1_norm[i]), full['ffn1_w_gu'][i], full['ffn1_w_down'][i], f"l{i}_ffn1")
    grad_x = dx.reshape(x.shape)

    local = {}
    for n, per_layer in gl.items():
        local[n] = jnp.stack(per_layer).reshape((depth,) + per_layer[0].shape[-2:]) if per_layer[0].shape[0] != 1 \
            else jnp.concatenate(per_layer, axis=0)
    for n, g in g1.items():
        local[n] = g.reshape((1,) + g.shape) if g.ndim == 2 and wts[n].ndim == 3 else g
    local['final_norm'] = d_final

    small_like = [wts[n] for n in SMALL]
    lora_like = [local[n] for n in LORA]
    packed = _pack([local[n] for n in SMALL] + [local[n] for n in LORA] + [loss_tile[0, :1]])
    red = _allreduce_small(packed)
    parts = _unpack(red, small_like + lora_like + [loss_tile[0, :1]])
    g_small = dict(zip(SMALL, parts[:len(SMALL)]))
    g_lora_full = dict(zip(LORA, parts[len(SMALL):len(SMALL) + len(LORA)]))
    loss = parts[-1].reshape(())
    for n in LORA:
        wdt = wts[n].shape[-1]
        g_small[n] = lax.dynamic_slice_in_dim(g_lora_full[n], chip * wdt, wdt, axis=2)
    tiny = SMALL + LORA
    wp, gp, mp, vp = (_pack([src[n] for n in tiny]) for src in (wts, g_small, mom, var))
    d_p, m_p, v_p = _rowwise(_adamw_math, [wp, gp, mp, vp], [], [(LANES, F32)] * 3, [], tm=512, name="adamw_small")
    like = [wts[n] for n in tiny]
    out_g = dict(g_small)
    out_d = dict(zip(tiny, _unpack(d_p, like)))
    out_m = dict(zip(tiny, _unpack(m_p, like)))
    out_v = dict(zip(tiny, _unpack(v_p, like)))

    core = lax.axis_index("c")
    stacked = [_split_shards(local[n], SHARDED[n]) for n in BIG]
    stacked = [s.reshape(N_CHIPS, -1, s.shape[-1]) for s in stacked]
    from_sibling = _swap_halves(stacked)
    chip_part = []
    for n, s, o in zip(BIG, stacked, from_sibling):
        h = o.shape[1]
        chip_part.append(_add2(lax.dynamic_slice_in_dim(s, core * h, h, axis=1), o, f"pair_{n}"))
    recv = _scatter_chips(chip_part)
    mine = [lax.dynamic_index_in_dim(s, chip, axis=0, keepdims=False) for s in chip_part]
    halves = [_sum4(a, r, f"sum_{n}") for n, a, r in zip(BIG, mine, recv)]
    others = _swap_cores(halves)
    core1 = core.astype(jnp.int32).reshape(1)
    for n, a, o in zip(BIG, halves, others):
        out_g[n], out_d[n], out_m[n], out_v[n] = _adamw_big(wts[n], mom[n], var[n], a, o, core1, f"adamw_{n}")

    fit = lambda dct: [dct[n].reshape(wts[n].shape) for n in WEIGHTS]
    return (loss, grad_x, *fit(out_g), *fit(out_d), *fit(out_m), *fit(out_v))
```

```python
import functools
import math

import jax
import jax.numpy as jnp
from jax import lax
from jax.experimental import pallas as pl
from jax.experimental.pallas import tpu as pltpu

F32 = jnp.float32
BF16 = jnp.bfloat16
MESH = pl.DeviceIdType.MESH

HEAD = 64
SWA_BLOCK = 128
SWA_GROUP = 4
DECAY_LORA = 64
ICLR_LORA = 64
GATE_LORA = 128
NORM_EPS = 1e-6
GN_EPS = 64e-5
L2_EPS = 1e-12
NEG = -1e30

ADAM_LR = 0.001
ADAM_B1 = 0.9
ADAM_B2 = 0.999
ADAM_EPS = 1e-08
ADAM_WD = 0.01
ADAM_STEP = 10

VMEM_LIMIT = 48 * 1024 * 1024
LANES = 128
SEG = 256
MM_TILE = 1408

WEIGHTS = ['ffn1_norm', 'ffn1_w_gu', 'ffn1_w_down', 'mix_norm', 'ffn2_norm', 'ffn2_w_gu', 'ffn2_w_down',
           'ple_norm', 'ple_w_gate', 'ple_w_proj', 'even_w_in', 'even_w_out', 'swa_sinks', 'rwkv_mu',
           'rwkv_w0', 'rwkv_w2', 'rwkv_a0', 'rwkv_a2', 'rwkv_g2', 'rwkv_k_k', 'rwkv_k_a', 'rwkv_r_k',
           'rwkv_ln_w', 'rwkv_ln_b', 'fox_w_in', 'fox_b_f', 'fox_w_out', 'final_norm']
SHARDED = {'ffn1_w_gu': 2, 'ffn1_w_down': 1, 'ffn2_w_gu': 2, 'ffn2_w_down': 1, 'ple_w_gate': 1,
           'ple_w_proj': 2, 'even_w_in': 2, 'even_w_out': 1, 'fox_w_in': 2, 'fox_w_out': 1,
           'rwkv_w2': 2, 'rwkv_a2': 2, 'rwkv_g2': 2}
LORA = ['rwkv_w2', 'rwkv_a2', 'rwkv_g2']
BIG = [n for n in WEIGHTS if n in SHARDED and n not in LORA]
SMALL = [n for n in WEIGHTS if n not in SHARDED]
N_CHIPS = 4
N_DEV = 8


def _tile(dim, target, align):
    best = None
    t = align
    while t <= min(dim, target):
        if dim % t == 0:
            best = t
        t += align
    return best if best is not None else dim


def _params(sem=None):
    return pltpu.CompilerParams(dimension_semantics=sem, vmem_limit_bytes=VMEM_LIMIT)


def _matmul(a, b, *, ta=False, tb=False, alpha=1.0, res=None, out_dtype=F32, name, norm=None, norm_bwd=None):
    if ta:
        kdim, m = a.shape
    else:
        m, kdim = a.shape
    if tb:
        n, kb = b.shape
    else:
        kb, n = b.shape
    assert kdim == kb, (a.shape, b.shape, ta, tb)
    tm = _tile(m, MM_TILE if norm_bwd is None else MM_TILE // 2, 128 if ta else 16)
    tn = _tile(n, MM_TILE, 128)
    tk = _tile(kdim, MM_TILE, 128)
    nk = kdim // tk
    if norm is not None:
        assert (tm == m) if ta else (tk == kdim), "the normalised tile must span whole feature rows"
    if norm_bwd is not None:
        assert tn == n and out_dtype == F32
    a_spec = pl.BlockSpec((tk, tm), lambda j, i, k: (k, i)) if ta else pl.BlockSpec((tm, tk), lambda j, i, k: (i, k))
    b_spec = pl.BlockSpec((tn, tk), lambda j, i, k: (j, k)) if tb else pl.BlockSpec((tk, tn), lambda j, i, k: (k, j))
    o_spec = pl.BlockSpec((tm, tn), lambda j, i, k: (i, j))
    whole = lambda arr: pl.BlockSpec(arr.shape, lambda j, i, k: (0, 0))
    dims = (((0 if ta else 1,), (1 if tb else 0,)), ((), ()))
    ins, in_specs = [a, b], [a_spec, b_spec]
    if norm is not None:
        ins.append(norm)
        in_specs.append(whole(norm))
    if res is not None:
        ins.append(res)
        in_specs.append(o_spec)
    if norm_bwd is not None:
        ins += list(norm_bwd)
        in_specs += [o_spec, whole(norm_bwd[1]), o_spec]
    n_in = len(ins)

    def body(*refs):
        a_ref, b_ref = refs[:2]
        rest = list(refs[2:n_in])
        outs = refs[n_in:]
        av = a_ref[...]
        if norm is not None:
            av = _rms_math(av, rest.pop(0)[...])
        prod = lax.dot_general(av.astype(BF16), b_ref[...].astype(BF16), dims, preferred_element_type=F32)

        def finish(acc):
            o = acc * alpha
            tail = list(rest)
            if res is not None:
                o = o + tail.pop(0)[...]
            if norm_bwd is None:
                outs[0][...] = o.astype(out_dtype)
                return
            x_ref, g_ref, dx_ref = tail
            dx, dg = _rms_bwd_math(x_ref[...], g_ref[...], o)
            outs[0][...] = dx_ref[...] + dx
            first = (pl.program_id(0) == 0) & (pl.program_id(1) == 0)

            @pl.when(first)
            def _():
                outs[1][...] = jnp.zeros_like(outs[1])
            outs[1][...] += dg

        if nk == 1:
            finish(prod)
        else:
            acc_ref = outs[-1]
            k = pl.program_id(2)

            @pl.when(k == 0)
            def _():
                acc_ref[...] = jnp.zeros_like(acc_ref)

            acc_ref[...] += prod
            pl.when(k == nk - 1)(lambda: finish(acc_ref[...]))

    out_specs, out_shape = [o_spec], [jax.ShapeDtypeStruct((m, n), out_dtype)]
    if norm_bwd is not None:
        out_specs.append(pl.BlockSpec((1, n), lambda j, i, k: (0, 0)))
        out_shape.append(jax.ShapeDtypeStruct((1, n), F32))
    sem = ("parallel", "parallel", "arbitrary") if norm_bwd is None else ("arbitrary",) * 3
    outs = pl.pallas_call(
        body, name=name, grid=(n // tn, m // tm, nk), in_specs=in_specs, out_specs=out_specs, out_shape=out_shape,
        scratch_shapes=[] if nk == 1 else [pltpu.VMEM((tm, tn), F32)],
        compiler_params=_params(sem),
    )(*ins)
    return outs[0] if norm_bwd is None else outs


def _rowwise(fn, tiled, full, tiled_out, acc_out, *, tm, name):
    rows = tiled[0].shape[0]
    tm = _tile(rows, tm, 16)
    nt, nf, no, na = len(tiled), len(full), len(tiled_out), len(acc_out)

    def body(*refs):
        ins = [r[...] for r in refs[:nt + nf]]
        outs = fn(*ins)
        if not isinstance(outs, (tuple, list)):
            outs = (outs,)
        assert len(outs) == no + na, (name, len(outs))
        for r, o in zip(refs[nt + nf:nt + nf + no], outs[:no]):
            r[...] = o.astype(r.dtype)
        if na:
            first = pl.program_id(0) == 0
            for r, o in zip(refs[nt + nf + no:], outs[no:]):
                @pl.when(first)
                def _(r=r):
                    r[...] = jnp.zeros_like(r)
                r[...] += o.astype(F32)

    def whole(shape):
        nd = len(shape)
        return pl.BlockSpec(tuple(shape), lambda i, nd=nd: (0,) * nd)

    in_specs = [pl.BlockSpec((tm, t.shape[1]), lambda i: (i, 0)) for t in tiled] + [whole(f.shape) for f in full]
    out_specs = [pl.BlockSpec((tm, w), lambda i: (i, 0)) for w, _ in tiled_out] + [whole(s) for s in acc_out]
    out_shape = [jax.ShapeDtypeStruct((rows, w), d) for w, d in tiled_out] + [jax.ShapeDtypeStruct(tuple(s), F32) for s in acc_out]
    res = pl.pallas_call(
        body, name=name, grid=(rows // tm,), in_specs=in_specs, out_specs=out_specs, out_shape=out_shape,
        compiler_params=_params(("arbitrary",) if na else ("parallel",)),
    )(*tiled, *full)
    return res


def _sigmoid(x):
    return 1.0 / (1.0 + jnp.exp(-x))


def _rms_math(x, g):
    return x * lax.rsqrt(jnp.mean(x * x, axis=-1, keepdims=True) + NORM_EPS) * g


def _rms_bwd_math(x, g, dh):
    rstd = lax.rsqrt(jnp.mean(x * x, axis=-1, keepdims=True) + NORM_EPS)
    xhat = x * rstd
    dxhat = dh * g
    dx = rstd * (dxhat - xhat * jnp.mean(dxhat * xhat, axis=-1, keepdims=True))
    dg = jnp.sum(dh * xhat, axis=0, keepdims=True)
    return dx, dg


def _swiglu_fwd(gu, name):
    f = gu.shape[1] // 2

    def fn(gu):
        g, u = gu[:, :f], gu[:, f:]
        return g * _sigmoid(g) * u
    return _rowwise(fn, [gu], [], [(f, BF16)], [], tm=256, name=name)[0]


def _swiglu_bwd(gu, dact, name):
    f = gu.shape[1] // 2

    def fn(gu, dact):
        g, u = gu[:, :f], gu[:, f:]
        s = _sigmoid(g)
        dg = dact * u * (s * (1.0 + g * (1.0 - s)))
        du = dact * (g * s)
        return jnp.concatenate([dg, du], axis=1)
    return _rowwise(fn, [gu, dact], [], [(2 * f, BF16)], [], tm=256, name=name)[0]


def _ffn_fwd(x, g, w_gu, w_down, tag):
    gu = _matmul(x, w_gu, norm=g, name=f"{tag}_gu")
    act = _swiglu_fwd(gu, f"{tag}_act")
    x2 = _matmul(act, w_down, alpha=0.5, res=x, name=f"{tag}_down")
    return x2, (x, gu, act)


def _ffn_bwd(dx2, saved, g, w_gu, w_down, tag):
    x, gu, act = saved
    dact = _matmul(dx2, w_down, tb=True, alpha=0.5, name=f"{tag}_dact")
    d_down = _matmul(act, dx2, ta=True, alpha=0.5, name=f"{tag}_dwdown")
    dgu = _swiglu_bwd(gu, dact, f"{tag}_dgu")
    d_gu = _matmul(x, dgu, ta=True, norm=g, name=f"{tag}_dwgu")
    dx, dg = _matmul(dgu, w_gu, tb=True, norm_bwd=(x, g, dx2), name=f"{tag}_dh")
    return dx, dg, d_gu, d_down


def _ple_fwd(x, g, w_gate, p, w_proj, tag):
    z = _matmul(x, w_gate, norm=g, name=f"{tag}_gate")
    pp = _matmul(p, w_proj, name=f"{tag}_proj")
    d = x.shape[1]
    x2 = _rowwise(lambda x, z, pp: x + _sigmoid(z) * pp, [x, z, pp], [], [(d, F32)], [], tm=512, name=f"{tag}_comb")[0]
    return x2, (x, z, pp)


def _ple_bwd(dx2, saved, g, w_gate, p, tag):
    x, z, pp = saved
    d = x.shape[1]

    def fn(dx2, z, pp):
        s = _sigmoid(z)
        return dx2 * pp * s * (1.0 - s), dx2 * s
    dz, dpp = _rowwise(fn, [dx2, z, pp], [], [(d, BF16), (d, BF16)], [], tm=512, name=f"{tag}_dcomb")
    d_gate = _matmul(x, dz, ta=True, norm=g, name=f"{tag}_dwgate")
    d_proj = _matmul(p, dpp, ta=True, name=f"{tag}_dwproj")
    dx, dg = _matmul(dz, w_gate, tb=True, norm_bwd=(x, g, dx2), name=f"{tag}_dh")
    return dx, dg, d_gate, d_proj


def _final_loss(x, g, tgt):
    d = x.shape[1]

    def fn(x, tgt, g):
        rstd = lax.rsqrt(jnp.mean(x * x, axis=-1, keepdims=True) + NORM_EPS)
        err = x * rstd * g - tgt
        loss = 0.5 * jnp.sum(jnp.mean(err * err, axis=-1, keepdims=True), axis=0, keepdims=True)
        dx, dg = _rms_bwd_math(x, g, err * (1.0 / d))
        return dx, jnp.zeros((8, LANES), F32) + loss, dg
    return _rowwise(fn, [x, tgt], [g], [(d, F32)], [(8, LANES), (1, d)], tm=256, name="final_loss")


def _swa_masks(n):
    qi = lax.broadcasted_iota(jnp.int32, (SWA_BLOCK, 2 * SWA_BLOCK), 0)
    ki = lax.broadcasted_iota(jnp.int32, (SWA_BLOCK, 2 * SWA_BLOCK), 1)
    dist = qi + SWA_BLOCK - ki
    valid = (dist >= 0) & (dist < SWA_BLOCK) & ((ki >= SWA_BLOCK) | (n > 0))
    return dist.astype(F32), valid


def _dot_nt(a, b):
    return lax.dot_general(a, b, (((1,), (1,)), ((), ())), preferred_element_type=F32)


def _dot_tn(a, b):
    return lax.dot_general(a, b, (((0,), (0,)), ((), ())), preferred_element_type=F32)


def _dot(a, b):
    return jnp.dot(a, b, preferred_element_type=F32)


def _swa_specs(kvh, t):
    nb = t // SWA_BLOCK
    q_spec = pl.BlockSpec((SWA_GROUP, SWA_BLOCK, HEAD), lambda h, n: (h, n, 0))
    cur = pl.BlockSpec((1, SWA_BLOCK, HEAD), lambda h, n: (h, n, 0))
    prev = pl.BlockSpec((1, SWA_BLOCK, HEAD), lambda h, n: (h, jnp.maximum(n - 1, 0), 0))
    smem = pl.BlockSpec(memory_space=pltpu.SMEM)
    stat = pl.BlockSpec((SWA_GROUP, SWA_BLOCK, 1), lambda h, n: (h, n, 0))
    return nb, q_spec, cur, prev, smem, stat


def _swa_fwd(q, k, v, sinks, slopes):
    nh, t, _ = q.shape
    kvh = nh // SWA_GROUP
    nb, q_spec, cur, prev, smem, stat = _swa_specs(kvh, t)
    scale = HEAD ** -0.5

    def body(q_ref, kp_ref, kc_ref, vp_ref, vc_ref, sink_ref, slope_ref, o_ref, lse_ref):
        hk, n = pl.program_id(0), pl.program_id(1)
        dist, valid = _swa_masks(n)
        kk = jnp.concatenate([kp_ref[0], kc_ref[0]], axis=0)
        vv = jnp.concatenate([vp_ref[0], vc_ref[0]], axis=0)
        for g in range(SWA_GROUP):
            h = hk * SWA_GROUP + g
            s = _dot_nt(q_ref[g], kk) * scale - slope_ref[h] * dist
            s = jnp.where(valid, s, NEG)
            m = jnp.maximum(jnp.max(s, axis=-1, keepdims=True), sink_ref[h])
            p = jnp.exp(s - m)
            den = jnp.sum(p, axis=-1, keepdims=True) + jnp.exp(sink_ref[h] - m)
            o_ref[g] = _dot(p.astype(BF16), vv) / den
            lse_ref[g] = m + jnp.log(den)

    return pl.pallas_call(
        body, name="swa_fwd", grid=(kvh, nb),
        in_specs=[q_spec, prev, cur, prev, cur, smem, smem],
        out_specs=[q_spec, stat],
        out_shape=[jax.ShapeDtypeStruct((nh, t, HEAD), F32), jax.ShapeDtypeStruct((nh, t, 1), F32)],
        compiler_params=_params(("parallel", "parallel")),
    )(q, k, k, v, v, sinks, slopes)


def _swa_bwd(q, k, v, sinks, slopes, o, lse, do):
    nh, t, _ = q.shape
    kvh = nh // SWA_GROUP
    nb, q_spec, cur, prev, smem, stat = _swa_specs(kvh, t)
    scale = HEAD ** -0.5
    kv2 = pl.BlockSpec((1, 1, 2 * SWA_BLOCK, HEAD), lambda h, n: (h, n, 0, 0))
    sk = pl.BlockSpec((1, 1, 8, LANES), lambda h, n: (h, n, 0, 0))

    def body(q_ref, kp_ref, kc_ref, vp_ref, vc_ref, sink_ref, slope_ref, o_ref, lse_ref, do_ref,
             dq_ref, dk_ref, dv_ref, ds_ref):
        hk, n = pl.program_id(0), pl.program_id(1)
        dist, valid = _swa_masks(n)
        kk = jnp.concatenate([kp_ref[0], kc_ref[0]], axis=0)
        vv = jnp.concatenate([vp_ref[0], vc_ref[0]], axis=0)
        dk = jnp.zeros((2 * SWA_BLOCK, HEAD), F32)
        dv = jnp.zeros((2 * SWA_BLOCK, HEAD), F32)
        row = lax.broadcasted_iota(jnp.int32, (8, LANES), 0)
        dsink = jnp.zeros((8, LANES), F32)
        for g in range(SWA_GROUP):
            h = hk * SWA_GROUP + g
            qg = q_ref[g]
            s = _dot_nt(qg, kk) * scale - slope_ref[h] * dist
            p = jnp.where(valid, jnp.exp(s - lse_ref[g]), 0.0)
            dog = do_ref[g]
            delta = jnp.sum(dog * o_ref[g], axis=-1, keepdims=True)
            dob = dog.astype(BF16)
            dv = dv + _dot_tn(p.astype(BF16), dob)
            dp = _dot_nt(dob, vv)
            dsc = (p * (dp - delta) * scale).astype(BF16)
            dq_ref[g] = _dot(dsc, kk)
            dk = dk + _dot_tn(dsc, qg)
            dsk = -jnp.sum(jnp.exp(sink_ref[h] - lse_ref[g]) * delta, axis=0, keepdims=True)
            dsink = dsink + jnp.where(row == g, dsk, 0.0)
        dk_ref[0, 0] = dk
        dv_ref[0, 0] = dv
        ds_ref[0, 0] = dsink

    return pl.pallas_call(
        body, name="swa_bwd", grid=(kvh, nb),
        in_specs=[q_spec, prev, cur, prev, cur, smem, smem, q_spec, stat, q_spec],
        out_specs=[q_spec, kv2, kv2, sk],
        out_shape=[jax.ShapeDtypeStruct((nh, t, HEAD), F32),
                   jax.ShapeDtypeStruct((kvh, nb, 2 * SWA_BLOCK, HEAD), F32),
                   jax.ShapeDtypeStruct((kvh, nb, 2 * SWA_BLOCK, HEAD), F32),
                   jax.ShapeDtypeStruct((kvh, nb, 8, LANES), F32)],
        compiler_params=_params(("parallel", "parallel")),
    )(q, k, k, v, v, sinks, slopes, o, lse, do)


def _heads(a):
    t, w = a.shape
    return a.reshape(t, w // HEAD, HEAD).transpose(1, 0, 2)


def _unheads(a):
    h, t, _ = a.shape
    return a.transpose(1, 0, 2).reshape(t, h * HEAD)


def _fold_kv(d2):
    kvh, nb = d2.shape[:2]
    own = d2[:, :, SWA_BLOCK:]
    prev = d2[:, :, :SWA_BLOCK]
    nxt = jnp.concatenate([prev[:, 1:], jnp.zeros_like(prev[:, :1])], axis=1)
    return (own + nxt).reshape(kvh, nb * SWA_BLOCK, HEAD)


FOX_BLOCK = 512
GATE_BLOCK = 256


def _tri3(tri, x):
    hi = x.astype(BF16)
    r1 = x - hi.astype(F32)
    mid = r1.astype(BF16)
    lo = (r1 - mid.astype(F32)).astype(BF16)
    return _dot(tri, hi) + _dot(tri, mid) + _dot(tri, lo)


def _fox_gate_fwd(fz, b_f):
    t, nh = fz.shape
    blk = _tile(t, GATE_BLOCK, 16)
    nblk = t // blk

    def body(fz_ref, b_ref, c_ref):
        ri = lax.broadcasted_iota(jnp.int32, (blk, blk), 0)
        ci = lax.broadcasted_iota(jnp.int32, (blk, blk), 1)
        tri = (ci <= ri).astype(BF16)

        def step(j, carry):
            rows = pl.ds(j * blk, blk)
            z = fz_ref[rows, :] + b_ref[...]
            lf = jnp.minimum(z, 0.0) - jnp.log(1.0 + jnp.exp(-jnp.abs(z)))
            c_ref[rows, :] = carry + _tri3(tri, lf)
            return carry + jnp.sum(lf, axis=0, keepdims=True)
        lax.fori_loop(0, nblk, step, jnp.zeros((1, nh), F32))

    return pl.pallas_call(body, name="fox_gate_fwd", out_shape=jax.ShapeDtypeStruct((t, nh), F32),
                          compiler_params=_params())(fz, b_f)


def _fox_gate_bwd(fz, b_f, dc):
    t, nh = fz.shape
    blk = _tile(t, GATE_BLOCK, 16)
    nblk = t // blk

    def body(fz_ref, b_ref, dc_ref, dfz_ref, db_ref):
        ri = lax.broadcasted_iota(jnp.int32, (blk, blk), 0)
        ci = lax.broadcasted_iota(jnp.int32, (blk, blk), 1)
        tri = (ci >= ri).astype(BF16)

        def step(i, carry):
            acc, db = carry
            rows = pl.ds((nblk - 1 - i) * blk, blk)
            d = dc_ref[rows, :]
            dlf = acc + _tri3(tri, d)
            z = fz_ref[rows, :] + b_ref[...]
            dz = dlf * _sigmoid(-z)
            dfz_ref[rows, :] = dz
            return acc + jnp.sum(d, axis=0, keepdims=True), db + jnp.sum(dz, axis=0, keepdims=True)
        _, db = lax.fori_loop(0, nblk, step, (jnp.zeros((1, nh), F32), jnp.zeros((1, nh), F32)))
        db_ref[...] = db

    return pl.pallas_call(body, name="fox_gate_bwd",
                          out_shape=[jax.ShapeDtypeStruct((t, nh), F32), jax.ShapeDtypeStruct((1, nh), F32)],
                          compiler_params=_params())(fz, b_f, dc)


LOG2E = 1.4426950408889634
FOX_QSCALE = HEAD ** -0.5 * LOG2E


def _lower_triangle(blk):
    return lax.broadcasted_iota(jnp.int32, (blk, blk), 1) <= lax.broadcasted_iota(jnp.int32, (blk, blk), 0)


def _fox_fwd(q2, k, v, c_row2):
    nh, t, _ = q2.shape
    blk = c_row2.shape[-1]
    nb = t // blk

    def body(q_ref, k_ref, v_ref, ck_ref, o_ref, lse_ref):
        qi = pl.program_id(1)
        q = q_ref[0]

        def step(j, carry, diagonal):
            m, l, acc = carry
            ks = pl.ds(j * blk, blk)
            s = _dot_nt(q, k_ref[0, ks, :]) - ck_ref[0, j]
            if diagonal:
                s = jnp.where(_lower_triangle(blk), s, NEG)
            m2 = jnp.maximum(m, jnp.max(s, axis=-1, keepdims=True))
            a = jnp.exp2(m - m2)
            p = jnp.exp2(s - m2)
            l = a * l + jnp.sum(p, axis=-1, keepdims=True)
            acc = a * acc + _dot(p.astype(BF16), v_ref[0, ks, :])
            return m2, l, acc
        init = (jnp.full((blk, 1), NEG, F32), jnp.zeros((blk, 1), F32), jnp.zeros((blk, HEAD), F32))
        carry = lax.fori_loop(0, qi, lambda j, c: step(j, c, False), init)
        m, l, acc = step(qi, carry, True)
        o_ref[0] = acc / l
        lse_ref[0] = m + jnp.log(l) * LOG2E

    qb = pl.BlockSpec((1, blk, HEAD), lambda h, i: (h, i, 0))
    full = pl.BlockSpec((1, t, HEAD), lambda h, i: (h, 0, 0))
    colb = pl.BlockSpec((1, blk, 1), lambda h, i: (h, i, 0))
    rowf = pl.BlockSpec((1, nb, 1, blk), lambda h, i: (h, 0, 0, 0))
    return pl.pallas_call(
        body, name="fox_fwd", grid=(nh, nb), in_specs=[qb, full, full, rowf], out_specs=[qb, colb],
        out_shape=[jax.ShapeDtypeStruct((nh, t, HEAD), F32), jax.ShapeDtypeStruct((nh, t, 1), F32)],
        compiler_params=_params(("parallel", "parallel")),
    )(q2, k, v, c_row2)


def _fox_bwd(q2, k, v, c_row2, lse2, delta, do):
    nh, t, _ = q2.shape
    blk = c_row2.shape[-1]
    nb = t // blk
    scale = HEAD ** -0.5

    def body(q_ref, do_ref, lse_ref, dl_ref, k_ref, v_ref, ck_ref, dq_ref, dk_ref, dv_ref, dc_ref, dcq_ref):
        kb = pl.program_id(1)

        @pl.when(kb == 0)
        def _():
            dq_ref[...] = jnp.zeros_like(dq_ref)
            dcq_ref[...] = jnp.zeros_like(dcq_ref)

        k = k_ref[0]
        v = v_ref[0]
        ck = ck_ref[0, 0]

        def step(i, carry, diagonal):
            dk, dv, dck = carry
            rs = pl.ds(i * blk, blk)
            q = q_ref[0, rs, :]
            do = do_ref[0, rs, :]
            p = jnp.exp2(_dot_nt(q, k) - ck - lse_ref[0, rs, :])
            if diagonal:
                p = jnp.where(_lower_triangle(blk), p, 0.0)
            dv = dv + _dot_tn(p.astype(BF16), do)
            ds = p * (_dot_nt(do, v) - dl_ref[0, rs, :])
            dck = dck - jnp.sum(ds, axis=0, keepdims=True)
            dcq_ref[0, rs, :] += jnp.sum(ds, axis=1, keepdims=True)
            dsb = ds.astype(BF16)
            dk = dk + _dot_tn(dsb, q)
            dq_ref[0, rs, :] += _dot(dsb, k) * scale
            return dk, dv, dck
        init = (jnp.zeros((blk, HEAD), F32), jnp.zeros((blk, HEAD), F32), jnp.zeros((1, blk), F32))
        carry = step(kb, init, True)
        dk, dv, dck = lax.fori_loop(kb + 1, nb, lambda i, c: step(i, c, False), carry)
        dk_ref[0] = dk * (1.0 / LOG2E)
        dv_ref[0] = dv
        dc_ref[0, 0] = dck

    full = pl.BlockSpec((1, t, HEAD), lambda h, j: (h, 0, 0))
    colf = pl.BlockSpec((1, t, 1), lambda h, j: (h, 0, 0))
    kb_spec = pl.BlockSpec((1, blk, HEAD), lambda h, j: (h, j, 0))
    rowb = pl.BlockSpec((1, 1, 1, blk), lambda h, j: (h, j, 0, 0))
    return pl.pallas_call(
        body, name="fox_bwd", grid=(nh, nb),
        in_specs=[full, full, colf, colf, kb_spec, kb_spec, rowb],
        out_specs=[full, kb_spec, kb_spec, rowb, colf],
        out_shape=[jax.ShapeDtypeStruct((nh, t, HEAD), F32), jax.ShapeDtypeStruct((nh, t, HEAD), F32),
                   jax.ShapeDtypeStruct((nh, t, HEAD), F32), jax.ShapeDtypeStruct((nh, nb, 1, blk), F32),
                   jax.ShapeDtypeStruct((nh, t, 1), F32)],
        compiler_params=_params(("parallel", "arbitrary")),
    )(q2, do, lse2, delta, k, v, c_row2)


def _split3(x):
    hi = x.astype(BF16)
    r1 = x - hi.astype(F32)
    mid = r1.astype(BF16)
    lo = (r1 - mid.astype(F32)).astype(BF16)
    return hi, mid, lo


def _segsum_raw(a, bm, parts=3):
    outs = []
    for s in range(a.shape[-1] // SEG):
        x = a[:, s * SEG:(s + 1) * SEG]
        if parts == 3:
            hi, mid, lo = _split3(x)
            outs.append(_dot(hi, bm) + _dot(mid, bm) + _dot(lo, bm))
        else:
            hi = x.astype(BF16)
            lo = (x - hi.astype(F32)).astype(BF16)
            outs.append(_dot(hi, bm) + _dot(lo, bm))
    return outs[0] if len(outs) == 1 else jnp.concatenate(outs, axis=-1)


@jax.custom_vjp
def _segsum(a, bm):
    return _segsum_raw(a, bm)


def _segsum_f(a, bm):
    return _segsum_raw(a, bm), bm


def _segsum_b(bm, ct):
    return _segsum_raw(ct, bm), jnp.zeros_like(bm)


_segsum.defvjp(_segsum_f, _segsum_b)


@jax.custom_vjp
def _bdot(a, w):
    return _dot(a.astype(BF16), w.astype(BF16))


def _bdot_f(a, w):
    return _bdot(a, w), (a, w)


def _bdot_b(saved, ct):
    a, w = saved
    ctb = ct.astype(BF16)
    return _dot_nt(ctb, w.astype(BF16)), _dot_tn(a.astype(BF16), ctb)


_bdot.defvjp(_bdot_f, _bdot_b)


def _softplus(z):
    return jnp.maximum(z, 0.0) + jnp.log(1.0 + jnp.exp(-jnp.abs(z)))


def _rwkv_pre_math(hb, hbp, mu, w0, a0, k_k, k_a, w2p, a2p, g2, bm):
    rd = w0.shape[-1]
    m = hb + (hbp - hb) * mu
    r, k, v = m[:, :rd], m[:, rd:2 * rd], m[:, 2 * rd:3 * rd]
    xwa = m[:, 3 * rd:3 * rd + LANES]
    xg = m[:, 3 * rd + LANES:]
    wlog = -_softplus(-(w0 + _bdot(jnp.tanh(xwa), w2p))) - 0.5
    decay = jnp.exp(-jnp.exp(wlog))
    a = _sigmoid(a0 + _bdot(xwa, a2p))
    g = _bdot(_sigmoid(xg), g2)
    kk0 = k * k_k
    kk = kk0 / jnp.maximum(jnp.sqrt(_segsum(kk0 * kk0, bm)), L2_EPS)
    kp = k * (1.0 + (a - 1.0) * k_a)
    return r, decay, kp, v, kk, kk * a, g


def _rwkv_post_math(y, r, kp, v, g, ln_w, ln_b, r_k, bm):
    mean = _segsum(y, bm) * (1.0 / HEAD)
    yc = y - mean
    var = _segsum(yc * yc, bm) * (1.0 / HEAD)
    yn = yc * lax.rsqrt(var + GN_EPS) * ln_w + ln_b
    bonus = _segsum(r * kp * r_k, bm) * v
    return (yn + bonus) * g


def _rwkv_pre(hb, hbp, prm, bm):
    rd = prm[1].shape[-1]
    outs = [(rd, F32)] * 7
    return _rowwise(_rwkv_pre_math, [hb, hbp], list(prm) + [bm], outs, [], tm=256, name="rwkv_pre")


def _rwkv_pre_bwd(hb, hbp, cts, prm, bm):
    n_in = hb.shape[1]

    def fn(hb, hbp, *rest):
        ct, full = rest[:7], rest[7:]
        prm_v, bm_v = full[:-1], full[-1]
        _, vjp = jax.vjp(lambda hb, hbp, *p: _rwkv_pre_math(hb, hbp, *p, bm_v), hb, hbp, *prm_v)
        g = vjp(tuple(ct))
        return g
    acc = [p.shape for p in prm]
    res = _rowwise(fn, [hb, hbp] + list(cts), list(prm) + [bm], [(n_in, F32)] * 2, acc, tm=256, name="rwkv_pre_bwd")
    return res[:2], res[2:]


def _rwkv_post(y, r, kp, v, g, prm, bm):
    rd = y.shape[1]
    return _rowwise(_rwkv_post_math, [y, r, kp, v, g], list(prm) + [bm], [(rd, F32)], [], tm=256, name="rwkv_post")[0]


def _rwkv_post_bwd(y, r, kp, v, g, dout, prm, bm):
    rd = y.shape[1]

    def fn(y, r, kp, v, g, dout, ln_w, ln_b, r_k, bm_v):
        _, vjp = jax.vjp(lambda *a: _rwkv_post_math(*a, bm_v), y, r, kp, v, g, ln_w, ln_b, r_k)
        return vjp(dout)
    acc = [p.shape for p in prm]
    res = _rowwise(fn, [y, r, kp, v, g, dout], list(prm) + [bm], [(rd, F32)] * 5, acc, tm=256, name="rwkv_post_bwd")
    return res[:5], res[5:]


SCAN_FWD_CHUNK = 32
SCAN_BWD_CHUNK = 16
SCAN_PARTS = 2


def _seg3d(x, bm):
    c, n, rd = x.shape
    return _segsum_raw(x.reshape(c * n, rd), bm, SCAN_PARTS).reshape(c, n, rd)


def _head_dots(x, bm):
    n, _, rd = x.shape
    y = _segsum_raw(jnp.broadcast_to(x, (n, 8, rd)).reshape(n * 8, rd), bm, 3).reshape(n, 8, rd)
    return jnp.sum(y, axis=1, keepdims=True) * 0.125


def _rwkv_scan_fwd(w, kk, b, k, v, r, bm, mk, side=()):
    t, _, rd = w.shape
    c = _tile(t, SCAN_FWD_CHUNK, 8)
    assert c % 2 == 0
    ns = len(side)
    steps = t // c

    def body(*refs):
        w_ref, kk_ref, b_ref, k_ref, v_ref, r_ref, bm_ref, mk_ref = refs[:8]
        side_in = refs[8:8 + ns]
        y_ref, s_ref = refs[8 + ns:10 + ns]
        side_out = refs[10 + ns:10 + 2 * ns]
        state, vb, beta, gamma = refs[10 + 2 * ns:14 + 2 * ns]
        if ns:
            start, forward, finish = _gather_phases(side, side_in, side_out, refs[14 + 2 * ns:])
            pl.when(pl.program_id(0) == 0)(start)
            pl.when(pl.program_id(0) == steps // 2)(forward)
            pl.when(pl.program_id(0) == steps - 1)(finish)

        @pl.when(pl.program_id(0) == 0)
        def _():
            state[...] = jnp.zeros_like(state)

        bmv = bm_ref[...]
        mkv = mk_ref[...]
        vb[...] = _seg3d(v_ref[...] * mkv[None], bmv)
        kk_next = kk_ref[pl.ds(1, c - 1)]
        beta[pl.ds(0, c - 1)] = _head_dots(b_ref[pl.ds(0, c - 1)] * kk_next, bmv)
        gamma[pl.ds(0, c - 1)] = _head_dots(k_ref[pl.ds(0, c - 1)] * kk_next, bmv)

        def pair(p, s):
            ia, ib = 2 * p, 2 * p + 1
            sk_a = _segsum_raw(s * kk_ref[ia], bmv, SCAN_PARTS)
            through = _segsum_raw(s * (w_ref[ia] * kk_ref[ib]), bmv, SCAN_PARTS)
            va = vb[ia]
            s = s * w_ref[ia] - sk_a * b_ref[ia] + va * k_ref[ia]
            s_ref[ia] = s
            sk_b = through - sk_a * beta[ia] + va * gamma[ia]
            s = s * w_ref[ib] - sk_b * b_ref[ib] + vb[ib] * k_ref[ib]
            s_ref[ib] = s
            return s
        state[...] = lax.fori_loop(0, c // 2, pair, state[...])
        yb = _seg3d(s_ref[...] * r_ref[...], bmv)
        y_ref[...] = jnp.sum(yb * mkv[None], axis=1, keepdims=True)

    vec = pl.BlockSpec((c, 1, rd), lambda i: (i, 0, 0))
    res = pl.pallas_call(
        body, name="rwkv_scan_fwd", grid=(steps,),
        in_specs=[vec] * 6 + [pl.BlockSpec((SEG, SEG), lambda i: (0, 0)), pl.BlockSpec((HEAD, rd), lambda i: (0, 0))]
        + [HBM_SPEC] * ns,
        out_specs=[vec, pl.BlockSpec((c, HEAD, rd), lambda i: (i, 0, 0))] + [HBM_SPEC] * ns,
        out_shape=[jax.ShapeDtypeStruct((t, 1, rd), F32), jax.ShapeDtypeStruct((t, HEAD, rd), F32)] + _gather_shapes(side),
        scratch_shapes=[pltpu.VMEM((HEAD, rd), F32), pltpu.VMEM((c, HEAD, rd), F32),
                        pltpu.VMEM((c, 1, rd), F32), pltpu.VMEM((c, 1, rd), F32)] + (_gather_sems(ns) if ns else []),
        compiler_params=_params(("arbitrary",)),
    )(w, kk, b, k, v, r, bm, mk, *side)
    return res[0], res[1], list(res[2:])


def _rwkv_scan_bwd(w, kk, b, k, v, r, dy, states, bm, mk):
    t, _, rd = w.shape
    c = _tile(t, SCAN_BWD_CHUNK, 8)
    nc = t // c
    assert c % 2 == 0

    def body(w_ref, kk_ref, b_ref, k_ref, v_ref, r_ref, dy_ref, s_ref, sp_ref, bm_ref, mk_ref,
             dr_ref, dw_ref, dk_ref, dv_ref, dkk_ref, db_ref, gstate, sp, vb, dyb, skb, gall, gball, delta, eps):
        step_id = pl.program_id(0)

        @pl.when(step_id == 0)
        def _():
            gstate[...] = jnp.zeros_like(gstate)

        bmv = bm_ref[...]
        mkv = mk_ref[...]
        sp[0] = jnp.where(step_id == nc - 1, 0.0, sp_ref[0])
        sp[1:c] = s_ref[0:c - 1]
        vb[...] = _seg3d(v_ref[...] * mkv[None], bmv)
        dyb[...] = _seg3d(dy_ref[...] * mkv[None], bmv)
        skb[...] = _seg3d(sp[...] * kk_ref[...], bmv)
        dr_ref[...] = jnp.sum(s_ref[...] * dyb[...], axis=1, keepdims=True)
        delta[pl.ds(0, c - 1)] = _head_dots(kk_ref[pl.ds(1, c - 1)] * b_ref[pl.ds(0, c - 1)], bmv)
        eps[...] = _head_dots(r_ref[...] * b_ref[...], bmv)

        def pair(p, g):
            ib = c - 1 - 2 * p
            ia = ib - 1
            g = g + dyb[ib] * r_ref[ib]
            gall[ib] = g
            gb_b = _segsum_raw(g * b_ref[ib], bmv, SCAN_PARTS)
            through = _segsum_raw(g * (w_ref[ib] * b_ref[ia]), bmv, SCAN_PARTS)
            gball[ib] = gb_b
            dya = dyb[ia]
            g = g * w_ref[ib] - gb_b * kk_ref[ib] + dya * r_ref[ia]
            gall[ia] = g
            gb_a = through - gb_b * delta[ia] + dya * eps[ia]
            gball[ia] = gb_a
            return g * w_ref[ia] - gb_a * kk_ref[ia]
        gstate[...] = lax.fori_loop(0, c // 2, pair, gstate[...])
        ga = gall[...]
        dv_ref[...] = jnp.sum(_seg3d(ga * k_ref[...], bmv) * mkv[None], axis=1, keepdims=True)
        dk_ref[...] = jnp.sum(ga * vb[...], axis=1, keepdims=True)
        dw_ref[...] = jnp.sum(ga * sp[...], axis=1, keepdims=True)
        db_ref[...] = -jnp.sum(ga * skb[...], axis=1, keepdims=True)
        dkk_ref[...] = -jnp.sum(sp[...] * gball[...], axis=1, keepdims=True)

    vec = pl.BlockSpec((c, 1, rd), lambda i: (nc - 1 - i, 0, 0))
    st = pl.BlockSpec((c, HEAD, rd), lambda i: (nc - 1 - i, 0, 0))
    st_prev = pl.BlockSpec((1, HEAD, rd), lambda i: (jnp.maximum((nc - 1 - i) * c - 1, 0), 0, 0))
    big = pltpu.VMEM((c, HEAD, rd), F32)
    return pl.pallas_call(
        body, name="rwkv_scan_bwd", grid=(nc,),
        in_specs=[vec] * 7 + [st, st_prev, pl.BlockSpec((SEG, SEG), lambda i: (0, 0)),
                              pl.BlockSpec((HEAD, rd), lambda i: (0, 0))],
        out_specs=[vec] * 6,
        out_shape=[jax.ShapeDtypeStruct((t, 1, rd), F32)] * 6,
        scratch_shapes=[pltpu.VMEM((HEAD, rd), F32), big, big, big, big, big, big,
                        pltpu.VMEM((c, 1, rd), F32), pltpu.VMEM((c, 1, rd), F32)],
        compiler_params=_params(("arbitrary",)),
    )(w, kk, b, k, v, r, dy, states, states, bm, mk)


def _shift_down(a):
    return jnp.concatenate([jnp.zeros_like(a[:1]), a[:-1]], axis=0)


def _rwkv_consts(rd):
    i = jnp.arange(SEG) // HEAD
    bm = (i[:, None] == i[None, :]).astype(BF16)
    mk = (jnp.arange(HEAD)[:, None] == (jnp.arange(rd) % HEAD)[None, :]).astype(F32)
    return bm, mk


def _lora_pad(w2, a2):
    z = jnp.zeros_like(w2)
    return jnp.concatenate([w2, z], axis=0), jnp.concatenate([jnp.zeros_like(a2), a2], axis=0)


def _rwkv_fwd(hb, prm, side=()):
    rd = prm['w0'].shape[-1]
    t = hb.shape[0]
    bm, mk = _rwkv_consts(rd)
    hbp = _shift_down(hb)
    pre_prm = (prm['mu'], prm['w0'], prm['a0'], prm['k_k'], prm['k_a'], prm['w2p'], prm['a2p'], prm['g2'])
    r, w, kp, v, kk, b, g = _rwkv_pre(hb, hbp, pre_prm, bm)
    to3 = lambda a: a.reshape(t, 1, rd)
    y3, states, gathered = _rwkv_scan_fwd(to3(w), to3(kk), to3(b), to3(kp), to3(v), to3(r), bm, mk, side)
    y = y3.reshape(t, rd)
    post_prm = (prm['ln_w'], prm['ln_b'], prm['r_k'])
    out = _rwkv_post(y, r, kp, v, g, post_prm, bm)
    return out, (hb, hbp, r, w, kp, v, kk, b, g, y, states), gathered


def _rwkv_bwd(dout, saved, prm):
    hb, hbp, r, w, kp, v, kk, b, g, y, states = saved
    rd = prm['w0'].shape[-1]
    t = hb.shape[0]
    bm, mk = _rwkv_consts(rd)
    post_prm = (prm['ln_w'], prm['ln_b'], prm['r_k'])
    (dy, dr1, dkp1, dv1, dg), (d_ln_w, d_ln_b, d_r_k) = _rwkv_post_bwd(y, r, kp, v, g, dout, post_prm, bm)
    to3 = lambda a: a.reshape(t, 1, rd)
    dr2, dw, dk2, dv2, dkk, db = _rwkv_scan_bwd(to3(w), to3(kk), to3(b), to3(kp), to3(v), to3(r), to3(dy), states, bm, mk)
    to2 = lambda a: a.reshape(t, rd)
    cts = [dr1 + to2(dr2), to2(dw), dkp1 + to2(dk2), dv1 + to2(dv2), to2(dkk), to2(db), dg]
    pre_prm = (prm['mu'], prm['w0'], prm['a0'], prm['k_k'], prm['k_a'], prm['w2p'], prm['a2p'], prm['g2'])
    (dhb, dhbp), gp = _rwkv_pre_bwd(hb, hbp, cts, pre_prm, bm)
    dhb = dhb + jnp.concatenate([dhbp[1:], jnp.zeros_like(dhbp[:1])], axis=0)
    d_mu, d_w0, d_a0, d_k_k, d_k_a, d_w2p, d_a2p, d_g2 = gp
    grads = {'rwkv_mu': d_mu, 'rwkv_w0': d_w0, 'rwkv_a0': d_a0, 'rwkv_k_k': d_k_k, 'rwkv_k_a': d_k_a,
             'rwkv_w2': d_w2p[:DECAY_LORA], 'rwkv_a2': d_a2p[DECAY_LORA:], 'rwkv_g2': d_g2,
             'rwkv_ln_w': d_ln_w, 'rwkv_ln_b': d_ln_b, 'rwkv_r_k': d_r_k}
    return dhb, grads


def _even_fwd(x, g, w_in, w_out, sinks, slopes, rprm, side=()):
    d = x.shape[1]
    q_w, kv_w = d // 2, d // 8
    proj = _matmul(x, w_in, norm=g, name="even_in")
    qa, ka, va, hb = proj[:, :q_w], proj[:, q_w:q_w + kv_w], proj[:, q_w + kv_w:q_w + 2 * kv_w], proj[:, q_w + 2 * kv_w:]
    qh, kh, vh = _heads(qa).astype(BF16), _heads(ka).astype(BF16), _heads(va).astype(BF16)
    oa, lse = _swa_fwd(qh, kh, vh, sinks, slopes)
    yb, rsaved, gathered = _rwkv_fwd(hb, rprm, side)
    cat = jnp.concatenate([_unheads(oa), yb], axis=1)
    x2 = _matmul(cat, w_out, res=x, name="even_out")
    return x2, (x, qh, kh, vh, oa, lse, cat, rsaved), gathered


def _even_bwd(dx2, saved, g, w_in, w_out, sinks, slopes, rprm):
    x, qh, kh, vh, oa, lse, cat, rsaved = saved
    d = x.shape[1]
    dcat = _matmul(dx2, w_out, tb=True, name="even_dcat")
    d_out = _matmul(cat, dx2, ta=True, name="even_dwout")
    dya, dyb = dcat[:, :d // 2], dcat[:, d // 2:]
    dq, dk2, dv2, dsk = _swa_bwd(qh, kh, vh, sinks, slopes, oa, lse, _heads(dya))
    d_sinks = jnp.sum(dsk[:, :, :SWA_GROUP, 0], axis=1).reshape(1, -1)
    dhb, rgrads = _rwkv_bwd(dyb, rsaved, rprm)
    dproj = jnp.concatenate([_unheads(dq), _unheads(_fold_kv(dk2)), _unheads(_fold_kv(dv2)), dhb], axis=1)
    d_in = _matmul(x, dproj, ta=True, norm=g, name="even_dwin")
    dx, dg = _matmul(dproj, w_in, tb=True, norm_bwd=(x, g, dx2), name="even_dhn")
    return dx, dg, d_in, d_out, d_sinks, rgrads


def _odd_fwd(x, g, w_in, w_out, b_f):
    d = x.shape[1]
    t = x.shape[0]
    nh = d // HEAD
    qkv = _matmul(x, w_in[:, :3 * d], norm=g, name="odd_in")
    fz = _matmul(x, w_in[:, 3 * d:], norm=g, name="odd_fz")
    c = _fox_gate_fwd(fz, b_f)
    blk = _tile(t, FOX_BLOCK, 128)
    c_row = (c.T * LOG2E).reshape(nh, t // blk, 1, blk)
    qh = _heads(qkv[:, :d] * FOX_QSCALE).astype(BF16)
    kh, vh = (_heads(qkv[:, i * d:(i + 1) * d]).astype(BF16) for i in (1, 2))
    o, lse = _fox_fwd(qh, kh, vh, c_row)
    y = _unheads(o)
    x2 = _matmul(y, w_out, res=x, name="odd_out")
    return x2, (x, fz, qh, kh, vh, c_row, o, lse, y)


def _odd_bwd(dx2, saved, g, w_in, w_out, b_f):
    x, fz, qh, kh, vh, c_row, o, lse, y = saved
    d = x.shape[1]
    t = x.shape[0]
    nh = d // HEAD
    dy = _matmul(dx2, w_out, tb=True, name="odd_dy")
    d_out = _matmul(y, dx2, ta=True, name="odd_dwout")
    doh = _heads(dy)
    delta = _rowwise(lambda a, b: jnp.sum(a * b, axis=-1, keepdims=True),
                     [o.reshape(nh * t, HEAD), doh.reshape(nh * t, HEAD)], [], [(1, F32)], [], tm=1024,
                     name="fox_delta")[0].reshape(nh, t, 1)
    dq, dk, dv, dcr, dcc = _fox_bwd(qh, kh, vh, c_row, lse, delta, doh.astype(BF16))
    dfz, d_bf = _fox_gate_bwd(fz, b_f, (dcr.reshape(nh, t) + dcc.reshape(nh, t)).T)
    dqkv = jnp.concatenate([_unheads(dq), _unheads(dk), _unheads(dv)], axis=1)
    d_in = jnp.concatenate([_matmul(x, dqkv, ta=True, norm=g, name="odd_dwin"),
                            _matmul(x, dfz, ta=True, norm=g, name="odd_dwin_fz")], axis=1)
    dhn_fz = _matmul(dfz, w_in[:, 3 * d:], tb=True, name="odd_dhn_fz")
    dx, dg = _matmul(dqkv, w_in[:, :3 * d], tb=True, res=dhn_fz, norm_bwd=(x, g, dx2), name="odd_dhn")
    return dx, dg, d_in, d_out, d_bf


def _place():
    return lax.axis_index("x"), lax.axis_index("y"), lax.axis_index("c")


def _other_chips(x, y):
    return [(1 - x, y), (x, 1 - y), (1 - x, 1 - y)]


HBM_SPEC = pl.BlockSpec(memory_space=pltpu.HBM)


def _rows(ref, which, h):
    return ref.at[pl.ds(which * h, h)]


def _gather_phases(shards, ins, outs, sems):
    ici_send, ici_recv, d2d_send, d2d_recv = sems
    x, y, c = _place()
    me = 2 * x + y
    sibling = (x, y, 1 - c)
    pairs = [(i, j, px, py) for i in range(len(shards)) for j, (px, py) in enumerate(_other_chips(x, y))]
    half = lambda i, ref, which: _rows(ref, which, shards[i].shape[0] // 2)

    def over_ici(i, j, px, py, slot):
        return pltpu.make_async_remote_copy(
            src_ref=half(i, ins[i], c), dst_ref=half(i, outs[i].at[slot], c), send_sem=ici_send.at[3 * i + j],
            recv_sem=ici_recv.at[3 * i + j], device_id=(px, py, c), device_id_type=MESH)

    def over_d2d(i, j, px, py, which):
        part = half(i, outs[i].at[2 * px + py], which)
        return pltpu.make_async_remote_copy(src_ref=part, dst_ref=part, send_sem=d2d_send.at[3 * i + j],
                                            recv_sem=d2d_recv.at[3 * i + j], device_id=sibling, device_id_type=MESH)

    def start():
        for i, j, px, py in pairs:
            over_ici(i, j, px, py, me).start()

    def forward():
        for i, j, px, py in pairs:
            over_ici(i, j, px, py, 2 * px + py).wait_recv()
            over_d2d(i, j, px, py, c).start()

    def finish():
        for i, j, px, py in pairs:
            over_d2d(i, j, px, py, 1 - c).wait_recv()
        for i, j, px, py in pairs:
            over_ici(i, j, px, py, me).wait_send()
            over_d2d(i, j, px, py, c).wait_send()

    return start, forward, finish


def _gather_sems(n):
    return [pltpu.SemaphoreType.DMA((3 * n,))] * 4


def _gather_shapes(shards):
    return [jax.ShapeDtypeStruct((N_CHIPS,) + s.shape, s.dtype) for s in shards]


def _gather_chips(shards):
    n = len(shards)

    def body(*refs):
        start, forward, finish = _gather_phases(shards, refs[:n], refs[n:2 * n], refs[2 * n:])
        start()
        forward()
        finish()

    return pl.pallas_call(
        body, name="gather_weights", in_specs=[HBM_SPEC] * n, out_specs=[HBM_SPEC] * n,
        out_shape=_gather_shapes(shards), scratch_shapes=_gather_sems(n),
    )(*shards)


def _swap_halves(stacked):
    n = len(stacked)
    halves = [s.shape[1] // 2 for s in stacked]

    def body(*refs):
        ins, outs = refs[:n], refs[n:2 * n]
        send_sems, recv_sems = refs[2 * n:]
        x, y, c = _place()
        sends = []
        for i in range(n):
            cp = pltpu.make_async_remote_copy(
                src_ref=ins[i].at[:, pl.ds((1 - c) * halves[i], halves[i])], dst_ref=outs[i], send_sem=send_sems.at[i],
                recv_sem=recv_sems.at[i], device_id=(x, y, 1 - c), device_id_type=MESH)
            cp.start()
            sends.append(cp)
        for cp in sends:
            cp.wait_recv()
        for cp in sends:
            cp.wait_send()

    return pl.pallas_call(
        body, name="swap_halves", in_specs=[HBM_SPEC] * n, out_specs=[HBM_SPEC] * n,
        out_shape=[jax.ShapeDtypeStruct((N_CHIPS, h) + s.shape[2:], s.dtype) for s, h in zip(stacked, halves)],
        scratch_shapes=[pltpu.SemaphoreType.DMA((n,)), pltpu.SemaphoreType.DMA((n,))],
    )(*stacked)


def _scatter_chips(stacked):
    n = len(stacked)

    def body(*refs):
        ins, outs = refs[:n], refs[n:2 * n]
        send_sems, recv_sems = refs[2 * n:]
        x, y, c = _place()
        chips = _other_chips(x, y)
        sends = []
        for i in range(n):
            for j, (px, py) in enumerate(chips):
                cp = pltpu.make_async_remote_copy(src_ref=ins[i].at[2 * px + py], dst_ref=outs[i].at[j], send_sem=send_sems.at[3 * i + j],
                                                  recv_sem=recv_sems.at[3 * i + j], device_id=(px, py, c), device_id_type=MESH)
                cp.start()
                sends.append(cp)
        for cp in sends:
            cp.wait_recv()
        for cp in sends:
            cp.wait_send()

    return pl.pallas_call(
        body, name="scatter_grads", in_specs=[HBM_SPEC] * n, out_specs=[HBM_SPEC] * n,
        out_shape=[jax.ShapeDtypeStruct((3,) + s.shape[1:], s.dtype) for s in stacked],
        scratch_shapes=[pltpu.SemaphoreType.DMA((3 * n,)), pltpu.SemaphoreType.DMA((3 * n,))],
    )(*stacked)


def _swap_cores(arrs):
    n = len(arrs)

    def body(*refs):
        ins, outs = refs[:n], refs[n:2 * n]
        send_sems, recv_sems = refs[2 * n:]
        x, y, c = _place()
        sends = []
        for i in range(n):
            cp = pltpu.make_async_remote_copy(src_ref=ins[i], dst_ref=outs[i], send_sem=send_sems.at[i], recv_sem=recv_sems.at[i],
                                              device_id=(x, y, 1 - c), device_id_type=MESH)
            cp.start()
            sends.append(cp)
        for cp in sends:
            cp.wait_recv()
        for cp in sends:
            cp.wait_send()

    sem = pltpu.SemaphoreType.DMA((n,))
    return pl.pallas_call(
        body, name="swap_cores", in_specs=[HBM_SPEC] * n, out_specs=[HBM_SPEC] * n,
        out_shape=[jax.ShapeDtypeStruct(s.shape, s.dtype) for s in arrs],
        scratch_shapes=[sem, sem],
    )(*arrs)


def _allreduce_small(buf):
    rows = buf.shape[0]

    def body(in_ref, out_ref, gat, send_sems, recv_sems):
        x, y, c = _place()
        me = 4 * x + 2 * y + c
        gat[me] = in_ref[...]
        sends = []
        for k in range(1, N_DEV):
            bx, by, bc = (k >> 2) & 1, (k >> 1) & 1, k & 1
            peer = (x ^ bx, y ^ by, c ^ bc)
            cp = pltpu.make_async_remote_copy(src_ref=in_ref, dst_ref=gat.at[me], send_sem=send_sems.at[k - 1],
                                              recv_sem=recv_sems.at[k - 1], device_id=peer, device_id_type=MESH)
            cp.start()
            sends.append((cp, 4 * peer[0] + 2 * peer[1] + peer[2]))
        for k, (cp, slot) in enumerate(sends):
            pltpu.make_async_remote_copy(src_ref=in_ref, dst_ref=gat.at[slot], send_sem=send_sems.at[k], recv_sem=recv_sems.at[k],
                                         device_id=(x, y, c), device_id_type=MESH).wait_recv()
        for cp, _ in sends:
            cp.wait_send()
        acc = gat[0]
        for k in range(1, N_DEV):
            acc = acc + gat[k]
        out_ref[...] = acc

    vm = pl.BlockSpec(memory_space=pltpu.VMEM)
    return pl.pallas_call(
        body, name="allreduce_small", in_specs=[vm], out_specs=vm, out_shape=jax.ShapeDtypeStruct(buf.shape, F32),
        scratch_shapes=[pltpu.VMEM((N_DEV, rows, LANES), F32), pltpu.SemaphoreType.DMA((N_DEV - 1,)), pltpu.SemaphoreType.DMA((N_DEV - 1,))],
        compiler_params=_params(),
    )(buf)


def _as2d(a):
    return a.reshape(-1, a.shape[-1])


def _cast_bf16(a, name):
    a2 = _as2d(a)
    out = _rowwise(lambda v: v, [a2], [], [(a2.shape[1], BF16)], [], tm=512, name=name)[0]
    return out.reshape(a.shape)


def _assemble(gathered, axis):
    _, l, r, c = gathered.shape
    if axis == 1:
        return gathered.transpose(1, 0, 2, 3).reshape(l, N_CHIPS * r, c)
    return gathered.transpose(1, 2, 0, 3).reshape(l, r, N_CHIPS * c)


def _split_shards(full, axis):
    l, r, c = full.shape
    if axis == 1:
        return full.reshape(l, N_CHIPS, r // N_CHIPS, c).transpose(1, 0, 2, 3)
    return full.reshape(l, r, N_CHIPS, c // N_CHIPS).transpose(2, 0, 1, 3)


def _adamw_math(w, g, m, v):
    m2 = ADAM_B1 * m + (1.0 - ADAM_B1) * g
    v2 = ADAM_B2 * v + (1.0 - ADAM_B2) * (g * g)
    m_hat = m2 / (1.0 - ADAM_B1 ** ADAM_STEP)
    v_hat = v2 / (1.0 - ADAM_B2 ** ADAM_STEP)
    delta = -ADAM_LR * (m_hat / (jnp.sqrt(v_hat) + ADAM_EPS) + ADAM_WD * w)
    return delta, m2, v2


def _adamw_big(w, m, v, mine, other, core, name):
    shape = w.shape
    wd = shape[-1]
    h = mine.shape[0]
    tm = _tile(h, 256, 16)
    nh = h // tm

    def body(core_ref, w_ref, m_ref, v_ref, a_ref, b_ref, g_ref, d_ref, mo_ref, vo_ref):
        g = jnp.where(pl.program_id(0) // nh == core_ref[0], a_ref[...], b_ref[...])
        g_ref[...] = g
        d_ref[...], mo_ref[...], vo_ref[...] = _adamw_math(w_ref[...], g, m_ref[...], v_ref[...])

    rows = pl.BlockSpec((tm, wd), lambda i: (i, 0))
    half = pl.BlockSpec((tm, wd), lambda i: (i % nh, 0))
    outs = pl.pallas_call(
        body, name=name, grid=(2 * nh,),
        in_specs=[pl.BlockSpec(memory_space=pltpu.SMEM), rows, rows, rows, half, half], out_specs=[rows] * 4,
        out_shape=[jax.ShapeDtypeStruct((2 * h, wd), F32)] * 4,
        compiler_params=_params(("parallel",)),
    )(core, _as2d(w), _as2d(m), _as2d(v), mine, other)
    return [o.reshape(shape) for o in outs]


def _add2(a, b, name):
    wd = a.shape[-1]
    out = _rowwise(lambda p, q: p + q, [_as2d(a), _as2d(b)], [], [(wd, BF16)], [], tm=256, name=name)[0]
    return out.reshape(a.shape)


def _sum4(mine, recv, name):
    wd = mine.shape[-1]
    up = lambda v: v.astype(F32)
    return _rowwise(lambda a, b, c, d: ((up(a) + up(b)) + up(c)) + up(d), [mine, recv[0], recv[1], recv[2]], [], [(wd, F32)], [],
                    tm=256, name=name)[0]


def _pack(arrs):
    parts = []
    for a in arrs:
        f = a.reshape(-1).astype(F32)
        parts.append(jnp.pad(f, (0, (-f.shape[0]) % LANES)))
    flat = jnp.concatenate(parts)
    flat = jnp.pad(flat, (0, (-flat.shape[0]) % (8 * LANES)))
    return flat.reshape(-1, LANES)


def _unpack(buf, like):
    flat = buf.reshape(-1)
    out, off = [], 0
    for a in like:
        n = math.prod(a.shape)
        out.append(flat[off:off + n].reshape(a.shape))
        off += n + (-n) % LANES
    return out


def kernel(x, p, ffn1_norm, ffn1_w_gu, ffn1_w_down, mix_norm, ffn2_norm, ffn2_w_gu, ffn2_w_down, ple_norm, ple_w_gate, ple_w_proj, even_w_in, even_w_out, swa_sinks, rwkv_mu, rwkv_w0, rwkv_w2, rwkv_a0, rwkv_a2, rwkv_g2, rwkv_k_k, rwkv_k_a, rwkv_r_k, rwkv_ln_w, rwkv_ln_b, fox_w_in, fox_b_f, fox_w_out, final_norm, loss_target, m_ffn1_norm, m_ffn1_w_gu, m_ffn1_w_down, m_mix_norm, m_ffn2_norm, m_ffn2_w_gu, m_ffn2_w_down, m_ple_norm, m_ple_w_gate, m_ple_w_proj, m_even_w_in, m_even_w_out, m_swa_sinks, m_rwkv_mu, m_rwkv_w0, m_rwkv_w2, m_rwkv_a0, m_rwkv_a2, m_rwkv_g2, m_rwkv_k_k, m_rwkv_k_a, m_rwkv_r_k, m_rwkv_ln_w, m_rwkv_ln_b, m_fox_w_in, m_fox_b_f, m_fox_w_out, m_final_norm, v_ffn1_norm, v_ffn1_w_gu, v_ffn1_w_down, v_mix_norm, v_ffn2_norm, v_ffn2_w_gu, v_ffn2_w_down, v_ple_norm, v_ple_w_gate, v_ple_w_proj, v_even_w_in, v_even_w_out, v_swa_sinks, v_rwkv_mu, v_rwkv_w0, v_rwkv_w2, v_rwkv_a0, v_rwkv_a2, v_rwkv_g2, v_rwkv_k_k, v_rwkv_k_a, v_rwkv_r_k, v_rwkv_ln_w, v_rwkv_ln_b, v_fox_w_in, v_fox_b_f, v_fox_w_out, v_final_norm):
    args = locals()
    wts = {n: args[n] for n in WEIGHTS}
    mom = {n: args['m_' + n] for n in WEIGHTS}
    var = {n: args['v_' + n] for n in WEIGHTS}
    xs = x[0]
    tgt = loss_target[0]
    t, d = xs.shape
    depth = ffn1_norm.shape[0]
    rd = d // 2
    row = lambda a: a.reshape(1, -1)

    names = BIG + LORA
    chip = 2 * lax.axis_index("x") + lax.axis_index("y")
    cast = {n: _cast_bf16(wts[n], f"cast_{n}") for n in names}
    items = [(n, i) for n in names for i in range(wts[n].shape[0])]
    early = lambda n, i: (n in ('ffn1_w_gu', 'ffn1_w_down') and i == 0) or n in ('even_w_in', 'even_w_out') or n in LORA
    first = [it for it in items if early(*it)]
    later = [it for it in items if not early(*it)]
    full = {n: [None] * wts[n].shape[0] for n in names}

    def place(group, gathered):
        for (n, i), g in zip(group, gathered):
            g = lax.dynamic_update_index_in_dim(g, cast[n][i], chip, 0)
            full[n][i] = _assemble(g[:, None], SHARDED[n])[0]

    place(first, _gather_chips([cast[n][i] for n, i in first]))

    n_swa = d // (2 * HEAD)
    slopes = 2.0 ** (-8.0 * jnp.arange(1, n_swa + 1, dtype=F32) / n_swa)
    w2p, a2p = _lora_pad(full['rwkv_w2'][0].astype(F32), full['rwkv_a2'][0].astype(F32))
    rprm = {'mu': rwkv_mu, 'w0': rwkv_w0, 'a0': rwkv_a0, 'k_k': rwkv_k_k, 'k_a': rwkv_k_a, 'w2p': w2p, 'a2p': a2p,
            'g2': full['rwkv_g2'][0].astype(F32), 'ln_w': rwkv_ln_w, 'ln_b': rwkv_ln_b, 'r_k': rwkv_r_k.reshape(1, rd)}

    saved = []
    h = xs
    for i in range(depth):
        h, s1 = _ffn_fwd(h, row(ffn1_norm[i]), full['ffn1_w_gu'][i], full['ffn1_w_down'][i], f"l{i}_ffn1")
        if i % 2 == 0:
            h, sm, gathered = _even_fwd(h, row(mix_norm[i]), full['even_w_in'][i // 2], full['even_w_out'][i // 2],
                                        swa_sinks[i // 2], slopes, rprm, [cast[n][k] for n, k in later])
            place(later, gathered)
        else:
            h, sm = _odd_fwd(h, row(mix_norm[i]), full['fox_w_in'][i // 2], full['fox_w_out'][i // 2], row(fox_b_f[i // 2]))
        h, s2 = _ffn_fwd(h, row(ffn2_norm[i]), full['ffn2_w_gu'][i], full['ffn2_w_down'][i], f"l{i}_ffn2")
        h, sp = _ple_fwd(h, row(ple_norm[i]), full['ple_w_gate'][i], p[i, 0], full['ple_w_proj'][i], f"l{i}_ple")
        saved.append((s1, sm, s2, sp))
    dx, loss_tile, d_final = _final_loss(h, row(final_norm), tgt)

    gl = {n: [None] * depth for n in ['ffn1_norm', 'ffn1_w_gu', 'ffn1_w_down', 'mix_norm', 'ffn2_norm', 'ffn2_w_gu',
                                      'ffn2_w_down', 'ple_norm', 'ple_w_gate', 'ple_w_proj']}
    g1 = {}
    for i in reversed(range(depth)):
        s1, sm, s2, sp = saved[i]
        dx, gl['ple_norm'][i], gl['ple_w_gate'][i], gl['ple_w_proj'][i] = _ple_bwd(
            dx, sp, row(ple_norm[i]), full['ple_w_gate'][i], p[i, 0], f"l{i}_ple")
        dx, gl['ffn2_norm'][i], gl['ffn2_w_gu'][i], gl['ffn2_w_down'][i] = _ffn_bwd(
            dx, s2, row(ffn2_norm[i]), full['ffn2_w_gu'][i], full['ffn2_w_down'][i], f"l{i}_ffn2")
        if i % 2 == 0:
            dx, gl['mix_norm'][i], g1['even_w_in'], g1['even_w_out'], g1['swa_sinks'], rg = _even_bwd(
                dx, sm, row(mix_norm[i]), full['even_w_in'][i // 2], full['even_w_out'][i // 2], swa_sinks[i // 2], slopes, rprm)
            g1.update(rg)
        else:
            dx, gl['mix_norm'][i], g1['fox_w_in'], g1['fox_w_out'], g1['fox_b_f'] = _odd_bwd(
                dx, sm, row(mix_norm[i]), full['fox_w_in'][i // 2], full['fox_w_out'][i // 2], row(fox_b_f[i // 2]))
        dx, gl['ffn1_norm'][i], gl['ffn1_w_gu'][i], gl['ffn1_w_down'][i] = _ffn_bwd(
            dx, s1, row(ffn1_norm[i]), full['ffn1_w_gu'][i], full['ffn1_w_down'][i], f"l{i}_ffn1")
    grad_x = dx.reshape(x.shape)

    local = {}
    for n, per_layer in gl.items():
        local[n] = jnp.stack(per_layer).reshape((depth,) + per_layer[0].shape[-2:]) if per_layer[0].shape[0] != 1 \
            else jnp.concatenate(per_layer, axis=0)
    for n, g in g1.items():
        local[n] = g.reshape((1,) + g.shape) if g.ndim == 2 and wts[n].ndim == 3 else g
    local['final_norm'] = d_final

    small_like = [wts[n] for n in SMALL]
    packed = _pack([local[n] for n in SMALL] + [loss_tile[0, :1]])
    parts = _unpack(_allreduce_small(packed), small_like + [loss_tile[0, :1]])
    g_small = dict(zip(SMALL, parts[:len(SMALL)]))
    loss = parts[-1].reshape(())
    wp, gp, mp, vp = (_pack([src[n] for n in SMALL]) for src in (wts, g_small, mom, var))
    d_p, m_p, v_p = _rowwise(_adamw_math, [wp, gp, mp, vp], [], [(LANES, F32)] * 3, [], tm=512, name="adamw_small")
    out_g = dict(g_small)
    out_d = dict(zip(SMALL, _unpack(d_p, small_like)))
    out_m = dict(zip(SMALL, _unpack(m_p, small_like)))
    out_v = dict(zip(SMALL, _unpack(v_p, small_like)))

    core = lax.axis_index("c")
    sharded = BIG + LORA
    stacked = [_split_shards(local[n], SHARDED[n]) for n in sharded]
    stacked = [s.reshape(N_CHIPS, -1, s.shape[-1]) for s in stacked]
    from_sibling = _swap_halves(stacked)
    chip_part = []
    for n, s, o in zip(sharded, stacked, from_sibling):
        h = o.shape[1]
        chip_part.append(_add2(lax.dynamic_slice_in_dim(s, core * h, h, axis=1), o, f"pair_{n}"))
    recv = _scatter_chips(chip_part)
    mine = [lax.dynamic_index_in_dim(s, chip, axis=0, keepdims=False) for s in chip_part]
    halves = [_sum4(a, r, f"sum_{n}") for n, a, r in zip(sharded, mine, recv)]
    others = _swap_cores(halves)
    core1 = core.astype(jnp.int32).reshape(1)
    for n, a, o in zip(sharded, halves, others):
        out_g[n], out_d[n], out_m[n], out_v[n] = _adamw_big(wts[n], mom[n], var[n], a, o, core1, f"adamw_{n}")

    fit = lambda dct: [dct[n].reshape(wts[n].shape) for n in WEIGHTS]
    return (loss, grad_x, *fit(out_g), *fit(out_d), *fit(out_m), *fit(out_v))
```

```python
import functools
import math

import jax
import jax.numpy as jnp
from jax import lax
from jax.experimental import pallas as pl
from jax.experimental.pallas import tpu as pltpu

F32 = jnp.float32
BF16 = jnp.bfloat16
MESH = pl.DeviceIdType.MESH

HEAD = 64
SWA_BLOCK = 128
SWA_GROUP = 4
DECAY_LORA = 64
ICLR_LORA = 64
GATE_LORA = 128
NORM_EPS = 1e-6
GN_EPS = 64e-5
L2_EPS = 1e-12
NEG = -1e30

ADAM_LR = 0.001
ADAM_B1 = 0.9
ADAM_B2 = 0.999
ADAM_EPS = 1e-08
ADAM_WD = 0.01
ADAM_STEP = 10

VMEM_LIMIT = 48 * 1024 * 1024
LANES = 128
SEG = 256
MM_TILE = 1408

WEIGHTS = ['ffn1_norm', 'ffn1_w_gu', 'ffn1_w_down', 'mix_norm', 'ffn2_norm', 'ffn2_w_gu', 'ffn2_w_down',
           'ple_norm', 'ple_w_gate', 'ple_w_proj', 'even_w_in', 'even_w_out', 'swa_sinks', 'rwkv_mu',
           'rwkv_w0', 'rwkv_w2', 'rwkv_a0', 'rwkv_a2', 'rwkv_g2', 'rwkv_k_k', 'rwkv_k_a', 'rwkv_r_k',
           'rwkv_ln_w', 'rwkv_ln_b', 'fox_w_in', 'fox_b_f', 'fox_w_out', 'final_norm']
SHARDED = {'ffn1_w_gu': 2, 'ffn1_w_down': 1, 'ffn2_w_gu': 2, 'ffn2_w_down': 1, 'ple_w_gate': 1,
           'ple_w_proj': 2, 'even_w_in': 2, 'even_w_out': 1, 'fox_w_in': 2, 'fox_w_out': 1,
           'rwkv_w2': 2, 'rwkv_a2': 2, 'rwkv_g2': 2}
LORA = ['rwkv_w2', 'rwkv_a2', 'rwkv_g2']
BIG = [n for n in WEIGHTS if n in SHARDED and n not in LORA]
SMALL = [n for n in WEIGHTS if n not in SHARDED]
N_CHIPS = 4
N_DEV = 8


def _tile(dim, target, align):
    best = None
    t = align
    while t <= min(dim, target):
        if dim % t == 0:
            best = t
        t += align
    return best if best is not None else dim


def _params(sem=None):
    return pltpu.CompilerParams(dimension_semantics=sem, vmem_limit_bytes=VMEM_LIMIT)


def _matmul(a, b, *, ta=False, tb=False, alpha=1.0, res=None, out_dtype=F32, name, norm=None, norm_bwd=None):
    if ta:
        kdim, m = a.shape
    else:
        m, kdim = a.shape
    if tb:
        n, kb = b.shape
    else:
        kb, n = b.shape
    assert kdim == kb, (a.shape, b.shape, ta, tb)
    tm = _tile(m, MM_TILE if norm_bwd is None else MM_TILE // 2, 128 if ta else 16)
    tn = _tile(n, MM_TILE, 128)
    tk = _tile(kdim, MM_TILE, 128)
    nk = kdim // tk
    if norm is not None:
        assert (tm == m) if ta else (tk == kdim), "the normalised tile must span whole feature rows"
    if norm_bwd is not None:
        assert tn == n and out_dtype == F32
    a_spec = pl.BlockSpec((tk, tm), lambda j, i, k: (k, i)) if ta else pl.BlockSpec((tm, tk), lambda j, i, k: (i, k))
    b_spec = pl.BlockSpec((tn, tk), lambda j, i, k: (j, k)) if tb else pl.BlockSpec((tk, tn), lambda j, i, k: (k, j))
    o_spec = pl.BlockSpec((tm, tn), lambda j, i, k: (i, j))
    whole = lambda arr: pl.BlockSpec(arr.shape, lambda j, i, k: (0, 0))
    dims = (((0 if ta else 1,), (1 if tb else 0,)), ((), ()))
    ins, in_specs = [a, b], [a_spec, b_spec]
    if norm is not None:
        ins.append(norm)
        in_specs.append(whole(norm))
    if res is not None:
        ins.append(res)
        in_specs.append(o_spec)
    if norm_bwd is not None:
        ins += list(norm_bwd)
        in_specs += [o_spec, whole(norm_bwd[1]), o_spec]
    n_in = len(ins)

    def body(*refs):
        a_ref, b_ref = refs[:2]
        rest = list(refs[2:n_in])
        outs = refs[n_in:]
        av = a_ref[...]
        if norm is not None:
            av = _rms_math(av, rest.pop(0)[...])
        prod = lax.dot_general(av.astype(BF16), b_ref[...].astype(BF16), dims, preferred_element_type=F32)

        def finish(acc):
            o = acc * alpha
            tail = list(rest)
            if res is not None:
                o = o + tail.pop(0)[...]
            if norm_bwd is None:
                outs[0][...] = o.astype(out_dtype)
                return
            x_ref, g_ref, dx_ref = tail
            dx, dg = _rms_bwd_math(x_ref[...], g_ref[...], o)
            outs[0][...] = dx_ref[...] + dx
            first = (pl.program_id(0) == 0) & (pl.program_id(1) == 0)

            @pl.when(first)
            def _():
                outs[1][...] = jnp.zeros_like(outs[1])
            outs[1][...] += dg

        if nk == 1:
            finish(prod)
        else:
            acc_ref = outs[-1]
            k = pl.program_id(2)

            @pl.when(k == 0)
            def _():
                acc_ref[...] = jnp.zeros_like(acc_ref)

            acc_ref[...] += prod
            pl.when(k == nk - 1)(lambda: finish(acc_ref[...]))

    out_specs, out_shape = [o_spec], [jax.ShapeDtypeStruct((m, n), out_dtype)]
    if norm_bwd is not None:
        out_specs.append(pl.BlockSpec((1, n), lambda j, i, k: (0, 0)))
        out_shape.append(jax.ShapeDtypeStruct((1, n), F32))
    sem = ("parallel", "parallel", "arbitrary") if norm_bwd is None else ("arbitrary",) * 3
    outs = pl.pallas_call(
        body, name=name, grid=(n // tn, m // tm, nk), in_specs=in_specs, out_specs=out_specs, out_shape=out_shape,
        scratch_shapes=[] if nk == 1 else [pltpu.VMEM((tm, tn), F32)],
        compiler_params=_params(sem),
    )(*ins)
    return outs[0] if norm_bwd is None else outs


def _rowwise(fn, tiled, full, tiled_out, acc_out, *, tm, name):
    rows = tiled[0].shape[0]
    tm = _tile(rows, tm, 16)
    nt, nf, no, na = len(tiled), len(full), len(tiled_out), len(acc_out)

    def body(*refs):
        ins = [r[...] for r in refs[:nt + nf]]
        outs = fn(*ins)
        if not isinstance(outs, (tuple, list)):
            outs = (outs,)
        assert len(outs) == no + na, (name, len(outs))
        for r, o in zip(refs[nt + nf:nt + nf + no], outs[:no]):
            r[...] = o.astype(r.dtype)
        if na:
            first = pl.program_id(0) == 0
            for r, o in zip(refs[nt + nf + no:], outs[no:]):
                @pl.when(first)
                def _(r=r):
                    r[...] = jnp.zeros_like(r)
                r[...] += o.astype(F32)

    def whole(shape):
        nd = len(shape)
        return pl.BlockSpec(tuple(shape), lambda i, nd=nd: (0,) * nd)

    in_specs = [pl.BlockSpec((tm, t.shape[1]), lambda i: (i, 0)) for t in tiled] + [whole(f.shape) for f in full]
    out_specs = [pl.BlockSpec((tm, w), lambda i: (i, 0)) for w, _ in tiled_out] + [whole(s) for s in acc_out]
    out_shape = [jax.ShapeDtypeStruct((rows, w), d) for w, d in tiled_out] + [jax.ShapeDtypeStruct(tuple(s), F32) for s in acc_out]
    res = pl.pallas_call(
        body, name=name, grid=(rows // tm,), in_specs=in_specs, out_specs=out_specs, out_shape=out_shape,
        compiler_params=_params(("arbitrary",) if na else ("parallel",)),
    )(*tiled, *full)
    return res


def _sigmoid(x):
    return 1.0 / (1.0 + jnp.exp(-x))


def _rms_math(x, g):
    return x * lax.rsqrt(jnp.mean(x * x, axis=-1, keepdims=True) + NORM_EPS) * g


def _rms_bwd_math(x, g, dh):
    rstd = lax.rsqrt(jnp.mean(x * x, axis=-1, keepdims=True) + NORM_EPS)
    xhat = x * rstd
    dxhat = dh * g
    dx = rstd * (dxhat - xhat * jnp.mean(dxhat * xhat, axis=-1, keepdims=True))
    dg = jnp.sum(dh * xhat, axis=0, keepdims=True)
    return dx, dg


def _swiglu_fwd(gu, name):
    f = gu.shape[1] // 2

    def fn(gu):
        g, u = gu[:, :f], gu[:, f:]
        return g * _sigmoid(g) * u
    return _rowwise(fn, [gu], [], [(f, BF16)], [], tm=256, name=name)[0]


def _swiglu_bwd(gu, dact, name):
    f = gu.shape[1] // 2

    def fn(gu, dact):
        g, u = gu[:, :f], gu[:, f:]
        s = _sigmoid(g)
        dg = dact * u * (s * (1.0 + g * (1.0 - s)))
        du = dact * (g * s)
        return jnp.concatenate([dg, du], axis=1)
    return _rowwise(fn, [gu, dact], [], [(2 * f, BF16)], [], tm=256, name=name)[0]


def _ffn_fwd(x, g, w_gu, w_down, tag):
    gu = _matmul(x, w_gu, norm=g, name=f"{tag}_gu")
    act = _swiglu_fwd(gu, f"{tag}_act")
    x2 = _matmul(act, w_down, alpha=0.5, res=x, name=f"{tag}_down")
    return x2, (x, gu, act)


def _ffn_bwd(dx2, saved, g, w_gu, w_down, tag):
    x, gu, act = saved
    dact = _matmul(dx2, w_down, tb=True, alpha=0.5, name=f"{tag}_dact")
    d_down = _matmul(act, dx2, ta=True, alpha=0.5, name=f"{tag}_dwdown")
    dgu = _swiglu_bwd(gu, dact, f"{tag}_dgu")
    d_gu = _matmul(x, dgu, ta=True, norm=g, name=f"{tag}_dwgu")
    dx, dg = _matmul(dgu, w_gu, tb=True, norm_bwd=(x, g, dx2), name=f"{tag}_dh")
    return dx, dg, d_gu, d_down


def _ple_fwd(x, g, w_gate, p, w_proj, tag):
    z = _matmul(x, w_gate, norm=g, name=f"{tag}_gate")
    pp = _matmul(p, w_proj, name=f"{tag}_proj")
    d = x.shape[1]
    x2 = _rowwise(lambda x, z, pp: x + _sigmoid(z) * pp, [x, z, pp], [], [(d, F32)], [], tm=512, name=f"{tag}_comb")[0]
    return x2, (x, z, pp)


def _ple_bwd(dx2, saved, g, w_gate, p, tag):
    x, z, pp = saved
    d = x.shape[1]

    def fn(dx2, z, pp):
        s = _sigmoid(z)
        return dx2 * pp * s * (1.0 - s), dx2 * s
    dz, dpp = _rowwise(fn, [dx2, z, pp], [], [(d, BF16), (d, BF16)], [], tm=512, name=f"{tag}_dcomb")
    d_gate = _matmul(x, dz, ta=True, norm=g, name=f"{tag}_dwgate")
    d_proj = _matmul(p, dpp, ta=True, name=f"{tag}_dwproj")
    dx, dg = _matmul(dz, w_gate, tb=True, norm_bwd=(x, g, dx2), name=f"{tag}_dh")
    return dx, dg, d_gate, d_proj


def _final_loss(x, g, tgt):
    d = x.shape[1]

    def fn(x, tgt, g):
        rstd = lax.rsqrt(jnp.mean(x * x, axis=-1, keepdims=True) + NORM_EPS)
        err = x * rstd * g - tgt
        loss = 0.5 * jnp.sum(jnp.mean(err * err, axis=-1, keepdims=True), axis=0, keepdims=True)
        dx, dg = _rms_bwd_math(x, g, err * (1.0 / d))
        return dx, jnp.zeros((8, LANES), F32) + loss, dg
    return _rowwise(fn, [x, tgt], [g], [(d, F32)], [(8, LANES), (1, d)], tm=256, name="final_loss")


def _swa_masks(n):
    qi = lax.broadcasted_iota(jnp.int32, (SWA_BLOCK, 2 * SWA_BLOCK), 0)
    ki = lax.broadcasted_iota(jnp.int32, (SWA_BLOCK, 2 * SWA_BLOCK), 1)
    dist = qi + SWA_BLOCK - ki
    valid = (dist >= 0) & (dist < SWA_BLOCK) & ((ki >= SWA_BLOCK) | (n > 0))
    return dist.astype(F32), valid


def _dot_nt(a, b):
    return lax.dot_general(a, b, (((1,), (1,)), ((), ())), preferred_element_type=F32)


def _dot_tn(a, b):
    return lax.dot_general(a, b, (((0,), (0,)), ((), ())), preferred_element_type=F32)


def _dot(a, b):
    return jnp.dot(a, b, preferred_element_type=F32)


def _swa_specs(kvh, t):
    nb = t // SWA_BLOCK
    q_spec = pl.BlockSpec((SWA_GROUP, SWA_BLOCK, HEAD), lambda h, n: (h, n, 0))
    cur = pl.BlockSpec((1, SWA_BLOCK, HEAD), lambda h, n: (h, n, 0))
    prev = pl.BlockSpec((1, SWA_BLOCK, HEAD), lambda h, n: (h, jnp.maximum(n - 1, 0), 0))
    smem = pl.BlockSpec(memory_space=pltpu.SMEM)
    stat = pl.BlockSpec((SWA_GROUP, SWA_BLOCK, 1), lambda h, n: (h, n, 0))
    return nb, q_spec, cur, prev, smem, stat


def _swa_fwd(q, k, v, sinks, slopes):
    nh, t, _ = q.shape
    kvh = nh // SWA_GROUP
    nb, q_spec, cur, prev, smem, stat = _swa_specs(kvh, t)
    scale = HEAD ** -0.5

    def body(q_ref, kp_ref, kc_ref, vp_ref, vc_ref, sink_ref, slope_ref, o_ref, lse_ref):
        hk, n = pl.program_id(0), pl.program_id(1)
        dist, valid = _swa_masks(n)
        kk = jnp.concatenate([kp_ref[0], kc_ref[0]], axis=0)
        vv = jnp.concatenate([vp_ref[0], vc_ref[0]], axis=0)
        for g in range(SWA_GROUP):
            h = hk * SWA_GROUP + g
            s = _dot_nt(q_ref[g], kk) * scale - slope_ref[h] * dist
            s = jnp.where(valid, s, NEG)
            m = jnp.maximum(jnp.max(s, axis=-1, keepdims=True), sink_ref[h])
            p = jnp.exp(s - m)
            den = jnp.sum(p, axis=-1, keepdims=True) + jnp.exp(sink_ref[h] - m)
            o_ref[g] = _dot(p.astype(BF16), vv) / den
            lse_ref[g] = m + jnp.log(den)

    return pl.pallas_call(
        body, name="swa_fwd", grid=(kvh, nb),
        in_specs=[q_spec, prev, cur, prev, cur, smem, smem],
        out_specs=[q_spec, stat],
        out_shape=[jax.ShapeDtypeStruct((nh, t, HEAD), F32), jax.ShapeDtypeStruct((nh, t, 1), F32)],
        compiler_params=_params(("parallel", "parallel")),
    )(q, k, k, v, v, sinks, slopes)


def _swa_bwd(q, k, v, sinks, slopes, o, lse, do):
    nh, t, _ = q.shape
    kvh = nh // SWA_GROUP
    nb, q_spec, cur, prev, smem, stat = _swa_specs(kvh, t)
    scale = HEAD ** -0.5
    kv2 = pl.BlockSpec((1, 1, 2 * SWA_BLOCK, HEAD), lambda h, n: (h, n, 0, 0))
    sk = pl.BlockSpec((1, 1, 8, LANES), lambda h, n: (h, n, 0, 0))

    def body(q_ref, kp_ref, kc_ref, vp_ref, vc_ref, sink_ref, slope_ref, o_ref, lse_ref, do_ref,
             dq_ref, dk_ref, dv_ref, ds_ref):
        hk, n = pl.program_id(0), pl.program_id(1)
        dist, valid = _swa_masks(n)
        kk = jnp.concatenate([kp_ref[0], kc_ref[0]], axis=0)
        vv = jnp.concatenate([vp_ref[0], vc_ref[0]], axis=0)
        dk = jnp.zeros((2 * SWA_BLOCK, HEAD), F32)
        dv = jnp.zeros((2 * SWA_BLOCK, HEAD), F32)
        row = lax.broadcasted_iota(jnp.int32, (8, LANES), 0)
        dsink = jnp.zeros((8, LANES), F32)
        for g in range(SWA_GROUP):
            h = hk * SWA_GROUP + g
            qg = q_ref[g]
            s = _dot_nt(qg, kk) * scale - slope_ref[h] * dist
            p = jnp.where(valid, jnp.exp(s - lse_ref[g]), 0.0)
            dog = do_ref[g]
            delta = jnp.sum(dog * o_ref[g], axis=-1, keepdims=True)
            dob = dog.astype(BF16)
            dv = dv + _dot_tn(p.astype(BF16), dob)
            dp = _dot_nt(dob, vv)
            dsc = (p * (dp - delta) * scale).astype(BF16)
            dq_ref[g] = _dot(dsc, kk)
            dk = dk + _dot_tn(dsc, qg)
            dsk = -jnp.sum(jnp.exp(sink_ref[h] - lse_ref[g]) * delta, axis=0, keepdims=True)
            dsink = dsink + jnp.where(row == g, dsk, 0.0)
        dk_ref[0, 0] = dk
        dv_ref[0, 0] = dv
        ds_ref[0, 0] = dsink

    return pl.pallas_call(
        body, name="swa_bwd", grid=(kvh, nb),
        in_specs=[q_spec, prev, cur, prev, cur, smem, smem, q_spec, stat, q_spec],
        out_specs=[q_spec, kv2, kv2, sk],
        out_shape=[jax.ShapeDtypeStruct((nh, t, HEAD), F32),
                   jax.ShapeDtypeStruct((kvh, nb, 2 * SWA_BLOCK, HEAD), F32),
                   jax.ShapeDtypeStruct((kvh, nb, 2 * SWA_BLOCK, HEAD), F32),
                   jax.ShapeDtypeStruct((kvh, nb, 8, LANES), F32)],
        compiler_params=_params(("parallel", "parallel")),
    )(q, k, k, v, v, sinks, slopes, o, lse, do)


def _heads(a):
    t, w = a.shape
    return a.reshape(t, w // HEAD, HEAD).transpose(1, 0, 2)


def _unheads(a):
    h, t, _ = a.shape
    return a.transpose(1, 0, 2).reshape(t, h * HEAD)


def _fold_kv(d2):
    kvh, nb = d2.shape[:2]
    own = d2[:, :, SWA_BLOCK:]
    prev = d2[:, :, :SWA_BLOCK]
    nxt = jnp.concatenate([prev[:, 1:], jnp.zeros_like(prev[:, :1])], axis=1)
    return (own + nxt).reshape(kvh, nb * SWA_BLOCK, HEAD)


FOX_BLOCK = 512
GATE_BLOCK = 256


def _tri3(tri, x):
    hi = x.astype(BF16)
    r1 = x - hi.astype(F32)
    mid = r1.astype(BF16)
    lo = (r1 - mid.astype(F32)).astype(BF16)
    return _dot(tri, hi) + _dot(tri, mid) + _dot(tri, lo)


def _fox_gate_fwd(fz, b_f):
    t, nh = fz.shape
    blk = _tile(t, GATE_BLOCK, 16)
    nblk = t // blk

    def body(fz_ref, b_ref, c_ref):
        ri = lax.broadcasted_iota(jnp.int32, (blk, blk), 0)
        ci = lax.broadcasted_iota(jnp.int32, (blk, blk), 1)
        tri = (ci <= ri).astype(BF16)

        def step(j, carry):
            rows = pl.ds(j * blk, blk)
            z = fz_ref[rows, :] + b_ref[...]
            lf = jnp.minimum(z, 0.0) - jnp.log(1.0 + jnp.exp(-jnp.abs(z)))
            c_ref[rows, :] = carry + _tri3(tri, lf)
            return carry + jnp.sum(lf, axis=0, keepdims=True)
        lax.fori_loop(0, nblk, step, jnp.zeros((1, nh), F32))

    return pl.pallas_call(body, name="fox_gate_fwd", out_shape=jax.ShapeDtypeStruct((t, nh), F32),
                          compiler_params=_params())(fz, b_f)


def _fox_gate_bwd(fz, b_f, dc):
    t, nh = fz.shape
    blk = _tile(t, GATE_BLOCK, 16)
    nblk = t // blk

    def body(fz_ref, b_ref, dc_ref, dfz_ref, db_ref):
        ri = lax.broadcasted_iota(jnp.int32, (blk, blk), 0)
        ci = lax.broadcasted_iota(jnp.int32, (blk, blk), 1)
        tri = (ci >= ri).astype(BF16)

        def step(i, carry):
            acc, db = carry
            rows = pl.ds((nblk - 1 - i) * blk, blk)
            d = dc_ref[rows, :]
            dlf = acc + _tri3(tri, d)
            z = fz_ref[rows, :] + b_ref[...]
            dz = dlf * _sigmoid(-z)
            dfz_ref[rows, :] = dz
            return acc + jnp.sum(d, axis=0, keepdims=True), db + jnp.sum(dz, axis=0, keepdims=True)
        _, db = lax.fori_loop(0, nblk, step, (jnp.zeros((1, nh), F32), jnp.zeros((1, nh), F32)))
        db_ref[...] = db

    return pl.pallas_call(body, name="fox_gate_bwd",
                          out_shape=[jax.ShapeDtypeStruct((t, nh), F32), jax.ShapeDtypeStruct((1, nh), F32)],
                          compiler_params=_params())(fz, b_f, dc)


LOG2E = 1.4426950408889634
FOX_QSCALE = HEAD ** -0.5 * LOG2E


def _lower_triangle(blk):
    return lax.broadcasted_iota(jnp.int32, (blk, blk), 1) <= lax.broadcasted_iota(jnp.int32, (blk, blk), 0)


def _fox_fwd(q2, k, v, c_row2):
    nh, t, _ = q2.shape
    blk = c_row2.shape[-1]
    nb = t // blk

    def body(q_ref, k_ref, v_ref, ck_ref, o_ref, lse_ref):
        qi = pl.program_id(1)
        q = q_ref[0]

        def step(j, carry, diagonal):
            m, l, acc = carry
            ks = pl.ds(j * blk, blk)
            s = _dot_nt(q, k_ref[0, ks, :]) - ck_ref[0, j]
            if diagonal:
                s = jnp.where(_lower_triangle(blk), s, NEG)
            m2 = jnp.maximum(m, jnp.max(s, axis=-1, keepdims=True))
            a = jnp.exp2(m - m2)
            p = jnp.exp2(s - m2)
            l = a * l + jnp.sum(p, axis=-1, keepdims=True)
            acc = a * acc + _dot(p.astype(BF16), v_ref[0, ks, :])
            return m2, l, acc
        init = (jnp.full((blk, 1), NEG, F32), jnp.zeros((blk, 1), F32), jnp.zeros((blk, HEAD), F32))
        carry = lax.fori_loop(0, qi, lambda j, c: step(j, c, False), init)
        m, l, acc = step(qi, carry, True)
        o_ref[0] = acc / l
        lse_ref[0] = m + jnp.log(l) * LOG2E

    qb = pl.BlockSpec((1, blk, HEAD), lambda h, i: (h, i, 0))
    full = pl.BlockSpec((1, t, HEAD), lambda h, i: (h, 0, 0))
    colb = pl.BlockSpec((1, blk, 1), lambda h, i: (h, i, 0))
    rowf = pl.BlockSpec((1, nb, 1, blk), lambda h, i: (h, 0, 0, 0))
    return pl.pallas_call(
        body, name="fox_fwd", grid=(nh, nb), in_specs=[qb, full, full, rowf], out_specs=[qb, colb],
        out_shape=[jax.ShapeDtypeStruct((nh, t, HEAD), F32), jax.ShapeDtypeStruct((nh, t, 1), F32)],
        compiler_params=_params(("parallel", "parallel")),
    )(q2, k, v, c_row2)


def _fox_bwd(q2, k, v, c_row2, lse2, delta, do):
    nh, t, _ = q2.shape
    blk = c_row2.shape[-1]
    nb = t // blk
    scale = HEAD ** -0.5

    def body(q_ref, do_ref, lse_ref, dl_ref, k_ref, v_ref, ck_ref, dq_ref, dk_ref, dv_ref, dc_ref, dcq_ref):
        kb = pl.program_id(1)

        @pl.when(kb == 0)
        def _():
            dq_ref[...] = jnp.zeros_like(dq_ref)
            dcq_ref[...] = jnp.zeros_like(dcq_ref)

        k = k_ref[0]
        v = v_ref[0]
        ck = ck_ref[0, 0]

        def step(i, carry, diagonal):
            dk, dv, dck = carry
            rs = pl.ds(i * blk, blk)
            q = q_ref[0, rs, :]
            do = do_ref[0, rs, :]
            p = jnp.exp2(_dot_nt(q, k) - ck - lse_ref[0, rs, :])
            if diagonal:
                p = jnp.where(_lower_triangle(blk), p, 0.0)
            dv = dv + _dot_tn(p.astype(BF16), do)
            ds = p * (_dot_nt(do, v) - dl_ref[0, rs, :])
            dck = dck - jnp.sum(ds, axis=0, keepdims=True)
            dcq_ref[0, rs, :] += jnp.sum(ds, axis=1, keepdims=True)
            dsb = ds.astype(BF16)
            dk = dk + _dot_tn(dsb, q)
            dq_ref[0, rs, :] += _dot(dsb, k) * scale
            return dk, dv, dck
        init = (jnp.zeros((blk, HEAD), F32), jnp.zeros((blk, HEAD), F32), jnp.zeros((1, blk), F32))
        carry = step(kb, init, True)
        dk, dv, dck = lax.fori_loop(kb + 1, nb, lambda i, c: step(i, c, False), carry)
        dk_ref[0] = dk * (1.0 / LOG2E)
        dv_ref[0] = dv
        dc_ref[0, 0] = dck

    full = pl.BlockSpec((1, t, HEAD), lambda h, j: (h, 0, 0))
    colf = pl.BlockSpec((1, t, 1), lambda h, j: (h, 0, 0))
    kb_spec = pl.BlockSpec((1, blk, HEAD), lambda h, j: (h, j, 0))
    rowb = pl.BlockSpec((1, 1, 1, blk), lambda h, j: (h, j, 0, 0))
    return pl.pallas_call(
        body, name="fox_bwd", grid=(nh, nb),
        in_specs=[full, full, colf, colf, kb_spec, kb_spec, rowb],
        out_specs=[full, kb_spec, kb_spec, rowb, colf],
        out_shape=[jax.ShapeDtypeStruct((nh, t, HEAD), F32), jax.ShapeDtypeStruct((nh, t, HEAD), F32),
                   jax.ShapeDtypeStruct((nh, t, HEAD), F32), jax.ShapeDtypeStruct((nh, nb, 1, blk), F32),
                   jax.ShapeDtypeStruct((nh, t, 1), F32)],
        compiler_params=_params(("parallel", "arbitrary")),
    )(q2, do, lse2, delta, k, v, c_row2)


def _split3(x):
    hi = x.astype(BF16)
    r1 = x - hi.astype(F32)
    mid = r1.astype(BF16)
    lo = (r1 - mid.astype(F32)).astype(BF16)
    return hi, mid, lo


def _segsum_raw(a, bm, parts=3):
    outs = []
    for s in range(a.shape[-1] // SEG):
        x = a[:, s * SEG:(s + 1) * SEG]
        if parts == 3:
            hi, mid, lo = _split3(x)
            outs.append(_dot(hi, bm) + _dot(mid, bm) + _dot(lo, bm))
        else:
            hi = x.astype(BF16)
            lo = (x - hi.astype(F32)).astype(BF16)
            outs.append(_dot(hi, bm) + _dot(lo, bm))
    return outs[0] if len(outs) == 1 else jnp.concatenate(outs, axis=-1)


@jax.custom_vjp
def _segsum(a, bm):
    return _segsum_raw(a, bm)


def _segsum_f(a, bm):
    return _segsum_raw(a, bm), bm


def _segsum_b(bm, ct):
    return _segsum_raw(ct, bm), jnp.zeros_like(bm)


_segsum.defvjp(_segsum_f, _segsum_b)


@jax.custom_vjp
def _bdot(a, w):
    return _dot(a.astype(BF16), w.astype(BF16))


def _bdot_f(a, w):
    return _bdot(a, w), (a, w)


def _bdot_b(saved, ct):
    a, w = saved
    ctb = ct.astype(BF16)
    return _dot_nt(ctb, w.astype(BF16)), _dot_tn(a.astype(BF16), ctb)


_bdot.defvjp(_bdot_f, _bdot_b)


def _softplus(z):
    return jnp.maximum(z, 0.0) + jnp.log(1.0 + jnp.exp(-jnp.abs(z)))


def _rwkv_pre_math(hb, hbp, mu, w0, a0, k_k, k_a, w2p, a2p, g2, bm):
    rd = w0.shape[-1]
    m = hb + (hbp - hb) * mu
    r, k, v = m[:, :rd], m[:, rd:2 * rd], m[:, 2 * rd:3 * rd]
    xwa = m[:, 3 * rd:3 * rd + LANES]
    xg = m[:, 3 * rd + LANES:]
    wlog = -_softplus(-(w0 + _bdot(jnp.tanh(xwa), w2p))) - 0.5
    decay = jnp.exp(-jnp.exp(wlog))
    a = _sigmoid(a0 + _bdot(xwa, a2p))
    g = _bdot(_sigmoid(xg), g2)
    kk0 = k * k_k
    kk = kk0 / jnp.maximum(jnp.sqrt(_segsum(kk0 * kk0, bm)), L2_EPS)
    kp = k * (1.0 + (a - 1.0) * k_a)
    return r, decay, kp, v, kk, kk * a, g


def _rwkv_post_math(y, r, kp, v, g, ln_w, ln_b, r_k, bm):
    mean = _segsum(y, bm) * (1.0 / HEAD)
    yc = y - mean
    var = _segsum(yc * yc, bm) * (1.0 / HEAD)
    yn = yc * lax.rsqrt(var + GN_EPS) * ln_w + ln_b
    bonus = _segsum(r * kp * r_k, bm) * v
    return (yn + bonus) * g


def _rwkv_pre(hb, hbp, prm, bm):
    rd = prm[1].shape[-1]
    outs = [(rd, F32)] * 7
    return _rowwise(_rwkv_pre_math, [hb, hbp], list(prm) + [bm], outs, [], tm=256, name="rwkv_pre")


def _rwkv_pre_bwd(hb, hbp, cts, prm, bm):
    n_in = hb.shape[1]

    def fn(hb, hbp, *rest):
        ct, full = rest[:7], rest[7:]
        prm_v, bm_v = full[:-1], full[-1]
        _, vjp = jax.vjp(lambda hb, hbp, *p: _rwkv_pre_math(hb, hbp, *p, bm_v), hb, hbp, *prm_v)
        g = vjp(tuple(ct))
        return g
    acc = [p.shape for p in prm]
    res = _rowwise(fn, [hb, hbp] + list(cts), list(prm) + [bm], [(n_in, F32)] * 2, acc, tm=256, name="rwkv_pre_bwd")
    return res[:2], res[2:]


def _rwkv_post(y, r, kp, v, g, prm, bm):
    rd = y.shape[1]
    return _rowwise(_rwkv_post_math, [y, r, kp, v, g], list(prm) + [bm], [(rd, F32)], [], tm=256, name="rwkv_post")[0]


def _rwkv_post_bwd(y, r, kp, v, g, dout, prm, bm):
    rd = y.shape[1]

    def fn(y, r, kp, v, g, dout, ln_w, ln_b, r_k, bm_v):
        _, vjp = jax.vjp(lambda *a: _rwkv_post_math(*a, bm_v), y, r, kp, v, g, ln_w, ln_b, r_k)
        return vjp(dout)
    acc = [p.shape for p in prm]
    res = _rowwise(fn, [y, r, kp, v, g, dout], list(prm) + [bm], [(rd, F32)] * 5, acc, tm=256, name="rwkv_post_bwd")
    return res[:5], res[5:]


SCAN_FWD_CHUNK = 32
SCAN_BWD_CHUNK = 16
SCAN_PARTS = 2


def _spread_rows(x, mk, bm):
    c, _, rd = x.shape
    hi = x.astype(BF16)
    lo = (x - hi.astype(F32)).astype(BF16)
    keep = mk[None] != 0.0
    tile = lambda p: jnp.where(keep, p, jnp.zeros((), BF16)).reshape(c * HEAD, rd)
    parts = (tile(hi), tile(lo))
    outs = [sum(_dot(p[:, s * SEG:(s + 1) * SEG], bm) for p in parts) for s in range(rd // SEG)]
    out = outs[0] if len(outs) == 1 else jnp.concatenate(outs, axis=-1)
    return out.reshape(c, HEAD, rd)


def _seg3d(x, bm):
    c, n, rd = x.shape
    return _segsum_raw(x.reshape(c * n, rd), bm, SCAN_PARTS).reshape(c, n, rd)


def _head_dots(x, bm):
    n, _, rd = x.shape
    y = _segsum_raw(jnp.broadcast_to(x, (n, 8, rd)).reshape(n * 8, rd), bm, 3).reshape(n, 8, rd)
    return jnp.sum(y, axis=1, keepdims=True) * 0.125


def _rwkv_scan_fwd(w, kk, b, k, v, r, bm, mk, side=()):
    t, _, rd = w.shape
    c = _tile(t, SCAN_FWD_CHUNK, 8)
    assert c % 2 == 0
    ns = len(side)
    steps = t // c

    def body(*refs):
        w_ref, kk_ref, b_ref, k_ref, v_ref, r_ref, bm_ref, mk_ref = refs[:8]
        side_in = refs[8:8 + ns]
        y_ref, s_ref = refs[8 + ns:10 + ns]
        side_out = refs[10 + ns:10 + 2 * ns]
        state, vb, beta, gamma = refs[10 + 2 * ns:14 + 2 * ns]
        if ns:
            start, forward, finish = _gather_phases(side, side_in, side_out, refs[14 + 2 * ns:])
            pl.when(pl.program_id(0) == 0)(start)
            pl.when(pl.program_id(0) == steps // 2)(forward)
            pl.when(pl.program_id(0) == steps - 1)(finish)

        @pl.when(pl.program_id(0) == 0)
        def _():
            state[...] = jnp.zeros_like(state)

        bmv = bm_ref[...]
        mkv = mk_ref[...]
        vb[...] = _spread_rows(v_ref[...], mkv, bmv)
        kk_next = kk_ref[pl.ds(1, c - 1)]
        beta[pl.ds(0, c - 1)] = _head_dots(b_ref[pl.ds(0, c - 1)] * kk_next, bmv)
        gamma[pl.ds(0, c - 1)] = _head_dots(k_ref[pl.ds(0, c - 1)] * kk_next, bmv)

        def pair(p, s):
            ia, ib = 2 * p, 2 * p + 1
            sk_a = _segsum_raw(s * kk_ref[ia], bmv, SCAN_PARTS)
            through = _segsum_raw(s * (w_ref[ia] * kk_ref[ib]), bmv, SCAN_PARTS)
            va = vb[ia]
            s = s * w_ref[ia] - sk_a * b_ref[ia] + va * k_ref[ia]
            s_ref[ia] = s
            sk_b = through - sk_a * beta[ia] + va * gamma[ia]
            s = s * w_ref[ib] - sk_b * b_ref[ib] + vb[ib] * k_ref[ib]
            s_ref[ib] = s
            return s
        state[...] = lax.fori_loop(0, c // 2, pair, state[...])
        yb = _seg3d(s_ref[...] * r_ref[...], bmv)
        y_ref[...] = jnp.sum(yb * mkv[None], axis=1, keepdims=True)

    vec = pl.BlockSpec((c, 1, rd), lambda i: (i, 0, 0))
    res = pl.pallas_call(
        body, name="rwkv_scan_fwd", grid=(steps,),
        in_specs=[vec] * 6 + [pl.BlockSpec((SEG, SEG), lambda i: (0, 0)), pl.BlockSpec((HEAD, rd), lambda i: (0, 0))]
        + [HBM_SPEC] * ns,
        out_specs=[vec, pl.BlockSpec((c, HEAD, rd), lambda i: (i, 0, 0))] + [HBM_SPEC] * ns,
        out_shape=[jax.ShapeDtypeStruct((t, 1, rd), F32), jax.ShapeDtypeStruct((t, HEAD, rd), F32)] + _gather_shapes(side),
        scratch_shapes=[pltpu.VMEM((HEAD, rd), F32), pltpu.VMEM((c, HEAD, rd), F32),
                        pltpu.VMEM((c, 1, rd), F32), pltpu.VMEM((c, 1, rd), F32)] + (_gather_sems(ns) if ns else []),
        compiler_params=_params(("arbitrary",)),
    )(w, kk, b, k, v, r, bm, mk, *side)
    return res[0], res[1], list(res[2:])


def _rwkv_scan_bwd(w, kk, b, k, v, r, dy, states, bm, mk, side=()):
    t, _, rd = w.shape
    c = _tile(t, SCAN_BWD_CHUNK, 8)
    nc = t // c
    assert c % 2 == 0
    ns = len(side)

    def body(*refs):
        w_ref, kk_ref, b_ref, k_ref, v_ref, r_ref, dy_ref, s_ref, sp_ref, bm_ref, mk_ref = refs[:11]
        dr_ref, dw_ref, dk_ref, dv_ref, dkk_ref, db_ref = refs[11 + ns:17 + ns]
        gstate, sp, vb, dyb, skb, gall, gball, delta, eps = refs[17 + 2 * ns:26 + 2 * ns]
        step_id = pl.program_id(0)
        if ns:
            start, finish = _scatter_phases(ns, refs[11:11 + ns], refs[17 + ns:17 + 2 * ns], refs[26 + 2 * ns:])
            pl.when(step_id == 0)(start)
            pl.when(step_id == nc - 1)(finish)

        @pl.when(step_id == 0)
        def _():
            gstate[...] = jnp.zeros_like(gstate)

        bmv = bm_ref[...]
        mkv = mk_ref[...]
        sp[0] = jnp.where(step_id == nc - 1, 0.0, sp_ref[0])
        sp[1:c] = s_ref[0:c - 1]
        vb[...] = _spread_rows(v_ref[...], mkv, bmv)
        dyb[...] = _spread_rows(dy_ref[...], mkv, bmv)
        skb[...] = _seg3d(sp[...] * kk_ref[...], bmv)
        dr_ref[...] = jnp.sum(s_ref[...] * dyb[...], axis=1, keepdims=True)
        delta[pl.ds(0, c - 1)] = _head_dots(kk_ref[pl.ds(1, c - 1)] * b_ref[pl.ds(0, c - 1)], bmv)
        eps[...] = _head_dots(r_ref[...] * b_ref[...], bmv)

        def pair(p, g):
            ib = c - 1 - 2 * p
            ia = ib - 1
            g = g + dyb[ib] * r_ref[ib]
            gall[ib] = g
            gb_b = _segsum_raw(g * b_ref[ib], bmv, SCAN_PARTS)
            through = _segsum_raw(g * (w_ref[ib] * b_ref[ia]), bmv, SCAN_PARTS)
            gball[ib] = gb_b
            dya = dyb[ia]
            g = g * w_ref[ib] - gb_b * kk_ref[ib] + dya * r_ref[ia]
            gall[ia] = g
            gb_a = through - gb_b * delta[ia] + dya * eps[ia]
            gball[ia] = gb_a
            return g * w_ref[ia] - gb_a * kk_ref[ia]
        gstate[...] = lax.fori_loop(0, c // 2, pair, gstate[...])
        ga = gall[...]
        dv_ref[...] = jnp.sum(_seg3d(ga * k_ref[...], bmv) * mkv[None], axis=1, keepdims=True)
        dk_ref[...] = jnp.sum(ga * vb[...], axis=1, keepdims=True)
        dw_ref[...] = jnp.sum(ga * sp[...], axis=1, keepdims=True)
        db_ref[...] = -jnp.sum(ga * skb[...], axis=1, keepdims=True)
        dkk_ref[...] = -jnp.sum(sp[...] * gball[...], axis=1, keepdims=True)

    vec = pl.BlockSpec((c, 1, rd), lambda i: (nc - 1 - i, 0, 0))
    st = pl.BlockSpec((c, HEAD, rd), lambda i: (nc - 1 - i, 0, 0))
    st_prev = pl.BlockSpec((1, HEAD, rd), lambda i: (jnp.maximum((nc - 1 - i) * c - 1, 0), 0, 0))
    big = pltpu.VMEM((c, HEAD, rd), F32)
    res = pl.pallas_call(
        body, name="rwkv_scan_bwd", grid=(nc,),
        in_specs=[vec] * 7 + [st, st_prev, pl.BlockSpec((SEG, SEG), lambda i: (0, 0)),
                              pl.BlockSpec((HEAD, rd), lambda i: (0, 0))] + [HBM_SPEC] * ns,
        out_specs=[vec] * 6 + [HBM_SPEC] * ns,
        out_shape=[jax.ShapeDtypeStruct((t, 1, rd), F32)] * 6 + _scatter_shapes(side),
        scratch_shapes=[pltpu.VMEM((HEAD, rd), F32), big, big, big, big, big, big,
                        pltpu.VMEM((c, 1, rd), F32), pltpu.VMEM((c, 1, rd), F32)] + (_scatter_sems(ns) if ns else []),
        compiler_params=_params(("arbitrary",)),
    )(w, kk, b, k, v, r, dy, states, states, bm, mk, *side)
    return res[:6], list(res[6:])


def _shift_down(a):
    return jnp.concatenate([jnp.zeros_like(a[:1]), a[:-1]], axis=0)


def _rwkv_consts(rd):
    i = jnp.arange(SEG) // HEAD
    bm = (i[:, None] == i[None, :]).astype(BF16)
    mk = (jnp.arange(HEAD)[:, None] == (jnp.arange(rd) % HEAD)[None, :]).astype(F32)
    return bm, mk


def _lora_pad(w2, a2):
    z = jnp.zeros_like(w2)
    return jnp.concatenate([w2, z], axis=0), jnp.concatenate([jnp.zeros_like(a2), a2], axis=0)


def _rwkv_fwd(hb, prm, side=()):
    rd = prm['w0'].shape[-1]
    t = hb.shape[0]
    bm, mk = _rwkv_consts(rd)
    hbp = _shift_down(hb)
    pre_prm = (prm['mu'], prm['w0'], prm['a0'], prm['k_k'], prm['k_a'], prm['w2p'], prm['a2p'], prm['g2'])
    r, w, kp, v, kk, b, g = _rwkv_pre(hb, hbp, pre_prm, bm)
    to3 = lambda a: a.reshape(t, 1, rd)
    y3, states, gathered = _rwkv_scan_fwd(to3(w), to3(kk), to3(b), to3(kp), to3(v), to3(r), bm, mk, side)
    y = y3.reshape(t, rd)
    post_prm = (prm['ln_w'], prm['ln_b'], prm['r_k'])
    out = _rwkv_post(y, r, kp, v, g, post_prm, bm)
    return out, (hb, hbp, r, w, kp, v, kk, b, g, y, states), gathered


def _rwkv_bwd(dout, saved, prm, side=()):
    hb, hbp, r, w, kp, v, kk, b, g, y, states = saved
    rd = prm['w0'].shape[-1]
    t = hb.shape[0]
    bm, mk = _rwkv_consts(rd)
    post_prm = (prm['ln_w'], prm['ln_b'], prm['r_k'])
    (dy, dr1, dkp1, dv1, dg), (d_ln_w, d_ln_b, d_r_k) = _rwkv_post_bwd(y, r, kp, v, g, dout, post_prm, bm)
    to3 = lambda a: a.reshape(t, 1, rd)
    (dr2, dw, dk2, dv2, dkk, db), received = _rwkv_scan_bwd(to3(w), to3(kk), to3(b), to3(kp), to3(v), to3(r), to3(dy), states,
                                                            bm, mk, side)
    to2 = lambda a: a.reshape(t, rd)
    cts = [dr1 + to2(dr2), to2(dw), dkp1 + to2(dk2), dv1 + to2(dv2), to2(dkk), to2(db), dg]
    pre_prm = (prm['mu'], prm['w0'], prm['a0'], prm['k_k'], prm['k_a'], prm['w2p'], prm['a2p'], prm['g2'])
    (dhb, dhbp), gp = _rwkv_pre_bwd(hb, hbp, cts, pre_prm, bm)
    dhb = dhb + jnp.concatenate([dhbp[1:], jnp.zeros_like(dhbp[:1])], axis=0)
    d_mu, d_w0, d_a0, d_k_k, d_k_a, d_w2p, d_a2p, d_g2 = gp
    grads = {'rwkv_mu': d_mu, 'rwkv_w0': d_w0, 'rwkv_a0': d_a0, 'rwkv_k_k': d_k_k, 'rwkv_k_a': d_k_a,
             'rwkv_w2': d_w2p[:DECAY_LORA], 'rwkv_a2': d_a2p[DECAY_LORA:], 'rwkv_g2': d_g2,
             'rwkv_ln_w': d_ln_w, 'rwkv_ln_b': d_ln_b, 'rwkv_r_k': d_r_k}
    return dhb, grads, received


def _even_fwd(x, g, w_in, w_out, sinks, slopes, rprm, side=()):
    d = x.shape[1]
    q_w, kv_w = d // 2, d // 8
    proj = _matmul(x, w_in, norm=g, name="even_in")
    qa, ka, va, hb = proj[:, :q_w], proj[:, q_w:q_w + kv_w], proj[:, q_w + kv_w:q_w + 2 * kv_w], proj[:, q_w + 2 * kv_w:]
    qh, kh, vh = _heads(qa).astype(BF16), _heads(ka).astype(BF16), _heads(va).astype(BF16)
    oa, lse = _swa_fwd(qh, kh, vh, sinks, slopes)
    yb, rsaved, gathered = _rwkv_fwd(hb, rprm, side)
    cat = jnp.concatenate([_unheads(oa), yb], axis=1)
    x2 = _matmul(cat, w_out, res=x, name="even_out")
    return x2, (x, qh, kh, vh, oa, lse, cat, rsaved), gathered


def _even_bwd(dx2, saved, g, w_in, w_out, sinks, slopes, rprm, side=()):
    x, qh, kh, vh, oa, lse, cat, rsaved = saved
    d = x.shape[1]
    dcat = _matmul(dx2, w_out, tb=True, name="even_dcat")
    d_out = _matmul(cat, dx2, ta=True, name="even_dwout")
    dya, dyb = dcat[:, :d // 2], dcat[:, d // 2:]
    dq, dk2, dv2, dsk = _swa_bwd(qh, kh, vh, sinks, slopes, oa, lse, _heads(dya))
    d_sinks = jnp.sum(dsk[:, :, :SWA_GROUP, 0], axis=1).reshape(1, -1)
    dhb, rgrads, received = _rwkv_bwd(dyb, rsaved, rprm, side)
    dproj = jnp.concatenate([_unheads(dq), _unheads(_fold_kv(dk2)), _unheads(_fold_kv(dv2)), dhb], axis=1)
    d_in = _matmul(x, dproj, ta=True, norm=g, name="even_dwin")
    dx, dg = _matmul(dproj, w_in, tb=True, norm_bwd=(x, g, dx2), name="even_dhn")
    return dx, dg, d_in, d_out, d_sinks, rgrads, received


def _odd_fwd(x, g, w_in, w_out, b_f):
    d = x.shape[1]
    t = x.shape[0]
    nh = d // HEAD
    qkv = _matmul(x, w_in[:, :3 * d], norm=g, name="odd_in")
    fz = _matmul(x, w_in[:, 3 * d:], norm=g, name="odd_fz")
    c = _fox_gate_fwd(fz, b_f)
    blk = _tile(t, FOX_BLOCK, 128)
    c_row = (c.T * LOG2E).reshape(nh, t // blk, 1, blk)
    qh = _heads(qkv[:, :d] * FOX_QSCALE).astype(BF16)
    kh, vh = (_heads(qkv[:, i * d:(i + 1) * d]).astype(BF16) for i in (1, 2))
    o, lse = _fox_fwd(qh, kh, vh, c_row)
    y = _unheads(o)
    x2 = _matmul(y, w_out, res=x, name="odd_out")
    return x2, (x, fz, qh, kh, vh, c_row, o, lse, y)


def _odd_bwd(dx2, saved, g, w_in, w_out, b_f):
    x, fz, qh, kh, vh, c_row, o, lse, y = saved
    d = x.shape[1]
    t = x.shape[0]
    nh = d // HEAD
    dy = _matmul(dx2, w_out, tb=True, name="odd_dy")
    d_out = _matmul(y, dx2, ta=True, name="odd_dwout")
    doh = _heads(dy)
    delta = _rowwise(lambda a, b: jnp.sum(a * b, axis=-1, keepdims=True),
                     [o.reshape(nh * t, HEAD), doh.reshape(nh * t, HEAD)], [], [(1, F32)], [], tm=1024,
                     name="fox_delta")[0].reshape(nh, t, 1)
    dq, dk, dv, dcr, dcc = _fox_bwd(qh, kh, vh, c_row, lse, delta, doh.astype(BF16))
    dfz, d_bf = _fox_gate_bwd(fz, b_f, (dcr.reshape(nh, t) + dcc.reshape(nh, t)).T)
    dqkv = jnp.concatenate([_unheads(dq), _unheads(dk), _unheads(dv)], axis=1)
    d_in = jnp.concatenate([_matmul(x, dqkv, ta=True, norm=g, name="odd_dwin"),
                            _matmul(x, dfz, ta=True, norm=g, name="odd_dwin_fz")], axis=1)
    dhn_fz = _matmul(dfz, w_in[:, 3 * d:], tb=True, name="odd_dhn_fz")
    dx, dg = _matmul(dqkv, w_in[:, :3 * d], tb=True, res=dhn_fz, norm_bwd=(x, g, dx2), name="odd_dhn")
    return dx, dg, d_in, d_out, d_bf


def _place():
    return lax.axis_index("x"), lax.axis_index("y"), lax.axis_index("c")


def _other_chips(x, y):
    return [(1 - x, y), (x, 1 - y), (1 - x, 1 - y)]


HBM_SPEC = pl.BlockSpec(memory_space=pltpu.HBM)


def _rows(ref, which, h):
    return ref.at[pl.ds(which * h, h)]


def _gather_phases(shards, ins, outs, sems):
    ici_send, ici_recv, d2d_send, d2d_recv = sems
    x, y, c = _place()
    me = 2 * x + y
    sibling = (x, y, 1 - c)
    pairs = [(i, j, px, py) for i in range(len(shards)) for j, (px, py) in enumerate(_other_chips(x, y))]
    half = lambda i, ref, which: _rows(ref, which, shards[i].shape[0] // 2)

    def over_ici(i, j, px, py, slot):
        return pltpu.make_async_remote_copy(
            src_ref=half(i, ins[i], c), dst_ref=half(i, outs[i].at[slot], c), send_sem=ici_send.at[3 * i + j],
            recv_sem=ici_recv.at[3 * i + j], device_id=(px, py, c), device_id_type=MESH)

    def over_d2d(i, j, px, py, which):
        part = half(i, outs[i].at[2 * px + py], which)
        return pltpu.make_async_remote_copy(src_ref=part, dst_ref=part, send_sem=d2d_send.at[3 * i + j],
                                            recv_sem=d2d_recv.at[3 * i + j], device_id=sibling, device_id_type=MESH)

    def start():
        for i, j, px, py in pairs:
            over_ici(i, j, px, py, me).start()

    def forward():
        for i, j, px, py in pairs:
            over_ici(i, j, px, py, 2 * px + py).wait_recv()
            over_d2d(i, j, px, py, c).start()

    def finish():
        for i, j, px, py in pairs:
            over_d2d(i, j, px, py, 1 - c).wait_recv()
        for i, j, px, py in pairs:
            over_ici(i, j, px, py, me).wait_send()
            over_d2d(i, j, px, py, c).wait_send()

    return start, forward, finish


def _gather_sems(n):
    return [pltpu.SemaphoreType.DMA((3 * n,))] * 4


def _gather_shapes(shards):
    return [jax.ShapeDtypeStruct((N_CHIPS,) + s.shape, s.dtype) for s in shards]


def _gather_chips(shards):
    n = len(shards)

    def body(*refs):
        start, forward, finish = _gather_phases(shards, refs[:n], refs[n:2 * n], refs[2 * n:])
        start()
        forward()
        finish()

    return pl.pallas_call(
        body, name="gather_weights", in_specs=[HBM_SPEC] * n, out_specs=[HBM_SPEC] * n,
        out_shape=_gather_shapes(shards), scratch_shapes=_gather_sems(n),
    )(*shards)


def _swap_halves(stacked):
    n = len(stacked)
    halves = [s.shape[1] // 2 for s in stacked]

    def body(*refs):
        ins, outs = refs[:n], refs[n:2 * n]
        send_sems, recv_sems = refs[2 * n:]
        x, y, c = _place()
        sends = []
        for i in range(n):
            cp = pltpu.make_async_remote_copy(
                src_ref=ins[i].at[:, pl.ds((1 - c) * halves[i], halves[i])], dst_ref=outs[i], send_sem=send_sems.at[i],
                recv_sem=recv_sems.at[i], device_id=(x, y, 1 - c), device_id_type=MESH)
            cp.start()
            sends.append(cp)
        for cp in sends:
            cp.wait_recv()
        for cp in sends:
            cp.wait_send()

    return pl.pallas_call(
        body, name="swap_halves", in_specs=[HBM_SPEC] * n, out_specs=[HBM_SPEC] * n,
        out_shape=[jax.ShapeDtypeStruct((N_CHIPS, h) + s.shape[2:], s.dtype) for s, h in zip(stacked, halves)],
        scratch_shapes=[pltpu.SemaphoreType.DMA((n,)), pltpu.SemaphoreType.DMA((n,))],
    )(*stacked)


def _scatter_phases(n, ins, outs, sems):
    send_sems, recv_sems = sems
    x, y, c = _place()
    pairs = [(i, j, px, py) for i in range(n) for j, (px, py) in enumerate(_other_chips(x, y))]

    def copy(i, j, px, py):
        return pltpu.make_async_remote_copy(src_ref=ins[i].at[2 * px + py], dst_ref=outs[i].at[j], send_sem=send_sems.at[3 * i + j],
                                            recv_sem=recv_sems.at[3 * i + j], device_id=(px, py, c), device_id_type=MESH)

    def start():
        for p in pairs:
            copy(*p).start()

    def finish():
        for p in pairs:
            copy(*p).wait_recv()
        for p in pairs:
            copy(*p).wait_send()

    return start, finish


def _scatter_sems(n):
    return [pltpu.SemaphoreType.DMA((3 * n,))] * 2


def _scatter_shapes(stacked):
    return [jax.ShapeDtypeStruct((3,) + s.shape[1:], s.dtype) for s in stacked]


def _scatter_chips(stacked):
    n = len(stacked)

    def body(*refs):
        start, finish = _scatter_phases(n, refs[:n], refs[n:2 * n], refs[2 * n:])
        start()
        finish()

    return pl.pallas_call(
        body, name="scatter_grads", in_specs=[HBM_SPEC] * n, out_specs=[HBM_SPEC] * n,
        out_shape=_scatter_shapes(stacked), scratch_shapes=_scatter_sems(n),
    )(*stacked)


def _swap_cores(arrs):
    n = len(arrs)

    def body(*refs):
        ins, outs = refs[:n], refs[n:2 * n]
        send_sems, recv_sems = refs[2 * n:]
        x, y, c = _place()
        sends = []
        for i in range(n):
            cp = pltpu.make_async_remote_copy(src_ref=ins[i], dst_ref=outs[i], send_sem=send_sems.at[i], recv_sem=recv_sems.at[i],
                                              device_id=(x, y, 1 - c), device_id_type=MESH)
            cp.start()
            sends.append(cp)
        for cp in sends:
            cp.wait_recv()
        for cp in sends:
            cp.wait_send()

    sem = pltpu.SemaphoreType.DMA((n,))
    return pl.pallas_call(
        body, name="swap_cores", in_specs=[HBM_SPEC] * n, out_specs=[HBM_SPEC] * n,
        out_shape=[jax.ShapeDtypeStruct(s.shape, s.dtype) for s in arrs],
        scratch_shapes=[sem, sem],
    )(*arrs)


def _allreduce_small(buf):
    rows = buf.shape[0]

    def body(in_ref, out_ref, gat, send_sems, recv_sems):
        x, y, c = _place()
        me = 4 * x + 2 * y + c
        gat[me] = in_ref[...]
        sends = []
        for k in range(1, N_DEV):
            bx, by, bc = (k >> 2) & 1, (k >> 1) & 1, k & 1
            peer = (x ^ bx, y ^ by, c ^ bc)
            cp = pltpu.make_async_remote_copy(src_ref=in_ref, dst_ref=gat.at[me], send_sem=send_sems.at[k - 1],
                                              recv_sem=recv_sems.at[k - 1], device_id=peer, device_id_type=MESH)
            cp.start()
            sends.append((cp, 4 * peer[0] + 2 * peer[1] + peer[2]))
        for k, (cp, slot) in enumerate(sends):
            pltpu.make_async_remote_copy(src_ref=in_ref, dst_ref=gat.at[slot], send_sem=send_sems.at[k], recv_sem=recv_sems.at[k],
                                         device_id=(x, y, c), device_id_type=MESH).wait_recv()
        for cp, _ in sends:
            cp.wait_send()
        acc = gat[0]
        for k in range(1, N_DEV):
            acc = acc + gat[k]
        out_ref[...] = acc

    vm = pl.BlockSpec(memory_space=pltpu.VMEM)
    return pl.pallas_call(
        body, name="allreduce_small", in_specs=[vm], out_specs=vm, out_shape=jax.ShapeDtypeStruct(buf.shape, F32),
        scratch_shapes=[pltpu.VMEM((N_DEV, rows, LANES), F32), pltpu.SemaphoreType.DMA((N_DEV - 1,)), pltpu.SemaphoreType.DMA((N_DEV - 1,))],
        compiler_params=_params(),
    )(buf)


def _as2d(a):
    return a.reshape(-1, a.shape[-1])


def _cast_bf16(a, name):
    a2 = _as2d(a)
    out = _rowwise(lambda v: v, [a2], [], [(a2.shape[1], BF16)], [], tm=512, name=name)[0]
    return out.reshape(a.shape)


def _assemble(gathered, axis):
    _, l, r, c = gathered.shape
    if axis == 1:
        return gathered.transpose(1, 0, 2, 3).reshape(l, N_CHIPS * r, c)
    return gathered.transpose(1, 2, 0, 3).reshape(l, r, N_CHIPS * c)


def _split_shards(full, axis):
    l, r, c = full.shape
    if axis == 1:
        return full.reshape(l, N_CHIPS, r // N_CHIPS, c).transpose(1, 0, 2, 3)
    return full.reshape(l, r, N_CHIPS, c // N_CHIPS).transpose(2, 0, 1, 3)


def _adamw_math(w, g, m, v):
    m2 = ADAM_B1 * m + (1.0 - ADAM_B1) * g
    v2 = ADAM_B2 * v + (1.0 - ADAM_B2) * (g * g)
    m_hat = m2 / (1.0 - ADAM_B1 ** ADAM_STEP)
    v_hat = v2 / (1.0 - ADAM_B2 ** ADAM_STEP)
    delta = -ADAM_LR * (m_hat / (jnp.sqrt(v_hat) + ADAM_EPS) + ADAM_WD * w)
    return delta, m2, v2


def _adamw_big(w, m, v, mine, other, core, name):
    shape = w.shape
    wd = shape[-1]
    h = mine.shape[0]
    tm = _tile(h, 256, 16)
    nh = h // tm

    def body(core_ref, w_ref, m_ref, v_ref, a_ref, b_ref, g_ref, d_ref, mo_ref, vo_ref):
        g = jnp.where(pl.program_id(0) // nh == core_ref[0], a_ref[...], b_ref[...])
        g_ref[...] = g
        d_ref[...], mo_ref[...], vo_ref[...] = _adamw_math(w_ref[...], g, m_ref[...], v_ref[...])

    rows = pl.BlockSpec((tm, wd), lambda i: (i, 0))
    half = pl.BlockSpec((tm, wd), lambda i: (i % nh, 0))
    outs = pl.pallas_call(
        body, name=name, grid=(2 * nh,),
        in_specs=[pl.BlockSpec(memory_space=pltpu.SMEM), rows, rows, rows, half, half], out_specs=[rows] * 4,
        out_shape=[jax.ShapeDtypeStruct((2 * h, wd), F32)] * 4,
        compiler_params=_params(("parallel",)),
    )(core, _as2d(w), _as2d(m), _as2d(v), mine, other)
    return [o.reshape(shape) for o in outs]


def _add2(a, b, name):
    wd = a.shape[-1]
    out = _rowwise(lambda p, q: p + q, [_as2d(a), _as2d(b)], [], [(wd, BF16)], [], tm=256, name=name)[0]
    return out.reshape(a.shape)


def _sum4(mine, recv, name):
    wd = mine.shape[-1]
    up = lambda v: v.astype(F32)
    return _rowwise(lambda a, b, c, d: ((up(a) + up(b)) + up(c)) + up(d), [mine, recv[0], recv[1], recv[2]], [], [(wd, F32)], [],
                    tm=256, name=name)[0]


def _pack(arrs):
    parts = []
    for a in arrs:
        f = a.reshape(-1).astype(F32)
        parts.append(jnp.pad(f, (0, (-f.shape[0]) % LANES)))
    flat = jnp.concatenate(parts)
    flat = jnp.pad(flat, (0, (-flat.shape[0]) % (8 * LANES)))
    return flat.reshape(-1, LANES)


def _unpack(buf, like):
    flat = buf.reshape(-1)
    out, off = [], 0
    for a in like:
        n = math.prod(a.shape)
        out.append(flat[off:off + n].reshape(a.shape))
        off += n + (-n) % LANES
    return out


def kernel(x, p, ffn1_norm, ffn1_w_gu, ffn1_w_down, mix_norm, ffn2_norm, ffn2_w_gu, ffn2_w_down, ple_norm, ple_w_gate, ple_w_proj, even_w_in, even_w_out, swa_sinks, rwkv_mu, rwkv_w0, rwkv_w2, rwkv_a0, rwkv_a2, rwkv_g2, rwkv_k_k, rwkv_k_a, rwkv_r_k, rwkv_ln_w, rwkv_ln_b, fox_w_in, fox_b_f, fox_w_out, final_norm, loss_target, m_ffn1_norm, m_ffn1_w_gu, m_ffn1_w_down, m_mix_norm, m_ffn2_norm, m_ffn2_w_gu, m_ffn2_w_down, m_ple_norm, m_ple_w_gate, m_ple_w_proj, m_even_w_in, m_even_w_out, m_swa_sinks, m_rwkv_mu, m_rwkv_w0, m_rwkv_w2, m_rwkv_a0, m_rwkv_a2, m_rwkv_g2, m_rwkv_k_k, m_rwkv_k_a, m_rwkv_r_k, m_rwkv_ln_w, m_rwkv_ln_b, m_fox_w_in, m_fox_b_f, m_fox_w_out, m_final_norm, v_ffn1_norm, v_ffn1_w_gu, v_ffn1_w_down, v_mix_norm, v_ffn2_norm, v_ffn2_w_gu, v_ffn2_w_down, v_ple_norm, v_ple_w_gate, v_ple_w_proj, v_even_w_in, v_even_w_out, v_swa_sinks, v_rwkv_mu, v_rwkv_w0, v_rwkv_w2, v_rwkv_a0, v_rwkv_a2, v_rwkv_g2, v_rwkv_k_k, v_rwkv_k_a, v_rwkv_r_k, v_rwkv_ln_w, v_rwkv_ln_b, v_fox_w_in, v_fox_b_f, v_fox_w_out, v_final_norm):
    args = locals()
    wts = {n: args[n] for n in WEIGHTS}
    mom = {n: args['m_' + n] for n in WEIGHTS}
    var = {n: args['v_' + n] for n in WEIGHTS}
    xs = x[0]
    tgt = loss_target[0]
    t, d = xs.shape
    depth = ffn1_norm.shape[0]
    rd = d // 2
    row = lambda a: a.reshape(1, -1)

    names = BIG + LORA
    chip = 2 * lax.axis_index("x") + lax.axis_index("y")
    cast = {n: _cast_bf16(wts[n], f"cast_{n}") for n in names}
    items = [(n, i) for n in names for i in range(wts[n].shape[0])]
    early = lambda n, i: (n in ('ffn1_w_gu', 'ffn1_w_down') and i == 0) or n in ('even_w_in', 'even_w_out') or n in LORA
    first = [it for it in items if early(*it)]
    later = [it for it in items if not early(*it)]
    full = {n: [None] * wts[n].shape[0] for n in names}

    def place(group, gathered):
        for (n, i), g in zip(group, gathered):
            g = lax.dynamic_update_index_in_dim(g, cast[n][i], chip, 0)
            full[n][i] = _assemble(g[:, None], SHARDED[n])[0]

    place(first, _gather_chips([cast[n][i] for n, i in first]))

    n_swa = d // (2 * HEAD)
    slopes = 2.0 ** (-8.0 * jnp.arange(1, n_swa + 1, dtype=F32) / n_swa)
    w2p, a2p = _lora_pad(full['rwkv_w2'][0].astype(F32), full['rwkv_a2'][0].astype(F32))
    rprm = {'mu': rwkv_mu, 'w0': rwkv_w0, 'a0': rwkv_a0, 'k_k': rwkv_k_k, 'k_a': rwkv_k_a, 'w2p': w2p, 'a2p': a2p,
            'g2': full['rwkv_g2'][0].astype(F32), 'ln_w': rwkv_ln_w, 'ln_b': rwkv_ln_b, 'r_k': rwkv_r_k.reshape(1, rd)}

    saved = []
    h = xs
    for i in range(depth):
        h, s1 = _ffn_fwd(h, row(ffn1_norm[i]), full['ffn1_w_gu'][i], full['ffn1_w_down'][i], f"l{i}_ffn1")
        if i % 2 == 0:
            h, sm, gathered = _even_fwd(h, row(mix_norm[i]), full['even_w_in'][i // 2], full['even_w_out'][i // 2],
                                        swa_sinks[i // 2], slopes, rprm, [cast[n][k] for n, k in later])
            place(later, gathered)
        else:
            h, sm = _odd_fwd(h, row(mix_norm[i]), full['fox_w_in'][i // 2], full['fox_w_out'][i // 2], row(fox_b_f[i // 2]))
        h, s2 = _ffn_fwd(h, row(ffn2_norm[i]), full['ffn2_w_gu'][i], full['ffn2_w_down'][i], f"l{i}_ffn2")
        h, sp = _ple_fwd(h, row(ple_norm[i]), full['ple_w_gate'][i], p[i, 0], full['ple_w_proj'][i], f"l{i}_ple")
        saved.append((s1, sm, s2, sp))
    dx, loss_tile, d_final = _final_loss(h, row(final_norm), tgt)

    core = lax.axis_index("c")
    gl = {n: [None] * depth for n in ['ffn1_norm', 'ffn1_w_gu', 'ffn1_w_down', 'mix_norm', 'ffn2_norm', 'ffn2_w_gu',
                                      'ffn2_w_down', 'ple_norm', 'ple_w_gate', 'ple_w_proj']}
    g1 = {}
    local = {}

    def settle(n):
        if n in gl:
            per_layer = gl[n]
            local[n] = jnp.stack(per_layer).reshape((depth,) + per_layer[0].shape[-2:]) if per_layer[0].shape[0] != 1 \
                else jnp.concatenate(per_layer, axis=0)
        else:
            g = g1[n]
            local[n] = g.reshape((1,) + g.shape) if g.ndim == 2 and wts[n].ndim == 3 else g

    def chip_partials(group):
        for n in group:
            settle(n)
        stacked = [_split_shards(local[n], SHARDED[n]) for n in group]
        stacked = [s.reshape(N_CHIPS, -1, s.shape[-1]) for s in stacked]
        from_sibling = _swap_halves(stacked)
        parts = []
        for n, s, o in zip(group, stacked, from_sibling):
            h = o.shape[1]
            parts.append(_add2(lax.dynamic_slice_in_dim(s, core * h, h, axis=1), o, f"pair_{n}"))
        return parts

    def own_halves(group, parts, received):
        mine = [lax.dynamic_index_in_dim(s, chip, axis=0, keepdims=False) for s in parts]
        return [_sum4(a, r, f"sum_{n}") for n, a, r in zip(group, mine, received)]

    early = [n for n in BIG if n.startswith(('ffn2_', 'ple_', 'fox_'))]
    late = [n for n in BIG + LORA if n not in early]
    for i in reversed(range(depth)):
        s1, sm, s2, sp = saved[i]
        dx, gl['ple_norm'][i], gl['ple_w_gate'][i], gl['ple_w_proj'][i] = _ple_bwd(
            dx, sp, row(ple_norm[i]), full['ple_w_gate'][i], p[i, 0], f"l{i}_ple")
        dx, gl['ffn2_norm'][i], gl['ffn2_w_gu'][i], gl['ffn2_w_down'][i] = _ffn_bwd(
            dx, s2, row(ffn2_norm[i]), full['ffn2_w_gu'][i], full['ffn2_w_down'][i], f"l{i}_ffn2")
        if i % 2 == 0:
            early_parts = chip_partials(early)
            dx, gl['mix_norm'][i], g1['even_w_in'], g1['even_w_out'], g1['swa_sinks'], rg, early_recv = _even_bwd(
                dx, sm, row(mix_norm[i]), full['even_w_in'][i // 2], full['even_w_out'][i // 2], swa_sinks[i // 2], slopes, rprm,
                early_parts)
            g1.update(rg)
        else:
            dx, gl['mix_norm'][i], g1['fox_w_in'], g1['fox_w_out'], g1['fox_b_f'] = _odd_bwd(
                dx, sm, row(mix_norm[i]), full['fox_w_in'][i // 2], full['fox_w_out'][i // 2], row(fox_b_f[i // 2]))
        dx, gl['ffn1_norm'][i], gl['ffn1_w_gu'][i], gl['ffn1_w_down'][i] = _ffn_bwd(
            dx, s1, row(ffn1_norm[i]), full['ffn1_w_gu'][i], full['ffn1_w_down'][i], f"l{i}_ffn1")
    grad_x = dx.reshape(x.shape)

    late_parts = chip_partials(late)
    halves = own_halves(early, early_parts, early_recv) + own_halves(late, late_parts, _scatter_chips(late_parts))
    others = _swap_cores(halves)
    core1 = core.astype(jnp.int32).reshape(1)
    out_g, out_d, out_m, out_v = {}, {}, {}, {}
    for n, a, o in zip(early + late, halves, others):
        out_g[n], out_d[n], out_m[n], out_v[n] = _adamw_big(wts[n], mom[n], var[n], a, o, core1, f"adamw_{n}")

    for n in SMALL:
        if n != 'final_norm':
            settle(n)
    local['final_norm'] = d_final
    small_like = [wts[n] for n in SMALL]
    packed = _pack([local[n] for n in SMALL] + [loss_tile[0, :1]])
    parts = _unpack(_allreduce_small(packed), small_like + [loss_tile[0, :1]])
    g_small = dict(zip(SMALL, parts[:len(SMALL)]))
    loss = parts[-1].reshape(())
    wp, gp, mp, vp = (_pack([src[n] for n in SMALL]) for src in (wts, g_small, mom, var))
    d_p, m_p, v_p = _rowwise(_adamw_math, [wp, gp, mp, vp], [], [(LANES, F32)] * 3, [], tm=512, name="adamw_small")
    out_g.update(g_small)
    out_d.update(zip(SMALL, _unpack(d_p, small_like)))
    out_m.update(zip(SMALL, _unpack(m_p, small_like)))
    out_v.update(zip(SMALL, _unpack(v_p, small_like)))

    fit = lambda dct: [dct[n].reshape(wts[n].shape) for n in WEIGHTS]
    return (loss, grad_x, *fit(out_g), *fit(out_d), *fit(out_m), *fit(out_v))
```

```python
import functools
import math

import jax
import jax.numpy as jnp
from jax import lax
from jax.experimental import pallas as pl
from jax.experimental.pallas import tpu as pltpu

F32 = jnp.float32
BF16 = jnp.bfloat16
MESH = pl.DeviceIdType.MESH

HEAD = 64
SWA_BLOCK = 128
SWA_GROUP = 4
DECAY_LORA = 64
ICLR_LORA = 64
GATE_LORA = 128
NORM_EPS = 1e-6
GN_EPS = 64e-5
L2_EPS = 1e-12
NEG = -1e30

ADAM_LR = 0.001
ADAM_B1 = 0.9
ADAM_B2 = 0.999
ADAM_EPS = 1e-08
ADAM_WD = 0.01
ADAM_STEP = 10

VMEM_LIMIT = 48 * 1024 * 1024
LANES = 128
SEG = 256
MM_TILE = 1408

WEIGHTS = ['ffn1_norm', 'ffn1_w_gu', 'ffn1_w_down', 'mix_norm', 'ffn2_norm', 'ffn2_w_gu', 'ffn2_w_down',
           'ple_norm', 'ple_w_gate', 'ple_w_proj', 'even_w_in', 'even_w_out', 'swa_sinks', 'rwkv_mu',
           'rwkv_w0', 'rwkv_w2', 'rwkv_a0', 'rwkv_a2', 'rwkv_g2', 'rwkv_k_k', 'rwkv_k_a', 'rwkv_r_k',
           'rwkv_ln_w', 'rwkv_ln_b', 'fox_w_in', 'fox_b_f', 'fox_w_out', 'final_norm']
SHARDED = {'ffn1_w_gu': 2, 'ffn1_w_down': 1, 'ffn2_w_gu': 2, 'ffn2_w_down': 1, 'ple_w_gate': 1,
           'ple_w_proj': 2, 'even_w_in': 2, 'even_w_out': 1, 'fox_w_in': 2, 'fox_w_out': 1,
           'rwkv_w2': 2, 'rwkv_a2': 2, 'rwkv_g2': 2}
LORA = ['rwkv_w2', 'rwkv_a2', 'rwkv_g2']
BIG = [n for n in WEIGHTS if n in SHARDED and n not in LORA]
SMALL = [n for n in WEIGHTS if n not in SHARDED]
N_CHIPS = 4
N_DEV = 8


def _tile(dim, target, align):
    best = None
    t = align
    while t <= min(dim, target):
        if dim % t == 0:
            best = t
        t += align
    return best if best is not None else dim


def _params(sem=None):
    return pltpu.CompilerParams(dimension_semantics=sem, vmem_limit_bytes=VMEM_LIMIT)


def _matmul(a, b, *, ta=False, tb=False, alpha=1.0, res=None, out_dtype=F32, name, norm=None, norm_bwd=None):
    if ta:
        kdim, m = a.shape
    else:
        m, kdim = a.shape
    if tb:
        n, kb = b.shape
    else:
        kb, n = b.shape
    assert kdim == kb, (a.shape, b.shape, ta, tb)
    tm = _tile(m, MM_TILE if norm_bwd is None else MM_TILE // 2, 128 if ta else 16)
    tn = _tile(n, MM_TILE, 128)
    tk = _tile(kdim, MM_TILE, 128)
    nk = kdim // tk
    if norm is not None:
        assert (tm == m) if ta else (tk == kdim), "the normalised tile must span whole feature rows"
    if norm_bwd is not None:
        assert tn == n and out_dtype == F32
    a_spec = pl.BlockSpec((tk, tm), lambda j, i, k: (k, i)) if ta else pl.BlockSpec((tm, tk), lambda j, i, k: (i, k))
    b_spec = pl.BlockSpec((tn, tk), lambda j, i, k: (j, k)) if tb else pl.BlockSpec((tk, tn), lambda j, i, k: (k, j))
    o_spec = pl.BlockSpec((tm, tn), lambda j, i, k: (i, j))
    whole = lambda arr: pl.BlockSpec(arr.shape, lambda j, i, k: (0, 0))
    dims = (((0 if ta else 1,), (1 if tb else 0,)), ((), ()))
    ins, in_specs = [a, b], [a_spec, b_spec]
    if norm is not None:
        ins.append(norm)
        in_specs.append(whole(norm))
    if res is not None:
        ins.append(res)
        in_specs.append(o_spec)
    if norm_bwd is not None:
        ins += list(norm_bwd)
        in_specs += [o_spec, whole(norm_bwd[1]), o_spec]
    n_in = len(ins)

    def body(*refs):
        a_ref, b_ref = refs[:2]
        rest = list(refs[2:n_in])
        outs = refs[n_in:]
        av = a_ref[...]
        if norm is not None:
            av = _rms_math(av, rest.pop(0)[...])
        prod = lax.dot_general(av.astype(BF16), b_ref[...].astype(BF16), dims, preferred_element_type=F32)

        def finish(acc):
            o = acc * alpha
            tail = list(rest)
            if res is not None:
                o = o + tail.pop(0)[...]
            if norm_bwd is None:
                outs[0][...] = o.astype(out_dtype)
                return
            x_ref, g_ref, dx_ref = tail
            dx, dg = _rms_bwd_math(x_ref[...], g_ref[...], o)
            outs[0][...] = dx_ref[...] + dx
            first = (pl.program_id(0) == 0) & (pl.program_id(1) == 0)

            @pl.when(first)
            def _():
                outs[1][...] = jnp.zeros_like(outs[1])
            outs[1][...] += dg

        if nk == 1:
            finish(prod)
        else:
            acc_ref = outs[-1]
            k = pl.program_id(2)

            @pl.when(k == 0)
            def _():
                acc_ref[...] = jnp.zeros_like(acc_ref)

            acc_ref[...] += prod
            pl.when(k == nk - 1)(lambda: finish(acc_ref[...]))

    out_specs, out_shape = [o_spec], [jax.ShapeDtypeStruct((m, n), out_dtype)]
    if norm_bwd is not None:
        out_specs.append(pl.BlockSpec((1, n), lambda j, i, k: (0, 0)))
        out_shape.append(jax.ShapeDtypeStruct((1, n), F32))
    sem = ("parallel", "parallel", "arbitrary") if norm_bwd is None else ("arbitrary",) * 3
    outs = pl.pallas_call(
        body, name=name, grid=(n // tn, m // tm, nk), in_specs=in_specs, out_specs=out_specs, out_shape=out_shape,
        scratch_shapes=[] if nk == 1 else [pltpu.VMEM((tm, tn), F32)],
        compiler_params=_params(sem),
    )(*ins)
    return outs[0] if norm_bwd is None else outs


def _rowwise(fn, tiled, full, tiled_out, acc_out, *, tm, name):
    rows = tiled[0].shape[0]
    tm = _tile(rows, tm, 16)
    nt, nf, no, na = len(tiled), len(full), len(tiled_out), len(acc_out)

    def body(*refs):
        ins = [r[...] for r in refs[:nt + nf]]
        outs = fn(*ins)
        if not isinstance(outs, (tuple, list)):
            outs = (outs,)
        assert len(outs) == no + na, (name, len(outs))
        for r, o in zip(refs[nt + nf:nt + nf + no], outs[:no]):
            r[...] = o.astype(r.dtype)
        if na:
            first = pl.program_id(0) == 0
            for r, o in zip(refs[nt + nf + no:], outs[no:]):
                @pl.when(first)
                def _(r=r):
                    r[...] = jnp.zeros_like(r)
                r[...] += o.astype(F32)

    def whole(shape):
        nd = len(shape)
        return pl.BlockSpec(tuple(shape), lambda i, nd=nd: (0,) * nd)

    in_specs = [pl.BlockSpec((tm, t.shape[1]), lambda i: (i, 0)) for t in tiled] + [whole(f.shape) for f in full]
    out_specs = [pl.BlockSpec((tm, w), lambda i: (i, 0)) for w, _ in tiled_out] + [whole(s) for s in acc_out]
    out_shape = [jax.ShapeDtypeStruct((rows, w), d) for w, d in tiled_out] + [jax.ShapeDtypeStruct(tuple(s), F32) for s in acc_out]
    res = pl.pallas_call(
        body, name=name, grid=(rows // tm,), in_specs=in_specs, out_specs=out_specs, out_shape=out_shape,
        compiler_params=_params(("arbitrary",) if na else ("parallel",)),
    )(*tiled, *full)
    return res


def _sigmoid(x):
    return 1.0 / (1.0 + jnp.exp(-x))


def _rms_math(x, g):
    return x * lax.rsqrt(jnp.mean(x * x, axis=-1, keepdims=True) + NORM_EPS) * g


def _rms_bwd_math(x, g, dh):
    rstd = lax.rsqrt(jnp.mean(x * x, axis=-1, keepdims=True) + NORM_EPS)
    xhat = x * rstd
    dxhat = dh * g
    dx = rstd * (dxhat - xhat * jnp.mean(dxhat * xhat, axis=-1, keepdims=True))
    dg = jnp.sum(dh * xhat, axis=0, keepdims=True)
    return dx, dg


def _swiglu_fwd(gu, name):
    f = gu.shape[1] // 2

    def fn(gu):
        g, u = gu[:, :f], gu[:, f:]
        return g * _sigmoid(g) * u
    return _rowwise(fn, [gu], [], [(f, BF16)], [], tm=256, name=name)[0]


def _swiglu_bwd(gu, dact, name):
    f = gu.shape[1] // 2

    def fn(gu, dact):
        g, u = gu[:, :f], gu[:, f:]
        s = _sigmoid(g)
        dg = dact * u * (s * (1.0 + g * (1.0 - s)))
        du = dact * (g * s)
        return jnp.concatenate([dg, du], axis=1)
    return _rowwise(fn, [gu, dact], [], [(2 * f, BF16)], [], tm=256, name=name)[0]


def _ffn_fwd(x, g, w_gu, w_down, tag):
    gu = _matmul(x, w_gu, norm=g, name=f"{tag}_gu")
    act = _swiglu_fwd(gu, f"{tag}_act")
    x2 = _matmul(act, w_down, alpha=0.5, res=x, name=f"{tag}_down")
    return x2, (x, gu, act)


def _ffn_bwd(dx2, saved, g, w_gu, w_down, tag):
    x, gu, act = saved
    dact = _matmul(dx2, w_down, tb=True, alpha=0.5, name=f"{tag}_dact")
    d_down = _matmul(act, dx2, ta=True, alpha=0.5, name=f"{tag}_dwdown")
    dgu = _swiglu_bwd(gu, dact, f"{tag}_dgu")
    d_gu = _matmul(x, dgu, ta=True, norm=g, name=f"{tag}_dwgu")
    dx, dg = _matmul(dgu, w_gu, tb=True, norm_bwd=(x, g, dx2), name=f"{tag}_dh")
    return dx, dg, d_gu, d_down


def _ple_fwd(x, g, w_gate, p, w_proj, tag):
    z = _matmul(x, w_gate, norm=g, name=f"{tag}_gate")
    pp = _matmul(p, w_proj, name=f"{tag}_proj")
    d = x.shape[1]
    x2 = _rowwise(lambda x, z, pp: x + _sigmoid(z) * pp, [x, z, pp], [], [(d, F32)], [], tm=512, name=f"{tag}_comb")[0]
    return x2, (x, z, pp)


def _ple_bwd(dx2, saved, g, w_gate, p, tag):
    x, z, pp = saved
    d = x.shape[1]

    def fn(dx2, z, pp):
        s = _sigmoid(z)
        return dx2 * pp * s * (1.0 - s), dx2 * s
    dz, dpp = _rowwise(fn, [dx2, z, pp], [], [(d, BF16), (d, BF16)], [], tm=512, name=f"{tag}_dcomb")
    d_gate = _matmul(x, dz, ta=True, norm=g, name=f"{tag}_dwgate")
    d_proj = _matmul(p, dpp, ta=True, name=f"{tag}_dwproj")
    dx, dg = _matmul(dz, w_gate, tb=True, norm_bwd=(x, g, dx2), name=f"{tag}_dh")
    return dx, dg, d_gate, d_proj


def _final_loss(x, g, tgt):
    d = x.shape[1]

    def fn(x, tgt, g):
        rstd = lax.rsqrt(jnp.mean(x * x, axis=-1, keepdims=True) + NORM_EPS)
        err = x * rstd * g - tgt
        loss = 0.5 * jnp.sum(jnp.mean(err * err, axis=-1, keepdims=True), axis=0, keepdims=True)
        dx, dg = _rms_bwd_math(x, g, err * (1.0 / d))
        return dx, jnp.zeros((8, LANES), F32) + loss, dg
    return _rowwise(fn, [x, tgt], [g], [(d, F32)], [(8, LANES), (1, d)], tm=256, name="final_loss")


def _swa_masks(n):
    qi = lax.broadcasted_iota(jnp.int32, (SWA_BLOCK, 2 * SWA_BLOCK), 0)
    ki = lax.broadcasted_iota(jnp.int32, (SWA_BLOCK, 2 * SWA_BLOCK), 1)
    dist = qi + SWA_BLOCK - ki
    valid = (dist >= 0) & (dist < SWA_BLOCK) & ((ki >= SWA_BLOCK) | (n > 0))
    return dist.astype(F32), valid


def _dot_nt(a, b):
    return lax.dot_general(a, b, (((1,), (1,)), ((), ())), preferred_element_type=F32)


def _dot_tn(a, b):
    return lax.dot_general(a, b, (((0,), (0,)), ((), ())), preferred_element_type=F32)


def _dot(a, b):
    return jnp.dot(a, b, preferred_element_type=F32)


def _swa_specs(kvh, t):
    nb = t // SWA_BLOCK
    q_spec = pl.BlockSpec((SWA_GROUP, SWA_BLOCK, HEAD), lambda h, n: (h, n, 0))
    cur = pl.BlockSpec((1, SWA_BLOCK, HEAD), lambda h, n: (h, n, 0))
    prev = pl.BlockSpec((1, SWA_BLOCK, HEAD), lambda h, n: (h, jnp.maximum(n - 1, 0), 0))
    smem = pl.BlockSpec(memory_space=pltpu.SMEM)
    stat = pl.BlockSpec((SWA_GROUP, SWA_BLOCK, 1), lambda h, n: (h, n, 0))
    return nb, q_spec, cur, prev, smem, stat


def _swa_fwd(q, k, v, sinks, slopes):
    nh, t, _ = q.shape
    kvh = nh // SWA_GROUP
    nb, q_spec, cur, prev, smem, stat = _swa_specs(kvh, t)
    scale = HEAD ** -0.5

    def body(q_ref, kp_ref, kc_ref, vp_ref, vc_ref, sink_ref, slope_ref, o_ref, lse_ref):
        hk, n = pl.program_id(0), pl.program_id(1)
        dist, valid = _swa_masks(n)
        kk = jnp.concatenate([kp_ref[0], kc_ref[0]], axis=0)
        vv = jnp.concatenate([vp_ref[0], vc_ref[0]], axis=0)
        for g in range(SWA_GROUP):
            h = hk * SWA_GROUP + g
            s = _dot_nt(q_ref[g], kk) * scale - slope_ref[h] * dist
            s = jnp.where(valid, s, NEG)
            m = jnp.maximum(jnp.max(s, axis=-1, keepdims=True), sink_ref[h])
            p = jnp.exp(s - m)
            den = jnp.sum(p, axis=-1, keepdims=True) + jnp.exp(sink_ref[h] - m)
            o_ref[g] = _dot(p.astype(BF16), vv) / den
            lse_ref[g] = m + jnp.log(den)

    return pl.pallas_call(
        body, name="swa_fwd", grid=(kvh, nb),
        in_specs=[q_spec, prev, cur, prev, cur, smem, smem],
        out_specs=[q_spec, stat],
        out_shape=[jax.ShapeDtypeStruct((nh, t, HEAD), F32), jax.ShapeDtypeStruct((nh, t, 1), F32)],
        compiler_params=_params(("parallel", "parallel")),
    )(q, k, k, v, v, sinks, slopes)


def _swa_bwd(q, k, v, sinks, slopes, o, lse, do):
    nh, t, _ = q.shape
    kvh = nh // SWA_GROUP
    nb, q_spec, cur, prev, smem, stat = _swa_specs(kvh, t)
    scale = HEAD ** -0.5
    kv2 = pl.BlockSpec((1, 1, 2 * SWA_BLOCK, HEAD), lambda h, n: (h, n, 0, 0))
    sk = pl.BlockSpec((1, 1, 8, LANES), lambda h, n: (h, n, 0, 0))

    def body(q_ref, kp_ref, kc_ref, vp_ref, vc_ref, sink_ref, slope_ref, o_ref, lse_ref, do_ref,
             dq_ref, dk_ref, dv_ref, ds_ref):
        hk, n = pl.program_id(0), pl.program_id(1)
        dist, valid = _swa_masks(n)
        kk = jnp.concatenate([kp_ref[0], kc_ref[0]], axis=0)
        vv = jnp.concatenate([vp_ref[0], vc_ref[0]], axis=0)
        dk = jnp.zeros((2 * SWA_BLOCK, HEAD), F32)
        dv = jnp.zeros((2 * SWA_BLOCK, HEAD), F32)
        row = lax.broadcasted_iota(jnp.int32, (8, LANES), 0)
        dsink = jnp.zeros((8, LANES), F32)
        for g in range(SWA_GROUP):
            h = hk * SWA_GROUP + g
            qg = q_ref[g]
            s = _dot_nt(qg, kk) * scale - slope_ref[h] * dist
            p = jnp.where(valid, jnp.exp(s - lse_ref[g]), 0.0)
            dog = do_ref[g]
            delta = jnp.sum(dog * o_ref[g], axis=-1, keepdims=True)
            dob = dog.astype(BF16)
            dv = dv + _dot_tn(p.astype(BF16), dob)
            dp = _dot_nt(dob, vv)
            dsc = (p * (dp - delta) * scale).astype(BF16)
            dq_ref[g] = _dot(dsc, kk)
            dk = dk + _dot_tn(dsc, qg)
            dsk = -jnp.sum(jnp.exp(sink_ref[h] - lse_ref[g]) * delta, axis=0, keepdims=True)
            dsink = dsink + jnp.where(row == g, dsk, 0.0)
        dk_ref[0, 0] = dk
        dv_ref[0, 0] = dv
        ds_ref[0, 0] = dsink

    return pl.pallas_call(
        body, name="swa_bwd", grid=(kvh, nb),
        in_specs=[q_spec, prev, cur, prev, cur, smem, smem, q_spec, stat, q_spec],
        out_specs=[q_spec, kv2, kv2, sk],
        out_shape=[jax.ShapeDtypeStruct((nh, t, HEAD), F32),
                   jax.ShapeDtypeStruct((kvh, nb, 2 * SWA_BLOCK, HEAD), F32),
                   jax.ShapeDtypeStruct((kvh, nb, 2 * SWA_BLOCK, HEAD), F32),
                   jax.ShapeDtypeStruct((kvh, nb, 8, LANES), F32)],
        compiler_params=_params(("parallel", "parallel")),
    )(q, k, k, v, v, sinks, slopes, o, lse, do)


def _heads(a):
    t, w = a.shape
    return a.reshape(t, w // HEAD, HEAD).transpose(1, 0, 2)


def _unheads(a):
    h, t, _ = a.shape
    return a.transpose(1, 0, 2).reshape(t, h * HEAD)


def _fold_kv(d2):
    kvh, nb = d2.shape[:2]
    own = d2[:, :, SWA_BLOCK:]
    prev = d2[:, :, :SWA_BLOCK]
    nxt = jnp.concatenate([prev[:, 1:], jnp.zeros_like(prev[:, :1])], axis=1)
    return (own + nxt).reshape(kvh, nb * SWA_BLOCK, HEAD)


FOX_BLOCK = 512
GATE_BLOCK = 256


def _tri3(tri, x):
    hi = x.astype(BF16)
    r1 = x - hi.astype(F32)
    mid = r1.astype(BF16)
    lo = (r1 - mid.astype(F32)).astype(BF16)
    return _dot(tri, hi) + _dot(tri, mid) + _dot(tri, lo)


def _fox_gate_fwd(fz, b_f):
    t, nh = fz.shape
    blk = _tile(t, GATE_BLOCK, 16)
    nblk = t // blk

    def body(fz_ref, b_ref, c_ref):
        ri = lax.broadcasted_iota(jnp.int32, (blk, blk), 0)
        ci = lax.broadcasted_iota(jnp.int32, (blk, blk), 1)
        tri = (ci <= ri).astype(BF16)

        def step(j, carry):
            rows = pl.ds(j * blk, blk)
            z = fz_ref[rows, :] + b_ref[...]
            lf = jnp.minimum(z, 0.0) - jnp.log(1.0 + jnp.exp(-jnp.abs(z)))
            c_ref[rows, :] = carry + _tri3(tri, lf)
            return carry + jnp.sum(lf, axis=0, keepdims=True)
        lax.fori_loop(0, nblk, step, jnp.zeros((1, nh), F32))

    return pl.pallas_call(body, name="fox_gate_fwd", out_shape=jax.ShapeDtypeStruct((t, nh), F32),
                          compiler_params=_params())(fz, b_f)


def _fox_gate_bwd(fz, b_f, dc):
    t, nh = fz.shape
    blk = _tile(t, GATE_BLOCK, 16)
    nblk = t // blk

    def body(fz_ref, b_ref, dc_ref, dfz_ref, db_ref):
        ri = lax.broadcasted_iota(jnp.int32, (blk, blk), 0)
        ci = lax.broadcasted_iota(jnp.int32, (blk, blk), 1)
        tri = (ci >= ri).astype(BF16)

        def step(i, carry):
            acc, db = carry
            rows = pl.ds((nblk - 1 - i) * blk, blk)
            d = dc_ref[rows, :]
            dlf = acc + _tri3(tri, d)
            z = fz_ref[rows, :] + b_ref[...]
            dz = dlf * _sigmoid(-z)
            dfz_ref[rows, :] = dz
            return acc + jnp.sum(d, axis=0, keepdims=True), db + jnp.sum(dz, axis=0, keepdims=True)
        _, db = lax.fori_loop(0, nblk, step, (jnp.zeros((1, nh), F32), jnp.zeros((1, nh), F32)))
        db_ref[...] = db

    return pl.pallas_call(body, name="fox_gate_bwd",
                          out_shape=[jax.ShapeDtypeStruct((t, nh), F32), jax.ShapeDtypeStruct((1, nh), F32)],
                          compiler_params=_params())(fz, b_f, dc)


LOG2E = 1.4426950408889634
FOX_QSCALE = HEAD ** -0.5 * LOG2E


def _lower_triangle(blk):
    return lax.broadcasted_iota(jnp.int32, (blk, blk), 1) <= lax.broadcasted_iota(jnp.int32, (blk, blk), 0)


def _fox_fwd(q2, k, v, c_row2):
    nh, t, _ = q2.shape
    blk = c_row2.shape[-1]
    nb = t // blk

    def body(q_ref, k_ref, v_ref, ck_ref, o_ref, lse_ref):
        qi = pl.program_id(1)
        q = q_ref[0]

        def step(j, carry, diagonal):
            m, l, acc = carry
            ks = pl.ds(j * blk, blk)
            s = _dot_nt(q, k_ref[0, ks, :]) - ck_ref[0, j]
            if diagonal:
                s = jnp.where(_lower_triangle(blk), s, NEG)
            m2 = jnp.maximum(m, jnp.max(s, axis=-1, keepdims=True))
            a = jnp.exp2(m - m2)
            p = jnp.exp2(s - m2)
            l = a * l + jnp.sum(p, axis=-1, keepdims=True)
            acc = a * acc + _dot(p.astype(BF16), v_ref[0, ks, :])
            return m2, l, acc
        init = (jnp.full((blk, 1), NEG, F32), jnp.zeros((blk, 1), F32), jnp.zeros((blk, HEAD), F32))
        carry = lax.fori_loop(0, qi, lambda j, c: step(j, c, False), init)
        m, l, acc = step(qi, carry, True)
        o_ref[0] = acc / l
        lse_ref[0] = m + jnp.log(l) * LOG2E

    qb = pl.BlockSpec((1, blk, HEAD), lambda h, i: (h, i, 0))
    full = pl.BlockSpec((1, t, HEAD), lambda h, i: (h, 0, 0))
    colb = pl.BlockSpec((1, blk, 1), lambda h, i: (h, i, 0))
    rowf = pl.BlockSpec((1, nb, 1, blk), lambda h, i: (h, 0, 0, 0))
    return pl.pallas_call(
        body, name="fox_fwd", grid=(nh, nb), in_specs=[qb, full, full, rowf], out_specs=[qb, colb],
        out_shape=[jax.ShapeDtypeStruct((nh, t, HEAD), F32), jax.ShapeDtypeStruct((nh, t, 1), F32)],
        compiler_params=_params(("parallel", "parallel")),
    )(q2, k, v, c_row2)


def _fox_bwd(q2, k, v, c_row2, lse2, delta, do):
    nh, t, _ = q2.shape
    blk = c_row2.shape[-1]
    nb = t // blk
    scale = HEAD ** -0.5

    def body(q_ref, do_ref, lse_ref, dl_ref, k_ref, v_ref, ck_ref, dq_ref, dk_ref, dv_ref, dc_ref, dcq_ref):
        kb = pl.program_id(1)

        @pl.when(kb == 0)
        def _():
            dq_ref[...] = jnp.zeros_like(dq_ref)
            dcq_ref[...] = jnp.zeros_like(dcq_ref)

        k = k_ref[0]
        v = v_ref[0]
        ck = ck_ref[0, 0]

        def step(i, carry, diagonal):
            dk, dv, dck = carry
            rs = pl.ds(i * blk, blk)
            q = q_ref[0, rs, :]
            do = do_ref[0, rs, :]
            p = jnp.exp2(_dot_nt(q, k) - ck - lse_ref[0, rs, :])
            if diagonal:
                p = jnp.where(_lower_triangle(blk), p, 0.0)
            dv = dv + _dot_tn(p.astype(BF16), do)
            ds = p * (_dot_nt(do, v) - dl_ref[0, rs, :])
            dck = dck - jnp.sum(ds, axis=0, keepdims=True)
            dcq_ref[0, rs, :] += jnp.sum(ds, axis=1, keepdims=True)
            dsb = ds.astype(BF16)
            dk = dk + _dot_tn(dsb, q)
            dq_ref[0, rs, :] += _dot(dsb, k) * scale
            return dk, dv, dck
        init = (jnp.zeros((blk, HEAD), F32), jnp.zeros((blk, HEAD), F32), jnp.zeros((1, blk), F32))
        carry = step(kb, init, True)
        dk, dv, dck = lax.fori_loop(kb + 1, nb, lambda i, c: step(i, c, False), carry)
        dk_ref[0] = dk * (1.0 / LOG2E)
        dv_ref[0] = dv
        dc_ref[0, 0] = dck

    full = pl.BlockSpec((1, t, HEAD), lambda h, j: (h, 0, 0))
    colf = pl.BlockSpec((1, t, 1), lambda h, j: (h, 0, 0))
    kb_spec = pl.BlockSpec((1, blk, HEAD), lambda h, j: (h, j, 0))
    rowb = pl.BlockSpec((1, 1, 1, blk), lambda h, j: (h, j, 0, 0))
    return pl.pallas_call(
        body, name="fox_bwd", grid=(nh, nb),
        in_specs=[full, full, colf, colf, kb_spec, kb_spec, rowb],
        out_specs=[full, kb_spec, kb_spec, rowb, colf],
        out_shape=[jax.ShapeDtypeStruct((nh, t, HEAD), F32), jax.ShapeDtypeStruct((nh, t, HEAD), F32),
                   jax.ShapeDtypeStruct((nh, t, HEAD), F32), jax.ShapeDtypeStruct((nh, nb, 1, blk), F32),
                   jax.ShapeDtypeStruct((nh, t, 1), F32)],
        compiler_params=_params(("parallel", "arbitrary")),
    )(q2, do, lse2, delta, k, v, c_row2)


def _split3(x):
    hi = x.astype(BF16)
    r1 = x - hi.astype(F32)
    mid = r1.astype(BF16)
    lo = (r1 - mid.astype(F32)).astype(BF16)
    return hi, mid, lo


def _segsum_raw(a, bm, parts=3):
    outs = []
    for s in range(a.shape[-1] // SEG):
        x = a[:, s * SEG:(s + 1) * SEG]
        if parts == 3:
            hi, mid, lo = _split3(x)
            outs.append(_dot(hi, bm) + _dot(mid, bm) + _dot(lo, bm))
        elif parts == 1:
            outs.append(_dot(x.astype(BF16), bm))
        else:
            hi = x.astype(BF16)
            lo = (x - hi.astype(F32)).astype(BF16)
            outs.append(_dot(hi, bm) + _dot(lo, bm))
    return outs[0] if len(outs) == 1 else jnp.concatenate(outs, axis=-1)


@jax.custom_vjp
def _segsum(a, bm):
    return _segsum_raw(a, bm)


def _segsum_f(a, bm):
    return _segsum_raw(a, bm), bm


def _segsum_b(bm, ct):
    return _segsum_raw(ct, bm), jnp.zeros_like(bm)


_segsum.defvjp(_segsum_f, _segsum_b)


@jax.custom_vjp
def _bdot(a, w):
    return _dot(a.astype(BF16), w.astype(BF16))


def _bdot_f(a, w):
    return _bdot(a, w), (a, w)


def _bdot_b(saved, ct):
    a, w = saved
    ctb = ct.astype(BF16)
    return _dot_nt(ctb, w.astype(BF16)), _dot_tn(a.astype(BF16), ctb)


_bdot.defvjp(_bdot_f, _bdot_b)


def _softplus(z):
    return jnp.maximum(z, 0.0) + jnp.log(1.0 + jnp.exp(-jnp.abs(z)))


def _rwkv_pre_math(hb, hbp, mu, w0, a0, k_k, k_a, w2p, a2p, g2, bm):
    rd = w0.shape[-1]
    m = hb + (hbp - hb) * mu
    r, k, v = m[:, :rd], m[:, rd:2 * rd], m[:, 2 * rd:3 * rd]
    xwa = m[:, 3 * rd:3 * rd + LANES]
    xg = m[:, 3 * rd + LANES:]
    wlog = -_softplus(-(w0 + _bdot(jnp.tanh(xwa), w2p))) - 0.5
    decay = jnp.exp(-jnp.exp(wlog))
    a = _sigmoid(a0 + _bdot(xwa, a2p))
    g = _bdot(_sigmoid(xg), g2)
    kk0 = k * k_k
    kk = kk0 / jnp.maximum(jnp.sqrt(_segsum(kk0 * kk0, bm)), L2_EPS)
    kp = k * (1.0 + (a - 1.0) * k_a)
    return r, decay, kp, v, kk, kk * a, g


def _rwkv_post_math(y, r, kp, v, g, ln_w, ln_b, r_k, bm):
    mean = _segsum(y, bm) * (1.0 / HEAD)
    yc = y - mean
    var = _segsum(yc * yc, bm) * (1.0 / HEAD)
    yn = yc * lax.rsqrt(var + GN_EPS) * ln_w + ln_b
    bonus = _segsum(r * kp * r_k, bm) * v
    return (yn + bonus) * g


def _rwkv_pre(hb, hbp, prm, bm):
    rd = prm[1].shape[-1]
    outs = [(rd, F32)] * 7
    return _rowwise(_rwkv_pre_math, [hb, hbp], list(prm) + [bm], outs, [], tm=256, name="rwkv_pre")


def _rwkv_pre_bwd(hb, hbp, cts, prm, bm):
    n_in = hb.shape[1]

    def fn(hb, hbp, *rest):
        ct, full = rest[:7], rest[7:]
        prm_v, bm_v = full[:-1], full[-1]
        _, vjp = jax.vjp(lambda hb, hbp, *p: _rwkv_pre_math(hb, hbp, *p, bm_v), hb, hbp, *prm_v)
        g = vjp(tuple(ct))
        return g
    acc = [p.shape for p in prm]
    res = _rowwise(fn, [hb, hbp] + list(cts), list(prm) + [bm], [(n_in, F32)] * 2, acc, tm=256, name="rwkv_pre_bwd")
    return res[:2], res[2:]


def _rwkv_post(y, r, kp, v, g, prm, bm):
    rd = y.shape[1]
    return _rowwise(_rwkv_post_math, [y, r, kp, v, g], list(prm) + [bm], [(rd, F32)], [], tm=256, name="rwkv_post")[0]


def _rwkv_post_bwd(y, r, kp, v, g, dout, prm, bm):
    rd = y.shape[1]

    def fn(y, r, kp, v, g, dout, ln_w, ln_b, r_k, bm_v):
        _, vjp = jax.vjp(lambda *a: _rwkv_post_math(*a, bm_v), y, r, kp, v, g, ln_w, ln_b, r_k)
        return vjp(dout)
    acc = [p.shape for p in prm]
    res = _rowwise(fn, [y, r, kp, v, g, dout], list(prm) + [bm], [(rd, F32)] * 5, acc, tm=256, name="rwkv_post_bwd")
    return res[:5], res[5:]


SCAN_FWD_CHUNK = 32
SCAN_BWD_CHUNK = 16
SCAN_PARTS = 2
READOUT_PARTS = 1


def _spread_rows(x, mk, bm):
    c, _, rd = x.shape
    hi = x.astype(BF16)
    lo = (x - hi.astype(F32)).astype(BF16)
    keep = mk[None] != 0.0
    tile = lambda p: jnp.where(keep, p, jnp.zeros((), BF16)).reshape(c * HEAD, rd)
    parts = (tile(hi), tile(lo))
    outs = [sum(_dot(p[:, s * SEG:(s + 1) * SEG], bm) for p in parts) for s in range(rd // SEG)]
    out = outs[0] if len(outs) == 1 else jnp.concatenate(outs, axis=-1)
    return out.reshape(c, HEAD, rd)


def _seg3d(x, bm, parts):
    c, n, rd = x.shape
    return _segsum_raw(x.reshape(c * n, rd), bm, parts).reshape(c, n, rd)


def _head_dots(x, bm):
    n, _, rd = x.shape
    y = _segsum_raw(jnp.broadcast_to(x, (n, 8, rd)).reshape(n * 8, rd), bm, 3).reshape(n, 8, rd)
    return jnp.sum(y, axis=1, keepdims=True) * 0.125


def _rwkv_scan_fwd(w, kk, b, k, v, r, bm, mk, side=()):
    t, _, rd = w.shape
    c = _tile(t, SCAN_FWD_CHUNK, 8)
    assert c % 2 == 0
    ns = len(side)
    steps = t // c

    def body(*refs):
        w_ref, kk_ref, b_ref, k_ref, v_ref, r_ref, bm_ref, mk_ref = refs[:8]
        side_in = refs[8:8 + ns]
        y_ref, s_ref = refs[8 + ns:10 + ns]
        side_out = refs[10 + ns:10 + 2 * ns]
        state, vb, beta, gamma = refs[10 + 2 * ns:14 + 2 * ns]
        if ns:
            start, forward, finish = _gather_phases(side, side_in, side_out, refs[14 + 2 * ns:])
            pl.when(pl.program_id(0) == 0)(start)
            pl.when(pl.program_id(0) == steps // 2)(forward)
            pl.when(pl.program_id(0) == steps - 1)(finish)

        @pl.when(pl.program_id(0) == 0)
        def _():
            state[...] = jnp.zeros_like(state)

        bmv = bm_ref[...]
        mkv = mk_ref[...]
        vb[...] = _spread_rows(v_ref[...], mkv, bmv)
        kk_next = kk_ref[pl.ds(1, c - 1)]
        beta[pl.ds(0, c - 1)] = _head_dots(b_ref[pl.ds(0, c - 1)] * kk_next, bmv)
        gamma[pl.ds(0, c - 1)] = _head_dots(k_ref[pl.ds(0, c - 1)] * kk_next, bmv)

        def pair(p, s):
            ia, ib = 2 * p, 2 * p + 1
            sk_a = _segsum_raw(s * kk_ref[ia], bmv, SCAN_PARTS)
            through = _segsum_raw(s * (w_ref[ia] * kk_ref[ib]), bmv, SCAN_PARTS)
            va = vb[ia]
            s = s * w_ref[ia] - sk_a * b_ref[ia] + va * k_ref[ia]
            s_ref[ia] = s
            sk_b = through - sk_a * beta[ia] + va * gamma[ia]
            s = s * w_ref[ib] - sk_b * b_ref[ib] + vb[ib] * k_ref[ib]
            s_ref[ib] = s
            return s
        state[...] = lax.fori_loop(0, c // 2, pair, state[...])
        yb = _seg3d(s_ref[...] * r_ref[...], bmv, READOUT_PARTS)
        y_ref[...] = jnp.sum(yb * mkv[None], axis=1, keepdims=True)

    vec = pl.BlockSpec((c, 1, rd), lambda i: (i, 0, 0))
    res = pl.pallas_call(
        body, name="rwkv_scan_fwd", grid=(steps,),
        in_specs=[vec] * 6 + [pl.BlockSpec((SEG, SEG), lambda i: (0, 0)), pl.BlockSpec((HEAD, rd), lambda i: (0, 0))]
        + [HBM_SPEC] * ns,
        out_specs=[vec, pl.BlockSpec((c, HEAD, rd), lambda i: (i, 0, 0))] + [HBM_SPEC] * ns,
        out_shape=[jax.ShapeDtypeStruct((t, 1, rd), F32), jax.ShapeDtypeStruct((t, HEAD, rd), F32)] + _gather_shapes(side),
        scratch_shapes=[pltpu.VMEM((HEAD, rd), F32), pltpu.VMEM((c, HEAD, rd), F32),
                        pltpu.VMEM((c, 1, rd), F32), pltpu.VMEM((c, 1, rd), F32)] + (_gather_sems(ns) if ns else []),
        compiler_params=_params(("arbitrary",)),
    )(w, kk, b, k, v, r, bm, mk, *side)
    return res[0], res[1], list(res[2:])


def _rwkv_scan_bwd(w, kk, b, k, v, r, dy, states, bm, mk, side=()):
    t, _, rd = w.shape
    c = _tile(t, SCAN_BWD_CHUNK, 8)
    nc = t // c
    assert c % 2 == 0
    ns = len(side)

    def body(*refs):
        w_ref, kk_ref, b_ref, k_ref, v_ref, r_ref, dy_ref, s_ref, sp_ref, bm_ref, mk_ref = refs[:11]
        dr_ref, dw_ref, dk_ref, dv_ref, dkk_ref, db_ref = refs[11 + ns:17 + ns]
        gstate, sp, vb, dyb, skb, gall, gball, delta, eps = refs[17 + 2 * ns:26 + 2 * ns]
        step_id = pl.program_id(0)
        if ns:
            start, finish = _scatter_phases(ns, refs[11:11 + ns], refs[17 + ns:17 + 2 * ns], refs[26 + 2 * ns:])
            pl.when(step_id == 0)(start)
            pl.when(step_id == nc - 1)(finish)

        @pl.when(step_id == 0)
        def _():
            gstate[...] = jnp.zeros_like(gstate)

        bmv = bm_ref[...]
        mkv = mk_ref[...]
        sp[0] = jnp.where(step_id == nc - 1, 0.0, sp_ref[0])
        sp[1:c] = s_ref[0:c - 1]
        vb[...] = _spread_rows(v_ref[...], mkv, bmv)
        dyb[...] = _spread_rows(dy_ref[...], mkv, bmv)
        skb[...] = _seg3d(sp[...] * kk_ref[...], bmv, READOUT_PARTS)
        dr_ref[...] = jnp.sum(s_ref[...] * dyb[...], axis=1, keepdims=True)
        delta[pl.ds(0, c - 1)] = _head_dots(kk_ref[pl.ds(1, c - 1)] * b_ref[pl.ds(0, c - 1)], bmv)
        eps[...] = _head_dots(r_ref[...] * b_ref[...], bmv)

        def pair(p, g):
            ib = c - 1 - 2 * p
            ia = ib - 1
            g = g + dyb[ib] * r_ref[ib]
            gall[ib] = g
            gb_b = _segsum_raw(g * b_ref[ib], bmv, SCAN_PARTS)
            through = _segsum_raw(g * (w_ref[ib] * b_ref[ia]), bmv, SCAN_PARTS)
            gball[ib] = gb_b
            dya = dyb[ia]
            g = g * w_ref[ib] - gb_b * kk_ref[ib] + dya * r_ref[ia]
            gall[ia] = g
            gb_a = through - gb_b * delta[ia] + dya * eps[ia]
            gball[ia] = gb_a
            return g * w_ref[ia] - gb_a * kk_ref[ia]
        gstate[...] = lax.fori_loop(0, c // 2, pair, gstate[...])
        ga = gall[...]
        dv_ref[...] = jnp.sum(_seg3d(ga * k_ref[...], bmv, READOUT_PARTS) * mkv[None], axis=1, keepdims=True)
        dk_ref[...] = jnp.sum(ga * vb[...], axis=1, keepdims=True)
        dw_ref[...] = jnp.sum(ga * sp[...], axis=1, keepdims=True)
        db_ref[...] = -jnp.sum(ga * skb[...], axis=1, keepdims=True)
        dkk_ref[...] = -jnp.sum(sp[...] * gball[...], axis=1, keepdims=True)

    vec = pl.BlockSpec((c, 1, rd), lambda i: (nc - 1 - i, 0, 0))
    st = pl.BlockSpec((c, HEAD, rd), lambda i: (nc - 1 - i, 0, 0))
    st_prev = pl.BlockSpec((1, HEAD, rd), lambda i: (jnp.maximum((nc - 1 - i) * c - 1, 0), 0, 0))
    big = pltpu.VMEM((c, HEAD, rd), F32)
    res = pl.pallas_call(
        body, name="rwkv_scan_bwd", grid=(nc,),
        in_specs=[vec] * 7 + [st, st_prev, pl.BlockSpec((SEG, SEG), lambda i: (0, 0)),
                              pl.BlockSpec((HEAD, rd), lambda i: (0, 0))] + [HBM_SPEC] * ns,
        out_specs=[vec] * 6 + [HBM_SPEC] * ns,
        out_shape=[jax.ShapeDtypeStruct((t, 1, rd), F32)] * 6 + _scatter_shapes(side),
        scratch_shapes=[pltpu.VMEM((HEAD, rd), F32), big, big, big, big, big, big,
                        pltpu.VMEM((c, 1, rd), F32), pltpu.VMEM((c, 1, rd), F32)] + (_scatter_sems(ns) if ns else []),
        compiler_params=_params(("arbitrary",)),
    )(w, kk, b, k, v, r, dy, states, states, bm, mk, *side)
    return res[:6], list(res[6:])


def _shift_down(a):
    return jnp.concatenate([jnp.zeros_like(a[:1]), a[:-1]], axis=0)


def _rwkv_consts(rd):
    i = jnp.arange(SEG) // HEAD
    bm = (i[:, None] == i[None, :]).astype(BF16)
    mk = (jnp.arange(HEAD)[:, None] == (jnp.arange(rd) % HEAD)[None, :]).astype(F32)
    return bm, mk


def _lora_pad(w2, a2):
    z = jnp.zeros_like(w2)
    return jnp.concatenate([w2, z], axis=0), jnp.concatenate([jnp.zeros_like(a2), a2], axis=0)


def _rwkv_fwd(hb, prm, side=()):
    rd = prm['w0'].shape[-1]
    t = hb.shape[0]
    bm, mk = _rwkv_consts(rd)
    hbp = _shift_down(hb)
    pre_prm = (prm['mu'], prm['w0'], prm['a0'], prm['k_k'], prm['k_a'], prm['w2p'], prm['a2p'], prm['g2'])
    r, w, kp, v, kk, b, g = _rwkv_pre(hb, hbp, pre_prm, bm)
    to3 = lambda a: a.reshape(t, 1, rd)
    y3, states, gathered = _rwkv_scan_fwd(to3(w), to3(kk), to3(b), to3(kp), to3(v), to3(r), bm, mk, side)
    y = y3.reshape(t, rd)
    post_prm = (prm['ln_w'], prm['ln_b'], prm['r_k'])
    out = _rwkv_post(y, r, kp, v, g, post_prm, bm)
    return out, (hb, hbp, r, w, kp, v, kk, b, g, y, states), gathered


def _rwkv_bwd(dout, saved, prm, side=()):
    hb, hbp, r, w, kp, v, kk, b, g, y, states = saved
    rd = prm['w0'].shape[-1]
    t = hb.shape[0]
    bm, mk = _rwkv_consts(rd)
    post_prm = (prm['ln_w'], prm['ln_b'], prm['r_k'])
    (dy, dr1, dkp1, dv1, dg), (d_ln_w, d_ln_b, d_r_k) = _rwkv_post_bwd(y, r, kp, v, g, dout, post_prm, bm)
    to3 = lambda a: a.reshape(t, 1, rd)
    (dr2, dw, dk2, dv2, dkk, db), received = _rwkv_scan_bwd(to3(w), to3(kk), to3(b), to3(kp), to3(v), to3(r), to3(dy), states,
                                                            bm, mk, side)
    to2 = lambda a: a.reshape(t, rd)
    cts = [dr1 + to2(dr2), to2(dw), dkp1 + to2(dk2), dv1 + to2(dv2), to2(dkk), to2(db), dg]
    pre_prm = (prm['mu'], prm['w0'], prm['a0'], prm['k_k'], prm['k_a'], prm['w2p'], prm['a2p'], prm['g2'])
    (dhb, dhbp), gp = _rwkv_pre_bwd(hb, hbp, cts, pre_prm, bm)
    dhb = dhb + jnp.concatenate([dhbp[1:], jnp.zeros_like(dhbp[:1])], axis=0)
    d_mu, d_w0, d_a0, d_k_k, d_k_a, d_w2p, d_a2p, d_g2 = gp
    grads = {'rwkv_mu': d_mu, 'rwkv_w0': d_w0, 'rwkv_a0': d_a0, 'rwkv_k_k': d_k_k, 'rwkv_k_a': d_k_a,
             'rwkv_w2': d_w2p[:DECAY_LORA], 'rwkv_a2': d_a2p[DECAY_LORA:], 'rwkv_g2': d_g2,
             'rwkv_ln_w': d_ln_w, 'rwkv_ln_b': d_ln_b, 'rwkv_r_k': d_r_k}
    return dhb, grads, received


def _even_fwd(x, g, w_in, w_out, sinks, slopes, rprm, side=()):
    d = x.shape[1]
    q_w, kv_w = d // 2, d // 8
    proj = _matmul(x, w_in, norm=g, name="even_in")
    qa, ka, va, hb = proj[:, :q_w], proj[:, q_w:q_w + kv_w], proj[:, q_w + kv_w:q_w + 2 * kv_w], proj[:, q_w + 2 * kv_w:]
    qh, kh, vh = _heads(qa).astype(BF16), _heads(ka).astype(BF16), _heads(va).astype(BF16)
    oa, lse = _swa_fwd(qh, kh, vh, sinks, slopes)
    yb, rsaved, gathered = _rwkv_fwd(hb, rprm, side)
    cat = jnp.concatenate([_unheads(oa), yb], axis=1)
    x2 = _matmul(cat, w_out, res=x, name="even_out")
    return x2, (x, qh, kh, vh, oa, lse, cat, rsaved), gathered


def _even_bwd(dx2, saved, g, w_in, w_out, sinks, slopes, rprm, side=()):
    x, qh, kh, vh, oa, lse, cat, rsaved = saved
    d = x.shape[1]
    dcat = _matmul(dx2, w_out, tb=True, name="even_dcat")
    d_out = _matmul(cat, dx2, ta=True, name="even_dwout")
    dya, dyb = dcat[:, :d // 2], dcat[:, d // 2:]
    dq, dk2, dv2, dsk = _swa_bwd(qh, kh, vh, sinks, slopes, oa, lse, _heads(dya))
    d_sinks = jnp.sum(dsk[:, :, :SWA_GROUP, 0], axis=1).reshape(1, -1)
    dhb, rgrads, received = _rwkv_bwd(dyb, rsaved, rprm, side)
    dproj = jnp.concatenate([_unheads(dq), _unheads(_fold_kv(dk2)), _unheads(_fold_kv(dv2)), dhb], axis=1)
    d_in = _matmul(x, dproj, ta=True, norm=g, name="even_dwin")
    dx, dg = _matmul(dproj, w_in, tb=True, norm_bwd=(x, g, dx2), name="even_dhn")
    return dx, dg, d_in, d_out, d_sinks, rgrads, received


def _odd_fwd(x, g, w_in, w_out, b_f):
    d = x.shape[1]
    t = x.shape[0]
    nh = d // HEAD
    qkv = _matmul(x, w_in[:, :3 * d], norm=g, name="odd_in")
    fz = _matmul(x, w_in[:, 3 * d:], norm=g, name="odd_fz")
    c = _fox_gate_fwd(fz, b_f)
    blk = _tile(t, FOX_BLOCK, 128)
    c_row = (c.T * LOG2E).reshape(nh, t // blk, 1, blk)
    qh = _heads(qkv[:, :d] * FOX_QSCALE).astype(BF16)
    kh, vh = (_heads(qkv[:, i * d:(i + 1) * d]).astype(BF16) for i in (1, 2))
    o, lse = _fox_fwd(qh, kh, vh, c_row)
    y = _unheads(o)
    x2 = _matmul(y, w_out, res=x, name="odd_out")
    return x2, (x, fz, qh, kh, vh, c_row, o, lse, y)


def _odd_bwd(dx2, saved, g, w_in, w_out, b_f):
    x, fz, qh, kh, vh, c_row, o, lse, y = saved
    d = x.shape[1]
    t = x.shape[0]
    nh = d // HEAD
    dy = _matmul(dx2, w_out, tb=True, name="odd_dy")
    d_out = _matmul(y, dx2, ta=True, name="odd_dwout")
    doh = _heads(dy)
    delta = _rowwise(lambda a, b: jnp.sum(a * b, axis=-1, keepdims=True),
                     [o.reshape(nh * t, HEAD), doh.reshape(nh * t, HEAD)], [], [(1, F32)], [], tm=1024,
                     name="fox_delta")[0].reshape(nh, t, 1)
    dq, dk, dv, dcr, dcc = _fox_bwd(qh, kh, vh, c_row, lse, delta, doh.astype(BF16))
    dfz, d_bf = _fox_gate_bwd(fz, b_f, (dcr.reshape(nh, t) + dcc.reshape(nh, t)).T)
    dqkv = jnp.concatenate([_unheads(dq), _unheads(dk), _unheads(dv)], axis=1)
    d_in = jnp.concatenate([_matmul(x, dqkv, ta=True, norm=g, name="odd_dwin"),
                            _matmul(x, dfz, ta=True, norm=g, name="odd_dwin_fz")], axis=1)
    dhn_fz = _matmul(dfz, w_in[:, 3 * d:], tb=True, name="odd_dhn_fz")
    dx, dg = _matmul(dqkv, w_in[:, :3 * d], tb=True, res=dhn_fz, norm_bwd=(x, g, dx2), name="odd_dhn")
    return dx, dg, d_in, d_out, d_bf


def _place():
    return lax.axis_index("x"), lax.axis_index("y"), lax.axis_index("c")


def _other_chips(x, y):
    return [(1 - x, y), (x, 1 - y), (1 - x, 1 - y)]


HBM_SPEC = pl.BlockSpec(memory_space=pltpu.HBM)


def _rows(ref, which, h):
    return ref.at[pl.ds(which * h, h)]


def _gather_phases(shards, ins, outs, sems):
    ici_send, ici_recv, d2d_send, d2d_recv = sems
    x, y, c = _place()
    me = 2 * x + y
    sibling = (x, y, 1 - c)
    pairs = [(i, j, px, py) for i in range(len(shards)) for j, (px, py) in enumerate(_other_chips(x, y))]
    half = lambda i, ref, which: _rows(ref, which, shards[i].shape[0] // 2)

    def over_ici(i, j, px, py, slot):
        return pltpu.make_async_remote_copy(
            src_ref=half(i, ins[i], c), dst_ref=half(i, outs[i].at[slot], c), send_sem=ici_send.at[3 * i + j],
            recv_sem=ici_recv.at[3 * i + j], device_id=(px, py, c), device_id_type=MESH)

    def over_d2d(i, j, px, py, which):
        part = half(i, outs[i].at[2 * px + py], which)
        return pltpu.make_async_remote_copy(src_ref=part, dst_ref=part, send_sem=d2d_send.at[3 * i + j],
                                            recv_sem=d2d_recv.at[3 * i + j], device_id=sibling, device_id_type=MESH)

    def start():
        for i, j, px, py in pairs:
            over_ici(i, j, px, py, me).start()

    def forward():
        for i, j, px, py in pairs:
            over_ici(i, j, px, py, 2 * px + py).wait_recv()
            over_d2d(i, j, px, py, c).start()

    def finish():
        for i, j, px, py in pairs:
            over_d2d(i, j, px, py, 1 - c).wait_recv()
        for i, j, px, py in pairs:
            over_ici(i, j, px, py, me).wait_send()
            over_d2d(i, j, px, py, c).wait_send()

    return start, forward, finish


def _gather_sems(n):
    return [pltpu.SemaphoreType.DMA((3 * n,))] * 4


def _gather_shapes(shards):
    return [jax.ShapeDtypeStruct((N_CHIPS,) + s.shape, s.dtype) for s in shards]


def _gather_chips(shards):
    n = len(shards)

    def body(*refs):
        start, forward, finish = _gather_phases(shards, refs[:n], refs[n:2 * n], refs[2 * n:])
        start()
        forward()
        finish()

    return pl.pallas_call(
        body, name="gather_weights", in_specs=[HBM_SPEC] * n, out_specs=[HBM_SPEC] * n,
        out_shape=_gather_shapes(shards), scratch_shapes=_gather_sems(n),
    )(*shards)


def _swap_halves(stacked):
    n = len(stacked)
    halves = [s.shape[1] // 2 for s in stacked]

    def body(*refs):
        ins, outs = refs[:n], refs[n:2 * n]
        send_sems, recv_sems = refs[2 * n:]
        x, y, c = _place()
        sends = []
        for i in range(n):
            cp = pltpu.make_async_remote_copy(
                src_ref=ins[i].at[:, pl.ds((1 - c) * halves[i], halves[i])], dst_ref=outs[i], send_sem=send_sems.at[i],
                recv_sem=recv_sems.at[i], device_id=(x, y, 1 - c), device_id_type=MESH)
            cp.start()
            sends.append(cp)
        for cp in sends:
            cp.wait_recv()
        for cp in sends:
            cp.wait_send()

    return pl.pallas_call(
        body, name="swap_halves", in_specs=[HBM_SPEC] * n, out_specs=[HBM_SPEC] * n,
        out_shape=[jax.ShapeDtypeStruct((N_CHIPS, h) + s.shape[2:], s.dtype) for s, h in zip(stacked, halves)],
        scratch_shapes=[pltpu.SemaphoreType.DMA((n,)), pltpu.SemaphoreType.DMA((n,))],
    )(*stacked)


def _scatter_phases(n, ins, outs, sems):
    send_sems, recv_sems = sems
    x, y, c = _place()
    pairs = [(i, j, px, py) for i in range(n) for j, (px, py) in enumerate(_other_chips(x, y))]

    def copy(i, j, px, py):
        return pltpu.make_async_remote_copy(src_ref=ins[i].at[2 * px + py], dst_ref=outs[i].at[j], send_sem=send_sems.at[3 * i + j],
                                            recv_sem=recv_sems.at[3 * i + j], device_id=(px, py, c), device_id_type=MESH)

    def start():
        for p in pairs:
            copy(*p).start()

    def finish():
        for p in pairs:
            copy(*p).wait_recv()
        for p in pairs:
            copy(*p).wait_send()

    return start, finish


def _scatter_sems(n):
    return [pltpu.SemaphoreType.DMA((3 * n,))] * 2


def _scatter_shapes(stacked):
    return [jax.ShapeDtypeStruct((3,) + s.shape[1:], s.dtype) for s in stacked]


def _scatter_chips(stacked):
    n = len(stacked)

    def body(*refs):
        start, finish = _scatter_phases(n, refs[:n], refs[n:2 * n], refs[2 * n:])
        start()
        finish()

    return pl.pallas_call(
        body, name="scatter_grads", in_specs=[HBM_SPEC] * n, out_specs=[HBM_SPEC] * n,
        out_shape=_scatter_shapes(stacked), scratch_shapes=_scatter_sems(n),
    )(*stacked)


def _swap_cores(arrs):
    n = len(arrs)

    def body(*refs):
        ins, outs = refs[:n], refs[n:2 * n]
        send_sems, recv_sems = refs[2 * n:]
        x, y, c = _place()
        sends = []
        for i in range(n):
            cp = pltpu.make_async_remote_copy(src_ref=ins[i], dst_ref=outs[i], send_sem=send_sems.at[i], recv_sem=recv_sems.at[i],
                                              device_id=(x, y, 1 - c), device_id_type=MESH)
            cp.start()
            sends.append(cp)
        for cp in sends:
            cp.wait_recv()
        for cp in sends:
            cp.wait_send()

    sem = pltpu.SemaphoreType.DMA((n,))
    return pl.pallas_call(
        body, name="swap_cores", in_specs=[HBM_SPEC] * n, out_specs=[HBM_SPEC] * n,
        out_shape=[jax.ShapeDtypeStruct(s.shape, s.dtype) for s in arrs],
        scratch_shapes=[sem, sem],
    )(*arrs)


def _allreduce_small(buf):
    rows = buf.shape[0]

    def body(in_ref, out_ref, gat, send_sems, recv_sems):
        x, y, c = _place()
        me = 4 * x + 2 * y + c
        gat[me] = in_ref[...]
        sends = []
        for k in range(1, N_DEV):
            bx, by, bc = (k >> 2) & 1, (k >> 1) & 1, k & 1
            peer = (x ^ bx, y ^ by, c ^ bc)
            cp = pltpu.make_async_remote_copy(src_ref=in_ref, dst_ref=gat.at[me], send_sem=send_sems.at[k - 1],
                                              recv_sem=recv_sems.at[k - 1], device_id=peer, device_id_type=MESH)
            cp.start()
            sends.append((cp, 4 * peer[0] + 2 * peer[1] + peer[2]))
        for k, (cp, slot) in enumerate(sends):
            pltpu.make_async_remote_copy(src_ref=in_ref, dst_ref=gat.at[slot], send_sem=send_sems.at[k], recv_sem=recv_sems.at[k],
                                         device_id=(x, y, c), device_id_type=MESH).wait_recv()
        for cp, _ in sends:
            cp.wait_send()
        acc = gat[0]
        for k in range(1, N_DEV):
            acc = acc + gat[k]
        out_ref[...] = acc

    vm = pl.BlockSpec(memory_space=pltpu.VMEM)
    return pl.pallas_call(
        body, name="allreduce_small", in_specs=[vm], out_specs=vm, out_shape=jax.ShapeDtypeStruct(buf.shape, F32),
        scratch_shapes=[pltpu.VMEM((N_DEV, rows, LANES), F32), pltpu.SemaphoreType.DMA((N_DEV - 1,)), pltpu.SemaphoreType.DMA((N_DEV - 1,))],
        compiler_params=_params(),
    )(buf)


def _as2d(a):
    return a.reshape(-1, a.shape[-1])


def _cast_bf16(a, name):
    a2 = _as2d(a)
    out = _rowwise(lambda v: v, [a2], [], [(a2.shape[1], BF16)], [], tm=512, name=name)[0]
    return out.reshape(a.shape)


def _assemble(gathered, axis):
    _, l, r, c = gathered.shape
    if axis == 1:
        return gathered.transpose(1, 0, 2, 3).reshape(l, N_CHIPS * r, c)
    return gathered.transpose(1, 2, 0, 3).reshape(l, r, N_CHIPS * c)


def _split_shards(full, axis):
    l, r, c = full.shape
    if axis == 1:
        return full.reshape(l, N_CHIPS, r // N_CHIPS, c).transpose(1, 0, 2, 3)
    return full.reshape(l, r, N_CHIPS, c // N_CHIPS).transpose(2, 0, 1, 3)


def _adamw_math(w, g, m, v):
    m2 = ADAM_B1 * m + (1.0 - ADAM_B1) * g
    v2 = ADAM_B2 * v + (1.0 - ADAM_B2) * (g * g)
    m_hat = m2 / (1.0 - ADAM_B1 ** ADAM_STEP)
    v_hat = v2 / (1.0 - ADAM_B2 ** ADAM_STEP)
    delta = -ADAM_LR * (m_hat / (jnp.sqrt(v_hat) + ADAM_EPS) + ADAM_WD * w)
    return delta, m2, v2


def _adamw_big(w, m, v, mine, other, core, name):
    shape = w.shape
    wd = shape[-1]
    h = mine.shape[0]
    tm = _tile(h, 256, 16)
    nh = h // tm

    def body(core_ref, w_ref, m_ref, v_ref, a_ref, b_ref, g_ref, d_ref, mo_ref, vo_ref):
        g = jnp.where(pl.program_id(0) // nh == core_ref[0], a_ref[...], b_ref[...])
        g_ref[...] = g
        d_ref[...], mo_ref[...], vo_ref[...] = _adamw_math(w_ref[...], g, m_ref[...], v_ref[...])

    rows = pl.BlockSpec((tm, wd), lambda i: (i, 0))
    half = pl.BlockSpec((tm, wd), lambda i: (i % nh, 0))
    outs = pl.pallas_call(
        body, name=name, grid=(2 * nh,),
        in_specs=[pl.BlockSpec(memory_space=pltpu.SMEM), rows, rows, rows, half, half], out_specs=[rows] * 4,
        out_shape=[jax.ShapeDtypeStruct((2 * h, wd), F32)] * 4,
        compiler_params=_params(("parallel",)),
    )(core, _as2d(w), _as2d(m), _as2d(v), mine, other)
    return [o.reshape(shape) for o in outs]


def _pair_add(s, o, which, name):
    _, h, c = o.shape
    tm = _tile(h, 256, 16)
    nb = h // tm

    def body(s_ref, o_ref, out_ref):
        out_ref[...] = (s_ref[...] + o_ref[...]).astype(BF16)

    mine = pl.BlockSpec((None, tm, c), lambda q, i: (q, which * nb + i, 0))
    theirs = pl.BlockSpec((None, tm, c), lambda q, i: (q, i, 0))
    return pl.pallas_call(
        body, name=name, grid=(N_CHIPS, nb), in_specs=[mine, theirs], out_specs=theirs,
        out_shape=jax.ShapeDtypeStruct(o.shape, BF16), compiler_params=_params(("parallel", "parallel")),
    )(s, o)


def _pair(s, o, core, name):
    return lax.cond(core == 0, lambda: _pair_add(s, o, 0, name + "_south"), lambda: _pair_add(s, o, 1, name + "_north"))


def _sum4(mine, recv, name):
    wd = mine.shape[-1]
    up = lambda v: v.astype(F32)
    return _rowwise(lambda a, b, c, d: ((up(a) + up(b)) + up(c)) + up(d), [mine, recv[0], recv[1], recv[2]], [], [(wd, F32)], [],
                    tm=256, name=name)[0]


def _pack(arrs):
    parts = []
    for a in arrs:
        f = a.reshape(-1).astype(F32)
        parts.append(jnp.pad(f, (0, (-f.shape[0]) % LANES)))
    flat = jnp.concatenate(parts)
    flat = jnp.pad(flat, (0, (-flat.shape[0]) % (8 * LANES)))
    return flat.reshape(-1, LANES)


def _unpack(buf, like):
    flat = buf.reshape(-1)
    out, off = [], 0
    for a in like:
        n = math.prod(a.shape)
        out.append(flat[off:off + n].reshape(a.shape))
        off += n + (-n) % LANES
    return out


def kernel(x, p, ffn1_norm, ffn1_w_gu, ffn1_w_down, mix_norm, ffn2_norm, ffn2_w_gu, ffn2_w_down, ple_norm, ple_w_gate, ple_w_proj, even_w_in, even_w_out, swa_sinks, rwkv_mu, rwkv_w0, rwkv_w2, rwkv_a0, rwkv_a2, rwkv_g2, rwkv_k_k, rwkv_k_a, rwkv_r_k, rwkv_ln_w, rwkv_ln_b, fox_w_in, fox_b_f, fox_w_out, final_norm, loss_target, m_ffn1_norm, m_ffn1_w_gu, m_ffn1_w_down, m_mix_norm, m_ffn2_norm, m_ffn2_w_gu, m_ffn2_w_down, m_ple_norm, m_ple_w_gate, m_ple_w_proj, m_even_w_in, m_even_w_out, m_swa_sinks, m_rwkv_mu, m_rwkv_w0, m_rwkv_w2, m_rwkv_a0, m_rwkv_a2, m_rwkv_g2, m_rwkv_k_k, m_rwkv_k_a, m_rwkv_r_k, m_rwkv_ln_w, m_rwkv_ln_b, m_fox_w_in, m_fox_b_f, m_fox_w_out, m_final_norm, v_ffn1_norm, v_ffn1_w_gu, v_ffn1_w_down, v_mix_norm, v_ffn2_norm, v_ffn2_w_gu, v_ffn2_w_down, v_ple_norm, v_ple_w_gate, v_ple_w_proj, v_even_w_in, v_even_w_out, v_swa_sinks, v_rwkv_mu, v_rwkv_w0, v_rwkv_w2, v_rwkv_a0, v_rwkv_a2, v_rwkv_g2, v_rwkv_k_k, v_rwkv_k_a, v_rwkv_r_k, v_rwkv_ln_w, v_rwkv_ln_b, v_fox_w_in, v_fox_b_f, v_fox_w_out, v_final_norm):
    args = locals()
    wts = {n: args[n] for n in WEIGHTS}
    mom = {n: args['m_' + n] for n in WEIGHTS}
    var = {n: args['v_' + n] for n in WEIGHTS}
    xs = x[0]
    tgt = loss_target[0]
    t, d = xs.shape
    depth = ffn1_norm.shape[0]
    rd = d // 2
    row = lambda a: a.reshape(1, -1)

    names = BIG + LORA
    chip = 2 * lax.axis_index("x") + lax.axis_index("y")
    cast = {n: _cast_bf16(wts[n], f"cast_{n}") for n in names}
    items = [(n, i) for n in names for i in range(wts[n].shape[0])]
    early = lambda n, i: (n in ('ffn1_w_gu', 'ffn1_w_down') and i == 0) or n in ('even_w_in', 'even_w_out') or n in LORA
    first = [it for it in items if early(*it)]
    later = [it for it in items if not early(*it)]
    full = {n: [None] * wts[n].shape[0] for n in names}

    def place(group, gathered):
        for (n, i), g in zip(group, gathered):
            g = lax.dynamic_update_index_in_dim(g, cast[n][i], chip, 0)
            full[n][i] = _assemble(g[:, None], SHARDED[n])[0]

    place(first, _gather_chips([cast[n][i] for n, i in first]))

    n_swa = d // (2 * HEAD)
    slopes = 2.0 ** (-8.0 * jnp.arange(1, n_swa + 1, dtype=F32) / n_swa)
    w2p, a2p = _lora_pad(full['rwkv_w2'][0].astype(F32), full['rwkv_a2'][0].astype(F32))
    rprm = {'mu': rwkv_mu, 'w0': rwkv_w0, 'a0': rwkv_a0, 'k_k': rwkv_k_k, 'k_a': rwkv_k_a, 'w2p': w2p, 'a2p': a2p,
            'g2': full['rwkv_g2'][0].astype(F32), 'ln_w': rwkv_ln_w, 'ln_b': rwkv_ln_b, 'r_k': rwkv_r_k.reshape(1, rd)}

    saved = []
    h = xs
    for i in range(depth):
        h, s1 = _ffn_fwd(h, row(ffn1_norm[i]), full['ffn1_w_gu'][i], full['ffn1_w_down'][i], f"l{i}_ffn1")
        if i % 2 == 0:
            h, sm, gathered = _even_fwd(h, row(mix_norm[i]), full['even_w_in'][i // 2], full['even_w_out'][i // 2],
                                        swa_sinks[i // 2], slopes, rprm, [cast[n][k] for n, k in later])
            place(later, gathered)
        else:
            h, sm = _odd_fwd(h, row(mix_norm[i]), full['fox_w_in'][i // 2], full['fox_w_out'][i // 2], row(fox_b_f[i // 2]))
        h, s2 = _ffn_fwd(h, row(ffn2_norm[i]), full['ffn2_w_gu'][i], full['ffn2_w_down'][i], f"l{i}_ffn2")
        h, sp = _ple_fwd(h, row(ple_norm[i]), full['ple_w_gate'][i], p[i, 0], full['ple_w_proj'][i], f"l{i}_ple")
        saved.append((s1, sm, s2, sp))
    dx, loss_tile, d_final = _final_loss(h, row(final_norm), tgt)

    core = lax.axis_index("c")
    gl = {n: [None] * depth for n in ['ffn1_norm', 'ffn1_w_gu', 'ffn1_w_down', 'mix_norm', 'ffn2_norm', 'ffn2_w_gu',
                                      'ffn2_w_down', 'ple_norm', 'ple_w_gate', 'ple_w_proj']}
    g1 = {}
    local = {}

    def settle(n):
        if n in gl:
            per_layer = gl[n]
            local[n] = jnp.stack(per_layer).reshape((depth,) + per_layer[0].shape[-2:]) if per_layer[0].shape[0] != 1 \
                else jnp.concatenate(per_layer, axis=0)
        else:
            g = g1[n]
            local[n] = g.reshape((1,) + g.shape) if g.ndim == 2 and wts[n].ndim == 3 else g

    def chip_partials(group):
        for n in group:
            settle(n)
        stacked = [_split_shards(local[n], SHARDED[n]) for n in group]
        stacked = [s.reshape(N_CHIPS, -1, s.shape[-1]) for s in stacked]
        from_sibling = _swap_halves(stacked)
        return [_pair(s, o, core, f"pair_{n}") for n, s, o in zip(group, stacked, from_sibling)]

    def own_halves(group, parts, received):
        mine = [lax.dynamic_index_in_dim(s, chip, axis=0, keepdims=False) for s in parts]
        return [_sum4(a, r, f"sum_{n}") for n, a, r in zip(group, mine, received)]

    early = [n for n in BIG if n.startswith(('ffn2_', 'ple_', 'fox_'))]
    late = [n for n in BIG + LORA if n not in early]
    for i in reversed(range(depth)):
        s1, sm, s2, sp = saved[i]
        dx, gl['ple_norm'][i], gl['ple_w_gate'][i], gl['ple_w_proj'][i] = _ple_bwd(
            dx, sp, row(ple_norm[i]), full['ple_w_gate'][i], p[i, 0], f"l{i}_ple")
        dx, gl['ffn2_norm'][i], gl['ffn2_w_gu'][i], gl['ffn2_w_down'][i] = _ffn_bwd(
            dx, s2, row(ffn2_norm[i]), full['ffn2_w_gu'][i], full['ffn2_w_down'][i], f"l{i}_ffn2")
        if i % 2 == 0:
            early_parts = chip_partials(early)
            dx, gl['mix_norm'][i], g1['even_w_in'], g1['even_w_out'], g1['swa_sinks'], rg, early_recv = _even_bwd(
                dx, sm, row(mix_norm[i]), full['even_w_in'][i // 2], full['even_w_out'][i // 2], swa_sinks[i // 2], slopes, rprm,
                early_parts)
            g1.update(rg)
        else:
            dx, gl['mix_norm'][i], g1['fox_w_in'], g1['fox_w_out'], g1['fox_b_f'] = _odd_bwd(
                dx, sm, row(mix_norm[i]), full['fox_w_in'][i // 2], full['fox_w_out'][i // 2], row(fox_b_f[i // 2]))
        dx, gl['ffn1_norm'][i], gl['ffn1_w_gu'][i], gl['ffn1_w_down'][i] = _ffn_bwd(
            dx, s1, row(ffn1_norm[i]), full['ffn1_w_gu'][i], full['ffn1_w_down'][i], f"l{i}_ffn1")
    grad_x = dx.reshape(x.shape)

    late_parts = chip_partials(late)
    halves = own_halves(early, early_parts, early_recv) + own_halves(late, late_parts, _scatter_chips(late_parts))
    others = _swap_cores(halves)
    core1 = core.astype(jnp.int32).reshape(1)
    out_g, out_d, out_m, out_v = {}, {}, {}, {}
    for n, a, o in zip(early + late, halves, others):
        out_g[n], out_d[n], out_m[n], out_v[n] = _adamw_big(wts[n], mom[n], var[n], a, o, core1, f"adamw_{n}")

    for n in SMALL:
        if n != 'final_norm':
            settle(n)
    local['final_norm'] = d_final
    small_like = [wts[n] for n in SMALL]
    packed = _pack([local[n] for n in SMALL] + [loss_tile[0, :1]])
    parts = _unpack(_allreduce_small(packed), small_like + [loss_tile[0, :1]])
    g_small = dict(zip(SMALL, parts[:len(SMALL)]))
    loss = parts[-1].reshape(())
    wp, gp, mp, vp = (_pack([src[n] for n in SMALL]) for src in (wts, g_small, mom, var))
    d_p, m_p, v_p = _rowwise(_adamw_math, [wp, gp, mp, vp], [], [(LANES, F32)] * 3, [], tm=512, name="adamw_small")
    out_g.update(g_small)
    out_d.update(zip(SMALL, _unpack(d_p, small_like)))
    out_m.update(zip(SMALL, _unpack(m_p, small_like)))
    out_v.update(zip(SMALL, _unpack(v_p, small_like)))

    fit = lambda dct: [dct[n].reshape(wts[n].shape) for n in WEIGHTS]
    return (loss, grad_x, *fit(out_g), *fit(out_d), *fit(out_m), *fit(out_v))
```

```python
import functools
import math

import jax
import jax.numpy as jnp
from jax import lax
from jax.experimental import pallas as pl
from jax.experimental.pallas import tpu as pltpu

F32 = jnp.float32
BF16 = jnp.bfloat16
MESH = pl.DeviceIdType.MESH

HEAD = 64
SWA_BLOCK = 128
SWA_GROUP = 4
DECAY_LORA = 64
ICLR_LORA = 64
GATE_LORA = 128
NORM_EPS = 1e-6
GN_EPS = 64e-5
L2_EPS = 1e-12
NEG = -1e30

ADAM_LR = 0.001
ADAM_B1 = 0.9
ADAM_B2 = 0.999
ADAM_EPS = 1e-08
ADAM_WD = 0.01
ADAM_STEP = 10

VMEM_LIMIT = 48 * 1024 * 1024
LANES = 128
SEG = 256
MM_TILE = 1408

WEIGHTS = ['ffn1_norm', 'ffn1_w_gu', 'ffn1_w_down', 'mix_norm', 'ffn2_norm', 'ffn2_w_gu', 'ffn2_w_down',
           'ple_norm', 'ple_w_gate', 'ple_w_proj', 'even_w_in', 'even_w_out', 'swa_sinks', 'rwkv_mu',
           'rwkv_w0', 'rwkv_w2', 'rwkv_a0', 'rwkv_a2', 'rwkv_g2', 'rwkv_k_k', 'rwkv_k_a', 'rwkv_r_k',
           'rwkv_ln_w', 'rwkv_ln_b', 'fox_w_in', 'fox_b_f', 'fox_w_out', 'final_norm']
SHARDED = {'ffn1_w_gu': 2, 'ffn1_w_down': 1, 'ffn2_w_gu': 2, 'ffn2_w_down': 1, 'ple_w_gate': 1,
           'ple_w_proj': 2, 'even_w_in': 2, 'even_w_out': 1, 'fox_w_in': 2, 'fox_w_out': 1,
           'rwkv_w2': 2, 'rwkv_a2': 2, 'rwkv_g2': 2}
LORA = ['rwkv_w2', 'rwkv_a2', 'rwkv_g2']
BIG = [n for n in WEIGHTS if n in SHARDED and n not in LORA]
SMALL = [n for n in WEIGHTS if n not in SHARDED]
N_CHIPS = 4
N_DEV = 8


def _tile(dim, target, align):
    best = None
    t = align
    while t <= min(dim, target):
        if dim % t == 0:
            best = t
        t += align
    return best if best is not None else dim


def _params(sem=None):
    return pltpu.CompilerParams(dimension_semantics=sem, vmem_limit_bytes=VMEM_LIMIT)


def _matmul(a, b, *, ta=False, tb=False, alpha=1.0, res=None, out_dtype=F32, name, norm=None, norm_bwd=None):
    if ta:
        kdim, m = a.shape
    else:
        m, kdim = a.shape
    if tb:
        n, kb = b.shape
    else:
        kb, n = b.shape
    assert kdim == kb, (a.shape, b.shape, ta, tb)
    tm = _tile(m, MM_TILE if norm_bwd is None else MM_TILE // 2, 128 if ta else 16)
    tn = _tile(n, MM_TILE, 128)
    tk = _tile(kdim, MM_TILE, 128)
    nk = kdim // tk
    if norm is not None:
        assert (tm == m) if ta else (tk == kdim), "the normalised tile must span whole feature rows"
    if norm_bwd is not None:
        assert tn == n and out_dtype == F32
    a_spec = pl.BlockSpec((tk, tm), lambda j, i, k: (k, i)) if ta else pl.BlockSpec((tm, tk), lambda j, i, k: (i, k))
    b_spec = pl.BlockSpec((tn, tk), lambda j, i, k: (j, k)) if tb else pl.BlockSpec((tk, tn), lambda j, i, k: (k, j))
    o_spec = pl.BlockSpec((tm, tn), lambda j, i, k: (i, j))
    whole = lambda arr: pl.BlockSpec(arr.shape, lambda j, i, k: (0, 0))
    dims = (((0 if ta else 1,), (1 if tb else 0,)), ((), ()))
    ins, in_specs = [a, b], [a_spec, b_spec]
    if norm is not None:
        ins.append(norm)
        in_specs.append(whole(norm))
    if res is not None:
        ins.append(res)
        in_specs.append(o_spec)
    if norm_bwd is not None:
        ins += list(norm_bwd)
        in_specs += [o_spec, whole(norm_bwd[1]), o_spec]
    n_in = len(ins)

    def body(*refs):
        a_ref, b_ref = refs[:2]
        rest = list(refs[2:n_in])
        outs = refs[n_in:]
        av = a_ref[...]
        if norm is not None:
            av = _rms_math(av, rest.pop(0)[...])
        prod = lax.dot_general(av.astype(BF16), b_ref[...].astype(BF16), dims, preferred_element_type=F32)

        def finish(acc):
            o = acc * alpha
            tail = list(rest)
            if res is not None:
                o = o + tail.pop(0)[...]
            if norm_bwd is None:
                outs[0][...] = o.astype(out_dtype)
                return
            x_ref, g_ref, dx_ref = tail
            dx, dg = _rms_bwd_math(x_ref[...], g_ref[...], o)
            outs[0][...] = dx_ref[...] + dx
            first = (pl.program_id(0) == 0) & (pl.program_id(1) == 0)

            @pl.when(first)
            def _():
                outs[1][...] = jnp.zeros_like(outs[1])
            outs[1][...] += dg

        if nk == 1:
            finish(prod)
        else:
            acc_ref = outs[-1]
            k = pl.program_id(2)

            @pl.when(k == 0)
            def _():
                acc_ref[...] = jnp.zeros_like(acc_ref)

            acc_ref[...] += prod
            pl.when(k == nk - 1)(lambda: finish(acc_ref[...]))

    out_specs, out_shape = [o_spec], [jax.ShapeDtypeStruct((m, n), out_dtype)]
    if norm_bwd is not None:
        out_specs.append(pl.BlockSpec((1, n), lambda j, i, k: (0, 0)))
        out_shape.append(jax.ShapeDtypeStruct((1, n), F32))
    sem = ("parallel", "parallel", "arbitrary") if norm_bwd is None else ("arbitrary",) * 3
    outs = pl.pallas_call(
        body, name=name, grid=(n // tn, m // tm, nk), in_specs=in_specs, out_specs=out_specs, out_shape=out_shape,
        scratch_shapes=[] if nk == 1 else [pltpu.VMEM((tm, tn), F32)],
        compiler_params=_params(sem),
    )(*ins)
    return outs[0] if norm_bwd is None else outs


def _rowwise(fn, tiled, full, tiled_out, acc_out, *, tm, name):
    rows = tiled[0].shape[0]
    tm = _tile(rows, tm, 16)
    nt, nf, no, na = len(tiled), len(full), len(tiled_out), len(acc_out)

    def body(*refs):
        ins = [r[...] for r in refs[:nt + nf]]
        outs = fn(*ins)
        if not isinstance(outs, (tuple, list)):
            outs = (outs,)
        assert len(outs) == no + na, (name, len(outs))
        for r, o in zip(refs[nt + nf:nt + nf + no], outs[:no]):
            r[...] = o.astype(r.dtype)
        if na:
            first = pl.program_id(0) == 0
            for r, o in zip(refs[nt + nf + no:], outs[no:]):
                @pl.when(first)
                def _(r=r):
                    r[...] = jnp.zeros_like(r)
                r[...] += o.astype(F32)

    def whole(shape):
        nd = len(shape)
        return pl.BlockSpec(tuple(shape), lambda i, nd=nd: (0,) * nd)

    in_specs = [pl.BlockSpec((tm, t.shape[1]), lambda i: (i, 0)) for t in tiled] + [whole(f.shape) for f in full]
    out_specs = [pl.BlockSpec((tm, w), lambda i: (i, 0)) for w, _ in tiled_out] + [whole(s) for s in acc_out]
    out_shape = [jax.ShapeDtypeStruct((rows, w), d) for w, d in tiled_out] + [jax.ShapeDtypeStruct(tuple(s), F32) for s in acc_out]
    res = pl.pallas_call(
        body, name=name, grid=(rows // tm,), in_specs=in_specs, out_specs=out_specs, out_shape=out_shape,
        compiler_params=_params(("arbitrary",) if na else ("parallel",)),
    )(*tiled, *full)
    return res


def _sigmoid(x):
    return 1.0 / (1.0 + jnp.exp(-x))


def _rms_math(x, g):
    return x * lax.rsqrt(jnp.mean(x * x, axis=-1, keepdims=True) + NORM_EPS) * g


def _rms_bwd_math(x, g, dh):
    rstd = lax.rsqrt(jnp.mean(x * x, axis=-1, keepdims=True) + NORM_EPS)
    xhat = x * rstd
    dxhat = dh * g
    dx = rstd * (dxhat - xhat * jnp.mean(dxhat * xhat, axis=-1, keepdims=True))
    dg = jnp.sum(dh * xhat, axis=0, keepdims=True)
    return dx, dg


def _swiglu_fwd(gu, name):
    f = gu.shape[1] // 2

    def fn(gu):
        g, u = gu[:, :f], gu[:, f:]
        return g * _sigmoid(g) * u
    return _rowwise(fn, [gu], [], [(f, BF16)], [], tm=256, name=name)[0]


def _swiglu_bwd(gu, dact, name):
    f = gu.shape[1] // 2

    def fn(gu, dact):
        g, u = gu[:, :f], gu[:, f:]
        s = _sigmoid(g)
        dg = dact * u * (s * (1.0 + g * (1.0 - s)))
        du = dact * (g * s)
        return jnp.concatenate([dg, du], axis=1)
    return _rowwise(fn, [gu, dact], [], [(2 * f, BF16)], [], tm=256, name=name)[0]


def _ffn_fwd(x, g, w_gu, w_down, tag):
    gu = _matmul(x, w_gu, norm=g, name=f"{tag}_gu")
    act = _swiglu_fwd(gu, f"{tag}_act")
    x2 = _matmul(act, w_down, alpha=0.5, res=x, name=f"{tag}_down")
    return x2, (x, gu, act)


def _ffn_bwd(dx2, saved, g, w_gu, w_down, tag):
    x, gu, act = saved
    dact = _matmul(dx2, w_down, tb=True, alpha=0.5, name=f"{tag}_dact")
    d_down = _matmul(act, dx2, ta=True, alpha=0.5, name=f"{tag}_dwdown")
    dgu = _swiglu_bwd(gu, dact, f"{tag}_dgu")
    d_gu = _matmul(x, dgu, ta=True, norm=g, name=f"{tag}_dwgu")
    dx, dg = _matmul(dgu, w_gu, tb=True, norm_bwd=(x, g, dx2), name=f"{tag}_dh")
    return dx, dg, d_gu, d_down


def _ple_fwd(x, g, w_gate, p, w_proj, tag):
    z = _matmul(x, w_gate, norm=g, name=f"{tag}_gate")
    pp = _matmul(p, w_proj, name=f"{tag}_proj")
    d = x.shape[1]
    x2 = _rowwise(lambda x, z, pp: x + _sigmoid(z) * pp, [x, z, pp], [], [(d, F32)], [], tm=512, name=f"{tag}_comb")[0]
    return x2, (x, z, pp)


def _ple_bwd(dx2, saved, g, w_gate, p, tag):
    x, z, pp = saved
    d = x.shape[1]

    def fn(dx2, z, pp):
        s = _sigmoid(z)
        return dx2 * pp * s * (1.0 - s), dx2 * s
    dz, dpp = _rowwise(fn, [dx2, z, pp], [], [(d, BF16), (d, BF16)], [], tm=512, name=f"{tag}_dcomb")
    d_gate = _matmul(x, dz, ta=True, norm=g, name=f"{tag}_dwgate")
    d_proj = _matmul(p, dpp, ta=True, name=f"{tag}_dwproj")
    dx, dg = _matmul(dz, w_gate, tb=True, norm_bwd=(x, g, dx2), name=f"{tag}_dh")
    return dx, dg, d_gate, d_proj


def _final_loss(x, g, tgt):
    d = x.shape[1]

    def fn(x, tgt, g):
        rstd = lax.rsqrt(jnp.mean(x * x, axis=-1, keepdims=True) + NORM_EPS)
        err = x * rstd * g - tgt
        loss = 0.5 * jnp.sum(jnp.mean(err * err, axis=-1, keepdims=True), axis=0, keepdims=True)
        dx, dg = _rms_bwd_math(x, g, err * (1.0 / d))
        return dx, jnp.zeros((8, LANES), F32) + loss, dg
    return _rowwise(fn, [x, tgt], [g], [(d, F32)], [(8, LANES), (1, d)], tm=256, name="final_loss")


def _swa_masks(n):
    qi = lax.broadcasted_iota(jnp.int32, (SWA_BLOCK, 2 * SWA_BLOCK), 0)
    ki = lax.broadcasted_iota(jnp.int32, (SWA_BLOCK, 2 * SWA_BLOCK), 1)
    dist = qi + SWA_BLOCK - ki
    valid = (dist >= 0) & (dist < SWA_BLOCK) & ((ki >= SWA_BLOCK) | (n > 0))
    return dist.astype(F32), valid


def _dot_nt(a, b):
    return lax.dot_general(a, b, (((1,), (1,)), ((), ())), preferred_element_type=F32)


def _dot_tn(a, b):
    return lax.dot_general(a, b, (((0,), (0,)), ((), ())), preferred_element_type=F32)


def _dot(a, b):
    return jnp.dot(a, b, preferred_element_type=F32)


def _swa_specs(kvh, t):
    nb = t // SWA_BLOCK
    q_spec = pl.BlockSpec((SWA_GROUP, SWA_BLOCK, HEAD), lambda h, n: (h, n, 0))
    cur = pl.BlockSpec((1, SWA_BLOCK, HEAD), lambda h, n: (h, n, 0))
    prev = pl.BlockSpec((1, SWA_BLOCK, HEAD), lambda h, n: (h, jnp.maximum(n - 1, 0), 0))
    smem = pl.BlockSpec(memory_space=pltpu.SMEM)
    stat = pl.BlockSpec((SWA_GROUP, SWA_BLOCK, 1), lambda h, n: (h, n, 0))
    return nb, q_spec, cur, prev, smem, stat


def _swa_fwd(q, k, v, sinks, slopes):
    nh, t, _ = q.shape
    kvh = nh // SWA_GROUP
    nb, q_spec, cur, prev, smem, stat = _swa_specs(kvh, t)
    scale = HEAD ** -0.5

    def body(q_ref, kp_ref, kc_ref, vp_ref, vc_ref, sink_ref, slope_ref, o_ref, lse_ref):
        hk, n = pl.program_id(0), pl.program_id(1)
        dist, valid = _swa_masks(n)
        kk = jnp.concatenate([kp_ref[0], kc_ref[0]], axis=0)
        vv = jnp.concatenate([vp_ref[0], vc_ref[0]], axis=0)
        for g in range(SWA_GROUP):
            h = hk * SWA_GROUP + g
            s = _dot_nt(q_ref[g], kk) * scale - slope_ref[h] * dist
            s = jnp.where(valid, s, NEG)
            m = jnp.maximum(jnp.max(s, axis=-1, keepdims=True), sink_ref[h])
            p = jnp.exp(s - m)
            den = jnp.sum(p, axis=-1, keepdims=True) + jnp.exp(sink_ref[h] - m)
            o_ref[g] = _dot(p.astype(BF16), vv) / den
            lse_ref[g] = m + jnp.log(den)

    return pl.pallas_call(
        body, name="swa_fwd", grid=(kvh, nb),
        in_specs=[q_spec, prev, cur, prev, cur, smem, smem],
        out_specs=[q_spec, stat],
        out_shape=[jax.ShapeDtypeStruct((nh, t, HEAD), F32), jax.ShapeDtypeStruct((nh, t, 1), F32)],
        compiler_params=_params(("parallel", "parallel")),
    )(q, k, k, v, v, sinks, slopes)


def _swa_bwd(q, k, v, sinks, slopes, o, lse, do):
    nh, t, _ = q.shape
    kvh = nh // SWA_GROUP
    nb, q_spec, cur, prev, smem, stat = _swa_specs(kvh, t)
    scale = HEAD ** -0.5
    kv2 = pl.BlockSpec((1, 1, 2 * SWA_BLOCK, HEAD), lambda h, n: (h, n, 0, 0))
    sk = pl.BlockSpec((1, 1, 8, LANES), lambda h, n: (h, n, 0, 0))

    def body(q_ref, kp_ref, kc_ref, vp_ref, vc_ref, sink_ref, slope_ref, o_ref, lse_ref, do_ref,
             dq_ref, dk_ref, dv_ref, ds_ref):
        hk, n = pl.program_id(0), pl.program_id(1)
        dist, valid = _swa_masks(n)
        kk = jnp.concatenate([kp_ref[0], kc_ref[0]], axis=0)
        vv = jnp.concatenate([vp_ref[0], vc_ref[0]], axis=0)
        dk = jnp.zeros((2 * SWA_BLOCK, HEAD), F32)
        dv = jnp.zeros((2 * SWA_BLOCK, HEAD), F32)
        row = lax.broadcasted_iota(jnp.int32, (8, LANES), 0)
        dsink = jnp.zeros((8, LANES), F32)
        for g in range(SWA_GROUP):
            h = hk * SWA_GROUP + g
            qg = q_ref[g]
            s = _dot_nt(qg, kk) * scale - slope_ref[h] * dist
            p = jnp.where(valid, jnp.exp(s - lse_ref[g]), 0.0)
            dog = do_ref[g]
            delta = jnp.sum(dog * o_ref[g], axis=-1, keepdims=True)
            dob = dog.astype(BF16)
            dv = dv + _dot_tn(p.astype(BF16), dob)
            dp = _dot_nt(dob, vv)
            dsc = (p * (dp - delta) * scale).astype(BF16)
            dq_ref[g] = _dot(dsc, kk)
            dk = dk + _dot_tn(dsc, qg)
            dsk = -jnp.sum(jnp.exp(sink_ref[h] - lse_ref[g]) * delta, axis=0, keepdims=True)
            dsink = dsink + jnp.where(row == g, dsk, 0.0)
        dk_ref[0, 0] = dk
        dv_ref[0, 0] = dv
        ds_ref[0, 0] = dsink

    return pl.pallas_call(
        body, name="swa_bwd", grid=(kvh, nb),
        in_specs=[q_spec, prev, cur, prev, cur, smem, smem, q_spec, stat, q_spec],
        out_specs=[q_spec, kv2, kv2, sk],
        out_shape=[jax.ShapeDtypeStruct((nh, t, HEAD), F32),
                   jax.ShapeDtypeStruct((kvh, nb, 2 * SWA_BLOCK, HEAD), F32),
                   jax.ShapeDtypeStruct((kvh, nb, 2 * SWA_BLOCK, HEAD), F32),
                   jax.ShapeDtypeStruct((kvh, nb, 8, LANES), F32)],
        compiler_params=_params(("parallel", "parallel")),
    )(q, k, k, v, v, sinks, slopes, o, lse, do)


def _heads(a):
    t, w = a.shape
    return a.reshape(t, w // HEAD, HEAD).transpose(1, 0, 2)


def _unheads(a):
    h, t, _ = a.shape
    return a.transpose(1, 0, 2).reshape(t, h * HEAD)


def _fold_kv(d2):
    kvh, nb = d2.shape[:2]
    own = d2[:, :, SWA_BLOCK:]
    prev = d2[:, :, :SWA_BLOCK]
    nxt = jnp.concatenate([prev[:, 1:], jnp.zeros_like(prev[:, :1])], axis=1)
    return (own + nxt).reshape(kvh, nb * SWA_BLOCK, HEAD)


FOX_BLOCK = 512
GATE_BLOCK = 256


def _tri3(tri, x):
    hi = x.astype(BF16)
    r1 = x - hi.astype(F32)
    mid = r1.astype(BF16)
    lo = (r1 - mid.astype(F32)).astype(BF16)
    return _dot(tri, hi) + _dot(tri, mid) + _dot(tri, lo)


def _fox_gate_fwd(fz, b_f):
    t, nh = fz.shape
    blk = _tile(t, GATE_BLOCK, 16)
    nblk = t // blk

    def body(fz_ref, b_ref, c_ref):
        ri = lax.broadcasted_iota(jnp.int32, (blk, blk), 0)
        ci = lax.broadcasted_iota(jnp.int32, (blk, blk), 1)
        tri = (ci <= ri).astype(BF16)

        def step(j, carry):
            rows = pl.ds(j * blk, blk)
            z = fz_ref[rows, :] + b_ref[...]
            lf = jnp.minimum(z, 0.0) - jnp.log(1.0 + jnp.exp(-jnp.abs(z)))
            c_ref[rows, :] = carry + _tri3(tri, lf)
            return carry + jnp.sum(lf, axis=0, keepdims=True)
        lax.fori_loop(0, nblk, step, jnp.zeros((1, nh), F32))

    return pl.pallas_call(body, name="fox_gate_fwd", out_shape=jax.ShapeDtypeStruct((t, nh), F32),
                          compiler_params=_params())(fz, b_f)


def _fox_gate_bwd(fz, b_f, dc):
    t, nh = fz.shape
    blk = _tile(t, GATE_BLOCK, 16)
    nblk = t // blk

    def body(fz_ref, b_ref, dc_ref, dfz_ref, db_ref):
        ri = lax.broadcasted_iota(jnp.int32, (blk, blk), 0)
        ci = lax.broadcasted_iota(jnp.int32, (blk, blk), 1)
        tri = (ci >= ri).astype(BF16)

        def step(i, carry):
            acc, db = carry
            rows = pl.ds((nblk - 1 - i) * blk, blk)
            d = dc_ref[rows, :]
            dlf = acc + _tri3(tri, d)
            z = fz_ref[rows, :] + b_ref[...]
            dz = dlf * _sigmoid(-z)
            dfz_ref[rows, :] = dz
            return acc + jnp.sum(d, axis=0, keepdims=True), db + jnp.sum(dz, axis=0, keepdims=True)
        _, db = lax.fori_loop(0, nblk, step, (jnp.zeros((1, nh), F32), jnp.zeros((1, nh), F32)))
        db_ref[...] = db

    return pl.pallas_call(body, name="fox_gate_bwd",
                          out_shape=[jax.ShapeDtypeStruct((t, nh), F32), jax.ShapeDtypeStruct((1, nh), F32)],
                          compiler_params=_params())(fz, b_f, dc)


LOG2E = 1.4426950408889634
FOX_QSCALE = HEAD ** -0.5 * LOG2E


def _lower_triangle(blk):
    return lax.broadcasted_iota(jnp.int32, (blk, blk), 1) <= lax.broadcasted_iota(jnp.int32, (blk, blk), 0)


def _fox_fwd(q2, k, v, c_row2):
    nh, t, _ = q2.shape
    blk = c_row2.shape[-1]
    nb = t // blk

    def body(q_ref, k_ref, v_ref, ck_ref, o_ref, lse_ref):
        qi = pl.program_id(1)
        q = q_ref[0]

        def step(j, carry, diagonal):
            m, l, acc = carry
            ks = pl.ds(j * blk, blk)
            s = _dot_nt(q, k_ref[0, ks, :]) - ck_ref[0, j]
            if diagonal:
                s = jnp.where(_lower_triangle(blk), s, NEG)
            m2 = jnp.maximum(m, jnp.max(s, axis=-1, keepdims=True))
            a = jnp.exp2(m - m2)
            p = jnp.exp2(s - m2)
            l = a * l + jnp.sum(p, axis=-1, keepdims=True)
            acc = a * acc + _dot(p.astype(BF16), v_ref[0, ks, :])
            return m2, l, acc
        init = (jnp.full((blk, 1), NEG, F32), jnp.zeros((blk, 1), F32), jnp.zeros((blk, HEAD), F32))
        carry = lax.fori_loop(0, qi, lambda j, c: step(j, c, False), init)
        m, l, acc = step(qi, carry, True)
        o_ref[0] = acc / l
        lse_ref[0] = m + jnp.log(l) * LOG2E

    qb = pl.BlockSpec((1, blk, HEAD), lambda h, i: (h, i, 0))
    full = pl.BlockSpec((1, t, HEAD), lambda h, i: (h, 0, 0))
    colb = pl.BlockSpec((1, blk, 1), lambda h, i: (h, i, 0))
    rowf = pl.BlockSpec((1, nb, 1, blk), lambda h, i: (h, 0, 0, 0))
    return pl.pallas_call(
        body, name="fox_fwd", grid=(nh, nb), in_specs=[qb, full, full, rowf], out_specs=[qb, colb],
        out_shape=[jax.ShapeDtypeStruct((nh, t, HEAD), F32), jax.ShapeDtypeStruct((nh, t, 1), F32)],
        compiler_params=_params(("parallel", "parallel")),
    )(q2, k, v, c_row2)


def _fox_bwd(q2, k, v, c_row2, lse2, o, do):
    nh, t, _ = q2.shape
    blk = c_row2.shape[-1]
    nb = t // blk
    scale = HEAD ** -0.5

    def body(q_ref, do_ref, lse_ref, o_ref, k_ref, v_ref, ck_ref, dq_ref, dk_ref, dv_ref, dc_ref, dcq_ref, dl_ref):
        kb = pl.program_id(1)

        @pl.when(kb == 0)
        def _():
            dq_ref[...] = jnp.zeros_like(dq_ref)
            dcq_ref[...] = jnp.zeros_like(dcq_ref)
            for i in range(nb):
                rs = pl.ds(i * blk, blk)
                dl_ref[rs, :] = jnp.sum(do_ref[0, rs, :].astype(F32) * o_ref[0, rs, :], axis=-1, keepdims=True)

        k = k_ref[0]
        v = v_ref[0]
        ck = ck_ref[0, 0]

        def step(i, carry, diagonal):
            dk, dv, dck = carry
            rs = pl.ds(i * blk, blk)
            q = q_ref[0, rs, :]
            do = do_ref[0, rs, :]
            p = jnp.exp2(_dot_nt(q, k) - ck - lse_ref[0, rs, :])
            if diagonal:
                p = jnp.where(_lower_triangle(blk), p, 0.0)
            dv = dv + _dot_tn(p.astype(BF16), do)
            ds = p * (_dot_nt(do, v) - dl_ref[rs, :])
            dck = dck - jnp.sum(ds, axis=0, keepdims=True)
            dcq_ref[0, rs, :] += jnp.sum(ds, axis=1, keepdims=True)
            dsb = ds.astype(BF16)
            dk = dk + _dot_tn(dsb, q)
            dq_ref[0, rs, :] += _dot(dsb, k) * scale
            return dk, dv, dck
        init = (jnp.zeros((blk, HEAD), F32), jnp.zeros((blk, HEAD), F32), jnp.zeros((1, blk), F32))
        carry = step(kb, init, True)
        dk, dv, dck = lax.fori_loop(kb + 1, nb, lambda i, c: step(i, c, False), carry)
        dk_ref[0] = dk * (1.0 / LOG2E)
        dv_ref[0] = dv
        dc_ref[0, 0] = dck

    full = pl.BlockSpec((1, t, HEAD), lambda h, j: (h, 0, 0))
    colf = pl.BlockSpec((1, t, 1), lambda h, j: (h, 0, 0))
    kb_spec = pl.BlockSpec((1, blk, HEAD), lambda h, j: (h, j, 0))
    rowb = pl.BlockSpec((1, 1, 1, blk), lambda h, j: (h, j, 0, 0))
    return pl.pallas_call(
        body, name="fox_bwd", grid=(nh, nb),
        in_specs=[full, full, colf, full, kb_spec, kb_spec, rowb],
        out_specs=[full, kb_spec, kb_spec, rowb, colf],
        scratch_shapes=[pltpu.VMEM((t, 1), F32)],
        out_shape=[jax.ShapeDtypeStruct((nh, t, HEAD), F32), jax.ShapeDtypeStruct((nh, t, HEAD), F32),
                   jax.ShapeDtypeStruct((nh, t, HEAD), F32), jax.ShapeDtypeStruct((nh, nb, 1, blk), F32),
                   jax.ShapeDtypeStruct((nh, t, 1), F32)],
        compiler_params=_params(("parallel", "arbitrary")),
    )(q2, do, lse2, o, k, v, c_row2)


def _split3(x):
    hi = x.astype(BF16)
    r1 = x - hi.astype(F32)
    mid = r1.astype(BF16)
    lo = (r1 - mid.astype(F32)).astype(BF16)
    return hi, mid, lo


def _segsum_raw(a, bm, parts=3):
    outs = []
    for s in range(a.shape[-1] // SEG):
        x = a[:, s * SEG:(s + 1) * SEG]
        if parts == 3:
            hi, mid, lo = _split3(x)
            outs.append(_dot(hi, bm) + _dot(mid, bm) + _dot(lo, bm))
        elif parts == 1:
            outs.append(_dot(x.astype(BF16), bm))
        else:
            hi = x.astype(BF16)
            lo = (x - hi.astype(F32)).astype(BF16)
            outs.append(_dot(hi, bm) + _dot(lo, bm))
    return outs[0] if len(outs) == 1 else jnp.concatenate(outs, axis=-1)


@jax.custom_vjp
def _segsum(a, bm):
    return _segsum_raw(a, bm)


def _segsum_f(a, bm):
    return _segsum_raw(a, bm), bm


def _segsum_b(bm, ct):
    return _segsum_raw(ct, bm), jnp.zeros_like(bm)


_segsum.defvjp(_segsum_f, _segsum_b)


@jax.custom_vjp
def _bdot(a, w):
    return _dot(a.astype(BF16), w.astype(BF16))


def _bdot_f(a, w):
    return _bdot(a, w), (a, w)


def _bdot_b(saved, ct):
    a, w = saved
    ctb = ct.astype(BF16)
    return _dot_nt(ctb, w.astype(BF16)), _dot_tn(a.astype(BF16), ctb)


_bdot.defvjp(_bdot_f, _bdot_b)


def _softplus(z):
    return jnp.maximum(z, 0.0) + jnp.log(1.0 + jnp.exp(-jnp.abs(z)))


def _rwkv_pre_math(hb, hbp, mu, w0, a0, k_k, k_a, w2p, a2p, g2, bm):
    rd = w0.shape[-1]
    m = hb + (hbp - hb) * mu
    r, k, v = m[:, :rd], m[:, rd:2 * rd], m[:, 2 * rd:3 * rd]
    xwa = m[:, 3 * rd:3 * rd + LANES]
    xg = m[:, 3 * rd + LANES:]
    wlog = -_softplus(-(w0 + _bdot(jnp.tanh(xwa), w2p))) - 0.5
    decay = jnp.exp(-jnp.exp(wlog))
    a = _sigmoid(a0 + _bdot(xwa, a2p))
    g = _bdot(_sigmoid(xg), g2)
    kk0 = k * k_k
    kk = kk0 / jnp.maximum(jnp.sqrt(_segsum(kk0 * kk0, bm)), L2_EPS)
    kp = k * (1.0 + (a - 1.0) * k_a)
    return r, decay, kp, v, kk, kk * a, g


def _rwkv_post_math(y, r, kp, v, g, ln_w, ln_b, r_k, bm):
    mean = _segsum(y, bm) * (1.0 / HEAD)
    yc = y - mean
    var = _segsum(yc * yc, bm) * (1.0 / HEAD)
    yn = yc * lax.rsqrt(var + GN_EPS) * ln_w + ln_b
    bonus = _segsum(r * kp * r_k, bm) * v
    return (yn + bonus) * g


def _rwkv_pre(hb, hbp, prm, bm):
    rd = prm[1].shape[-1]
    outs = [(rd, F32)] * 7
    return _rowwise(_rwkv_pre_math, [hb, hbp], list(prm) + [bm], outs, [], tm=256, name="rwkv_pre")


def _rwkv_pre_bwd(hb, hbp, cts, prm, bm):
    n_in = hb.shape[1]

    def fn(hb, hbp, *rest):
        ct, full = rest[:7], rest[7:]
        prm_v, bm_v = full[:-1], full[-1]
        _, vjp = jax.vjp(lambda hb, hbp, *p: _rwkv_pre_math(hb, hbp, *p, bm_v), hb, hbp, *prm_v)
        g = vjp(tuple(ct))
        return g
    acc = [p.shape for p in prm]
    res = _rowwise(fn, [hb, hbp] + list(cts), list(prm) + [bm], [(n_in, F32)] * 2, acc, tm=256, name="rwkv_pre_bwd")
    return res[:2], res[2:]


def _rwkv_post(y, r, kp, v, g, prm, bm):
    rd = y.shape[1]
    return _rowwise(_rwkv_post_math, [y, r, kp, v, g], list(prm) + [bm], [(rd, F32)], [], tm=256, name="rwkv_post")[0]


def _rwkv_post_bwd(y, r, kp, v, g, dout, prm, bm):
    rd = y.shape[1]

    def fn(y, r, kp, v, g, dout, ln_w, ln_b, r_k, bm_v):
        _, vjp = jax.vjp(lambda *a: _rwkv_post_math(*a, bm_v), y, r, kp, v, g, ln_w, ln_b, r_k)
        return vjp(dout)
    acc = [p.shape for p in prm]
    res = _rowwise(fn, [y, r, kp, v, g, dout], list(prm) + [bm], [(rd, F32)] * 5, acc, tm=256, name="rwkv_post_bwd")
    return res[:5], res[5:]


SCAN_FWD_CHUNK = 32
SCAN_BWD_CHUNK = 16
SCAN_PARTS = 2
READOUT_PARTS = 1
SPREAD_PARTS = 2


def _spread_rows(x, mk, bm):
    c, _, rd = x.shape
    hi = x.astype(BF16)
    keep = mk[None] != 0.0
    tile = lambda p: jnp.where(keep, p, jnp.zeros((), BF16)).reshape(c * HEAD, rd)
    parts = (tile(hi),) if SPREAD_PARTS == 1 else (tile(hi), tile((x - hi.astype(F32)).astype(BF16)))
    outs = [sum(_dot(p[:, s * SEG:(s + 1) * SEG], bm) for p in parts) for s in range(rd // SEG)]
    out = outs[0] if len(outs) == 1 else jnp.concatenate(outs, axis=-1)
    return out.reshape(c, HEAD, rd)


def _seg3d(x, bm, parts):
    c, n, rd = x.shape
    return _segsum_raw(x.reshape(c * n, rd), bm, parts).reshape(c, n, rd)


def _head_dots(x, bm):
    n, _, rd = x.shape
    y = _segsum_raw(jnp.broadcast_to(x, (n, 8, rd)).reshape(n * 8, rd), bm, 3).reshape(n, 8, rd)
    return jnp.sum(y, axis=1, keepdims=True) * 0.125


def _rwkv_scan_fwd(w, kk, b, k, v, r, bm, mk, side=()):
    t, _, rd = w.shape
    c = _tile(t, SCAN_FWD_CHUNK, 8)
    assert c % 2 == 0
    ns = len(side)
    steps = t // c

    def body(*refs):
        w_ref, kk_ref, b_ref, k_ref, v_ref, r_ref, bm_ref, mk_ref = refs[:8]
        side_in = refs[8:8 + ns]
        y_ref, s_ref = refs[8 + ns:10 + ns]
        side_out = refs[10 + ns:10 + 2 * ns]
        state, vb, beta, gamma = refs[10 + 2 * ns:14 + 2 * ns]
        if ns:
            start, forward, finish = _gather_phases(side, side_in, side_out, refs[14 + 2 * ns:])
            pl.when(pl.program_id(0) == 0)(start)
            pl.when(pl.program_id(0) == steps // 2)(forward)
            pl.when(pl.program_id(0) == steps - 1)(finish)

        @pl.when(pl.program_id(0) == 0)
        def _():
            state[...] = jnp.zeros_like(state)

        bmv = bm_ref[...]
        mkv = mk_ref[...]
        vb[...] = _spread_rows(v_ref[...], mkv, bmv)
        kk_next = kk_ref[pl.ds(1, c - 1)]
        beta[pl.ds(0, c - 1)] = _head_dots(b_ref[pl.ds(0, c - 1)] * kk_next, bmv)
        gamma[pl.ds(0, c - 1)] = _head_dots(k_ref[pl.ds(0, c - 1)] * kk_next, bmv)

        def pair(p, s):
            ia, ib = 2 * p, 2 * p + 1
            sk_a = _segsum_raw(s * kk_ref[ia], bmv, SCAN_PARTS)
            through = _segsum_raw(s * (w_ref[ia] * kk_ref[ib]), bmv, SCAN_PARTS)
            va = vb[ia]
            s = s * w_ref[ia] - sk_a * b_ref[ia] + va * k_ref[ia]
            s_ref[ia] = s
            sk_b = through - sk_a * beta[ia] + va * gamma[ia]
            s = s * w_ref[ib] - sk_b * b_ref[ib] + vb[ib] * k_ref[ib]
            s_ref[ib] = s
            return s
        state[...] = lax.fori_loop(0, c // 2, pair, state[...])
        yb = _seg3d(s_ref[...] * r_ref[...], bmv, READOUT_PARTS)
        y_ref[...] = jnp.sum(yb * mkv[None], axis=1, keepdims=True)

    vec = pl.BlockSpec((c, 1, rd), lambda i: (i, 0, 0))
    res = pl.pallas_call(
        body, name="rwkv_scan_fwd", grid=(steps,),
        in_specs=[vec] * 6 + [pl.BlockSpec((SEG, SEG), lambda i: (0, 0)), pl.BlockSpec((HEAD, rd), lambda i: (0, 0))]
        + [HBM_SPEC] * ns,
        out_specs=[vec, pl.BlockSpec((c, HEAD, rd), lambda i: (i, 0, 0))] + [HBM_SPEC] * ns,
        out_shape=[jax.ShapeDtypeStruct((t, 1, rd), F32), jax.ShapeDtypeStruct((t, HEAD, rd), F32)] + _gather_shapes(side),
        scratch_shapes=[pltpu.VMEM((HEAD, rd), F32), pltpu.VMEM((c, HEAD, rd), F32),
                        pltpu.VMEM((c, 1, rd), F32), pltpu.VMEM((c, 1, rd), F32)] + (_gather_sems(ns) if ns else []),
        compiler_params=_params(("arbitrary",)),
    )(w, kk, b, k, v, r, bm, mk, *side)
    return res[0], res[1], list(res[2:])


def _rwkv_scan_bwd(w, kk, b, k, v, r, dy, states, bm, mk, side=()):
    t, _, rd = w.shape
    c = _tile(t, SCAN_BWD_CHUNK, 8)
    nc = t // c
    assert c % 2 == 0
    ns = len(side)

    def body(*refs):
        w_ref, kk_ref, b_ref, k_ref, v_ref, r_ref, dy_ref, s_ref, sp_ref, bm_ref, mk_ref = refs[:11]
        dr_ref, dw_ref, dk_ref, dv_ref, dkk_ref, db_ref = refs[11 + ns:17 + ns]
        gstate, sp, vb, dyb, skb, gall, gball, delta, eps = refs[17 + 2 * ns:26 + 2 * ns]
        step_id = pl.program_id(0)
        if ns:
            start, finish = _scatter_phases(ns, refs[11:11 + ns], refs[17 + ns:17 + 2 * ns], refs[26 + 2 * ns:])
            pl.when(step_id == 0)(start)
            pl.when(step_id == nc - 1)(finish)

        @pl.when(step_id == 0)
        def _():
            gstate[...] = jnp.zeros_like(gstate)

        bmv = bm_ref[...]
        mkv = mk_ref[...]
        sp[0] = jnp.where(step_id == nc - 1, 0.0, sp_ref[0])
        sp[1:c] = s_ref[0:c - 1]
        vb[...] = _spread_rows(v_ref[...], mkv, bmv)
        dyb[...] = _spread_rows(dy_ref[...], mkv, bmv)
        skb[...] = _seg3d(sp[...] * kk_ref[...], bmv, READOUT_PARTS)
        dr_ref[...] = jnp.sum(s_ref[...] * dyb[...], axis=1, keepdims=True)
        delta[pl.ds(0, c - 1)] = _head_dots(kk_ref[pl.ds(1, c - 1)] * b_ref[pl.ds(0, c - 1)], bmv)
        eps[...] = _head_dots(r_ref[...] * b_ref[...], bmv)

        def pair(p, g):
            ib = c - 1 - 2 * p
            ia = ib - 1
            g = g + dyb[ib] * r_ref[ib]
            gall[ib] = g
            gb_b = _segsum_raw(g * b_ref[ib], bmv, SCAN_PARTS)
            through = _segsum_raw(g * (w_ref[ib] * b_ref[ia]), bmv, SCAN_PARTS)
            gball[ib] = gb_b
            dya = dyb[ia]
            g = g * w_ref[ib] - gb_b * kk_ref[ib] + dya * r_ref[ia]
            gall[ia] = g
            gb_a = through - gb_b * delta[ia] + dya * eps[ia]
            gball[ia] = gb_a
            return g * w_ref[ia] - gb_a * kk_ref[ia]
        gstate[...] = lax.fori_loop(0, c // 2, pair, gstate[...])
        ga = gall[...]
        dv_ref[...] = jnp.sum(_seg3d(ga * k_ref[...], bmv, READOUT_PARTS) * mkv[None], axis=1, keepdims=True)
        dk_ref[...] = jnp.sum(ga * vb[...], axis=1, keepdims=True)
        dw_ref[...] = jnp.sum(ga * sp[...], axis=1, keepdims=True)
        db_ref[...] = -jnp.sum(ga * skb[...], axis=1, keepdims=True)
        dkk_ref[...] = -jnp.sum(sp[...] * gball[...], axis=1, keepdims=True)

    vec = pl.BlockSpec((c, 1, rd), lambda i: (nc - 1 - i, 0, 0))
    st = pl.BlockSpec((c, HEAD, rd), lambda i: (nc - 1 - i, 0, 0))
    st_prev = pl.BlockSpec((1, HEAD, rd), lambda i: (jnp.maximum((nc - 1 - i) * c - 1, 0), 0, 0))
    big = pltpu.VMEM((c, HEAD, rd), F32)
    res = pl.pallas_call(
        body, name="rwkv_scan_bwd", grid=(nc,),
        in_specs=[vec] * 7 + [st, st_prev, pl.BlockSpec((SEG, SEG), lambda i: (0, 0)),
                              pl.BlockSpec((HEAD, rd), lambda i: (0, 0))] + [HBM_SPEC] * ns,
        out_specs=[vec] * 6 + [HBM_SPEC] * ns,
        out_shape=[jax.ShapeDtypeStruct((t, 1, rd), F32)] * 6 + _scatter_shapes(side),
        scratch_shapes=[pltpu.VMEM((HEAD, rd), F32), big, big, big, big, big, big,
                        pltpu.VMEM((c, 1, rd), F32), pltpu.VMEM((c, 1, rd), F32)] + (_scatter_sems(ns) if ns else []),
        compiler_params=_params(("arbitrary",)),
    )(w, kk, b, k, v, r, dy, states, states, bm, mk, *side)
    return res[:6], list(res[6:])


def _shift_down(a):
    return jnp.concatenate([jnp.zeros_like(a[:1]), a[:-1]], axis=0)


def _rwkv_consts(rd):
    i = jnp.arange(SEG) // HEAD
    bm = (i[:, None] == i[None, :]).astype(BF16)
    mk = (jnp.arange(HEAD)[:, None] == (jnp.arange(rd) % HEAD)[None, :]).astype(F32)
    return bm, mk


def _lora_pad(w2, a2):
    z = jnp.zeros_like(w2)
    return jnp.concatenate([w2, z], axis=0), jnp.concatenate([jnp.zeros_like(a2), a2], axis=0)


def _rwkv_fwd(hb, prm, side=()):
    rd = prm['w0'].shape[-1]
    t = hb.shape[0]
    bm, mk = _rwkv_consts(rd)
    hbp = _shift_down(hb)
    pre_prm = (prm['mu'], prm['w0'], prm['a0'], prm['k_k'], prm['k_a'], prm['w2p'], prm['a2p'], prm['g2'])
    r, w, kp, v, kk, b, g = _rwkv_pre(hb, hbp, pre_prm, bm)
    to3 = lambda a: a.reshape(t, 1, rd)
    y3, states, gathered = _rwkv_scan_fwd(to3(w), to3(kk), to3(b), to3(kp), to3(v), to3(r), bm, mk, side)
    y = y3.reshape(t, rd)
    post_prm = (prm['ln_w'], prm['ln_b'], prm['r_k'])
    out = _rwkv_post(y, r, kp, v, g, post_prm, bm)
    return out, (hb, hbp, r, w, kp, v, kk, b, g, y, states), gathered


def _rwkv_bwd(dout, saved, prm, side=()):
    hb, hbp, r, w, kp, v, kk, b, g, y, states = saved
    rd = prm['w0'].shape[-1]
    t = hb.shape[0]
    bm, mk = _rwkv_consts(rd)
    post_prm = (prm['ln_w'], prm['ln_b'], prm['r_k'])
    (dy, dr1, dkp1, dv1, dg), (d_ln_w, d_ln_b, d_r_k) = _rwkv_post_bwd(y, r, kp, v, g, dout, post_prm, bm)
    to3 = lambda a: a.reshape(t, 1, rd)
    (dr2, dw, dk2, dv2, dkk, db), received = _rwkv_scan_bwd(to3(w), to3(kk), to3(b), to3(kp), to3(v), to3(r), to3(dy), states,
                                                            bm, mk, side)
    to2 = lambda a: a.reshape(t, rd)
    cts = [dr1 + to2(dr2), to2(dw), dkp1 + to2(dk2), dv1 + to2(dv2), to2(dkk), to2(db), dg]
    pre_prm = (prm['mu'], prm['w0'], prm['a0'], prm['k_k'], prm['k_a'], prm['w2p'], prm['a2p'], prm['g2'])
    (dhb, dhbp), gp = _rwkv_pre_bwd(hb, hbp, cts, pre_prm, bm)
    dhb = dhb + jnp.concatenate([dhbp[1:], jnp.zeros_like(dhbp[:1])], axis=0)
    d_mu, d_w0, d_a0, d_k_k, d_k_a, d_w2p, d_a2p, d_g2 = gp
    grads = {'rwkv_mu': d_mu, 'rwkv_w0': d_w0, 'rwkv_a0': d_a0, 'rwkv_k_k': d_k_k, 'rwkv_k_a': d_k_a,
             'rwkv_w2': d_w2p[:DECAY_LORA], 'rwkv_a2': d_a2p[DECAY_LORA:], 'rwkv_g2': d_g2,
             'rwkv_ln_w': d_ln_w, 'rwkv_ln_b': d_ln_b, 'rwkv_r_k': d_r_k}
    return dhb, grads, received


def _even_fwd(x, g, w_in, w_out, sinks, slopes, rprm, side=()):
    d = x.shape[1]
    q_w, kv_w = d // 2, d // 8
    proj = _matmul(x, w_in, norm=g, name="even_in")
    qa, ka, va, hb = proj[:, :q_w], proj[:, q_w:q_w + kv_w], proj[:, q_w + kv_w:q_w + 2 * kv_w], proj[:, q_w + 2 * kv_w:]
    qh, kh, vh = _heads(qa).astype(BF16), _heads(ka).astype(BF16), _heads(va).astype(BF16)
    oa, lse = _swa_fwd(qh, kh, vh, sinks, slopes)
    yb, rsaved, gathered = _rwkv_fwd(hb, rprm, side)
    cat = jnp.concatenate([_unheads(oa), yb], axis=1)
    x2 = _matmul(cat, w_out, res=x, name="even_out")
    return x2, (x, qh, kh, vh, oa, lse, cat, rsaved), gathered


def _even_bwd(dx2, saved, g, w_in, w_out, sinks, slopes, rprm, side=()):
    x, qh, kh, vh, oa, lse, cat, rsaved = saved
    d = x.shape[1]
    dcat = _matmul(dx2, w_out, tb=True, name="even_dcat")
    d_out = _matmul(cat, dx2, ta=True, name="even_dwout")
    dya, dyb = dcat[:, :d // 2], dcat[:, d // 2:]
    dq, dk2, dv2, dsk = _swa_bwd(qh, kh, vh, sinks, slopes, oa, lse, _heads(dya))
    d_sinks = jnp.sum(dsk[:, :, :SWA_GROUP, 0], axis=1).reshape(1, -1)
    dhb, rgrads, received = _rwkv_bwd(dyb, rsaved, rprm, side)
    dproj = jnp.concatenate([_unheads(dq), _unheads(_fold_kv(dk2)), _unheads(_fold_kv(dv2)), dhb], axis=1)
    d_in = _matmul(x, dproj, ta=True, norm=g, name="even_dwin")
    dx, dg = _matmul(dproj, w_in, tb=True, norm_bwd=(x, g, dx2), name="even_dhn")
    return dx, dg, d_in, d_out, d_sinks, rgrads, received


def _odd_fwd(x, g, w_in, w_out, b_f):
    d = x.shape[1]
    t = x.shape[0]
    nh = d // HEAD
    qkv = _matmul(x, w_in[:, :3 * d], norm=g, name="odd_in")
    fz = _matmul(x, w_in[:, 3 * d:], norm=g, name="odd_fz")
    c = _fox_gate_fwd(fz, b_f)
    blk = _tile(t, FOX_BLOCK, 128)
    c_row = (c.T * LOG2E).reshape(nh, t // blk, 1, blk)
    qh = _heads(qkv[:, :d] * FOX_QSCALE).astype(BF16)
    kh, vh = (_heads(qkv[:, i * d:(i + 1) * d]).astype(BF16) for i in (1, 2))
    o, lse = _fox_fwd(qh, kh, vh, c_row)
    y = _unheads(o)
    x2 = _matmul(y, w_out, res=x, name="odd_out")
    return x2, (x, fz, qh, kh, vh, c_row, o, lse, y)


def _odd_bwd(dx2, saved, g, w_in, w_out, b_f):
    x, fz, qh, kh, vh, c_row, o, lse, y = saved
    d = x.shape[1]
    t = x.shape[0]
    nh = d // HEAD
    dy = _matmul(dx2, w_out, tb=True, name="odd_dy")
    d_out = _matmul(y, dx2, ta=True, name="odd_dwout")
    dq, dk, dv, dcr, dcc = _fox_bwd(qh, kh, vh, c_row, lse, o, _heads(dy).astype(BF16))
    dfz, d_bf = _fox_gate_bwd(fz, b_f, (dcr.reshape(nh, t) + dcc.reshape(nh, t)).T)
    dqkv = jnp.concatenate([_unheads(dq), _unheads(dk), _unheads(dv)], axis=1)
    d_in = jnp.concatenate([_matmul(x, dqkv, ta=True, norm=g, name="odd_dwin"),
                            _matmul(x, dfz, ta=True, norm=g, name="odd_dwin_fz")], axis=1)
    dhn_fz = _matmul(dfz, w_in[:, 3 * d:], tb=True, name="odd_dhn_fz")
    dx, dg = _matmul(dqkv, w_in[:, :3 * d], tb=True, res=dhn_fz, norm_bwd=(x, g, dx2), name="odd_dhn")
    return dx, dg, d_in, d_out, d_bf


def _place():
    return lax.axis_index("x"), lax.axis_index("y"), lax.axis_index("c")


def _other_chips(x, y):
    return [(1 - x, y), (x, 1 - y), (1 - x, 1 - y)]


HBM_SPEC = pl.BlockSpec(memory_space=pltpu.HBM)


def _rows(ref, which, h):
    return ref.at[pl.ds(which * h, h)]


def _gather_phases(shards, ins, outs, sems):
    ici_send, ici_recv, d2d_send, d2d_recv = sems
    x, y, c = _place()
    me = 2 * x + y
    sibling = (x, y, 1 - c)
    pairs = [(i, j, px, py) for i in range(len(shards)) for j, (px, py) in enumerate(_other_chips(x, y))]
    half = lambda i, ref, which: _rows(ref, which, shards[i].shape[0] // 2)

    def over_ici(i, j, px, py, slot):
        return pltpu.make_async_remote_copy(
            src_ref=half(i, ins[i], c), dst_ref=half(i, outs[i].at[slot], c), send_sem=ici_send.at[3 * i + j],
            recv_sem=ici_recv.at[3 * i + j], device_id=(px, py, c), device_id_type=MESH)

    def over_d2d(i, j, px, py, which):
        part = half(i, outs[i].at[2 * px + py], which)
        return pltpu.make_async_remote_copy(src_ref=part, dst_ref=part, send_sem=d2d_send.at[3 * i + j],
                                            recv_sem=d2d_recv.at[3 * i + j], device_id=sibling, device_id_type=MESH)

    def start():
        for i, j, px, py in pairs:
            over_ici(i, j, px, py, me).start()

    def forward():
        for i, j, px, py in pairs:
            over_ici(i, j, px, py, 2 * px + py).wait_recv()
            over_d2d(i, j, px, py, c).start()

    def finish():
        for i, j, px, py in pairs:
            over_d2d(i, j, px, py, 1 - c).wait_recv()
        for i, j, px, py in pairs:
            over_ici(i, j, px, py, me).wait_send()
            over_d2d(i, j, px, py, c).wait_send()

    return start, forward, finish


def _gather_sems(n):
    return [pltpu.SemaphoreType.DMA((3 * n,))] * 4


def _gather_shapes(shards):
    return [jax.ShapeDtypeStruct((N_CHIPS,) + s.shape, s.dtype) for s in shards]


def _gather_chips(shards):
    n = len(shards)

    def body(*refs):
        start, forward, finish = _gather_phases(shards, refs[:n], refs[n:2 * n], refs[2 * n:])
        start()
        forward()
        finish()

    return pl.pallas_call(
        body, name="gather_weights", in_specs=[HBM_SPEC] * n, out_specs=[HBM_SPEC] * n,
        out_shape=_gather_shapes(shards), scratch_shapes=_gather_sems(n),
    )(*shards)


def _swap_halves(stacked):
    n = len(stacked)
    halves = [s.shape[1] // 2 for s in stacked]

    def body(*refs):
        ins, outs = refs[:n], refs[n:2 * n]
        send_sems, recv_sems = refs[2 * n:]
        x, y, c = _place()
        sends = []
        for i in range(n):
            cp = pltpu.make_async_remote_copy(
                src_ref=ins[i].at[:, pl.ds((1 - c) * halves[i], halves[i])], dst_ref=outs[i], send_sem=send_sems.at[i],
                recv_sem=recv_sems.at[i], device_id=(x, y, 1 - c), device_id_type=MESH)
            cp.start()
            sends.append(cp)
        for cp in sends:
            cp.wait_recv()
        for cp in sends:
            cp.wait_send()

    return pl.pallas_call(
        body, name="swap_halves", in_specs=[HBM_SPEC] * n, out_specs=[HBM_SPEC] * n,
        out_shape=[jax.ShapeDtypeStruct((N_CHIPS, h) + s.shape[2:], s.dtype) for s, h in zip(stacked, halves)],
        scratch_shapes=[pltpu.SemaphoreType.DMA((n,)), pltpu.SemaphoreType.DMA((n,))],
    )(*stacked)


def _scatter_phases(n, ins, outs, sems):
    send_sems, recv_sems = sems
    x, y, c = _place()
    pairs = [(i, j, px, py) for i in range(n) for j, (px, py) in enumerate(_other_chips(x, y))]

    def copy(i, j, px, py):
        return pltpu.make_async_remote_copy(src_ref=ins[i].at[2 * px + py], dst_ref=outs[i].at[j], send_sem=send_sems.at[3 * i + j],
                                            recv_sem=recv_sems.at[3 * i + j], device_id=(px, py, c), device_id_type=MESH)

    def start():
        for p in pairs:
            copy(*p).start()

    def finish():
        for p in pairs:
            copy(*p).wait_recv()
        for p in pairs:
            copy(*p).wait_send()

    return start, finish


def _scatter_sems(n):
    return [pltpu.SemaphoreType.DMA((3 * n,))] * 2


def _scatter_shapes(stacked):
    return [jax.ShapeDtypeStruct((3,) + s.shape[1:], s.dtype) for s in stacked]


def _scatter_chips(stacked):
    n = len(stacked)

    def body(*refs):
        start, finish = _scatter_phases(n, refs[:n], refs[n:2 * n], refs[2 * n:])
        start()
        finish()

    return pl.pallas_call(
        body, name="scatter_grads", in_specs=[HBM_SPEC] * n, out_specs=[HBM_SPEC] * n,
        out_shape=_scatter_shapes(stacked), scratch_shapes=_scatter_sems(n),
    )(*stacked)


def _swap_cores(arrs):
    n = len(arrs)

    def body(*refs):
        ins, outs = refs[:n], refs[n:2 * n]
        send_sems, recv_sems = refs[2 * n:]
        x, y, c = _place()
        sends = []
        for i in range(n):
            cp = pltpu.make_async_remote_copy(src_ref=ins[i], dst_ref=outs[i], send_sem=send_sems.at[i], recv_sem=recv_sems.at[i],
                                              device_id=(x, y, 1 - c), device_id_type=MESH)
            cp.start()
            sends.append(cp)
        for cp in sends:
            cp.wait_recv()
        for cp in sends:
            cp.wait_send()

    sem = pltpu.SemaphoreType.DMA((n,))
    return pl.pallas_call(
        body, name="swap_cores", in_specs=[HBM_SPEC] * n, out_specs=[HBM_SPEC] * n,
        out_shape=[jax.ShapeDtypeStruct(s.shape, s.dtype) for s in arrs],
        scratch_shapes=[sem, sem],
    )(*arrs)


def _allreduce_small(buf):
    rows = buf.shape[0]

    def body(in_ref, out_ref, gat, send_sems, recv_sems):
        x, y, c = _place()
        me = 4 * x + 2 * y + c
        gat[me] = in_ref[...]
        sends = []
        for k in range(1, N_DEV):
            bx, by, bc = (k >> 2) & 1, (k >> 1) & 1, k & 1
            peer = (x ^ bx, y ^ by, c ^ bc)
            cp = pltpu.make_async_remote_copy(src_ref=in_ref, dst_ref=gat.at[me], send_sem=send_sems.at[k - 1],
                                              recv_sem=recv_sems.at[k - 1], device_id=peer, device_id_type=MESH)
            cp.start()
            sends.append((cp, 4 * peer[0] + 2 * peer[1] + peer[2]))
        for k, (cp, slot) in enumerate(sends):
            pltpu.make_async_remote_copy(src_ref=in_ref, dst_ref=gat.at[slot], send_sem=send_sems.at[k], recv_sem=recv_sems.at[k],
                                         device_id=(x, y, c), device_id_type=MESH).wait_recv()
        for cp, _ in sends:
            cp.wait_send()
        acc = gat[0]
        for k in range(1, N_DEV):
            acc = acc + gat[k]
        out_ref[...] = acc

    vm = pl.BlockSpec(memory_space=pltpu.VMEM)
    return pl.pallas_call(
        body, name="allreduce_small", in_specs=[vm], out_specs=vm, out_shape=jax.ShapeDtypeStruct(buf.shape, F32),
        scratch_shapes=[pltpu.VMEM((N_DEV, rows, LANES), F32), pltpu.SemaphoreType.DMA((N_DEV - 1,)), pltpu.SemaphoreType.DMA((N_DEV - 1,))],
        compiler_params=_params(),
    )(buf)


def _as2d(a):
    return a.reshape(-1, a.shape[-1])


def _cast_bf16(a, name):
    a2 = _as2d(a)
    out = _rowwise(lambda v: v, [a2], [], [(a2.shape[1], BF16)], [], tm=512, name=name)[0]
    return out.reshape(a.shape)


def _assemble(gathered, axis):
    _, l, r, c = gathered.shape
    if axis == 1:
        return gathered.transpose(1, 0, 2, 3).reshape(l, N_CHIPS * r, c)
    return gathered.transpose(1, 2, 0, 3).reshape(l, r, N_CHIPS * c)


def _split_shards(full, axis):
    l, r, c = full.shape
    if axis == 1:
        return full.reshape(l, N_CHIPS, r // N_CHIPS, c).transpose(1, 0, 2, 3)
    return full.reshape(l, r, N_CHIPS, c // N_CHIPS).transpose(2, 0, 1, 3)


def _adamw_math(w, g, m, v):
    m2 = ADAM_B1 * m + (1.0 - ADAM_B1) * g
    v2 = ADAM_B2 * v + (1.0 - ADAM_B2) * (g * g)
    m_hat = m2 / (1.0 - ADAM_B1 ** ADAM_STEP)
    v_hat = v2 / (1.0 - ADAM_B2 ** ADAM_STEP)
    delta = -ADAM_LR * (m_hat / (jnp.sqrt(v_hat) + ADAM_EPS) + ADAM_WD * w)
    return delta, m2, v2


def _adamw_big(w, m, v, mine, other, core, name):
    shape = w.shape
    wd = shape[-1]
    h = mine.shape[0]
    tm = _tile(h, 256, 16)
    nh = h // tm

    def body(core_ref, w_ref, m_ref, v_ref, a_ref, b_ref, g_ref, d_ref, mo_ref, vo_ref):
        g = jnp.where(pl.program_id(0) // nh == core_ref[0], a_ref[...], b_ref[...])
        g_ref[...] = g
        d_ref[...], mo_ref[...], vo_ref[...] = _adamw_math(w_ref[...], g, m_ref[...], v_ref[...])

    rows = pl.BlockSpec((tm, wd), lambda i: (i, 0))
    half = pl.BlockSpec((tm, wd), lambda i: (i % nh, 0))
    outs = pl.pallas_call(
        body, name=name, grid=(2 * nh,),
        in_specs=[pl.BlockSpec(memory_space=pltpu.SMEM), rows, rows, rows, half, half], out_specs=[rows] * 4,
        out_shape=[jax.ShapeDtypeStruct((2 * h, wd), F32)] * 4,
        compiler_params=_params(("parallel",)),
    )(core, _as2d(w), _as2d(m), _as2d(v), mine, other)
    return [o.reshape(shape) for o in outs]


def _pair_add(s, o, which, name):
    _, h, c = o.shape
    tm = _tile(h, 256, 16)
    nb = h // tm

    def body(s_ref, o_ref, out_ref):
        out_ref[...] = (s_ref[...] + o_ref[...]).astype(BF16)

    mine = pl.BlockSpec((None, tm, c), lambda q, i: (q, which * nb + i, 0))
    theirs = pl.BlockSpec((None, tm, c), lambda q, i: (q, i, 0))
    return pl.pallas_call(
        body, name=name, grid=(N_CHIPS, nb), in_specs=[mine, theirs], out_specs=theirs,
        out_shape=jax.ShapeDtypeStruct(o.shape, BF16), compiler_params=_params(("parallel", "parallel")),
    )(s, o)


def _pair(s, o, core, name):
    return lax.cond(core == 0, lambda: _pair_add(s, o, 0, name + "_south"), lambda: _pair_add(s, o, 1, name + "_north"))


def _sum4(mine, recv, name):
    wd = mine.shape[-1]
    up = lambda v: v.astype(F32)
    return _rowwise(lambda a, b, c, d: ((up(a) + up(b)) + up(c)) + up(d), [mine, recv[0], recv[1], recv[2]], [], [(wd, F32)], [],
                    tm=256, name=name)[0]


def _pack(arrs):
    parts = []
    for a in arrs:
        f = a.reshape(-1).astype(F32)
        parts.append(jnp.pad(f, (0, (-f.shape[0]) % LANES)))
    flat = jnp.concatenate(parts)
    flat = jnp.pad(flat, (0, (-flat.shape[0]) % (8 * LANES)))
    return flat.reshape(-1, LANES)


def _unpack(buf, like):
    flat = buf.reshape(-1)
    out, off = [], 0
    for a in like:
        n = math.prod(a.shape)
        out.append(flat[off:off + n].reshape(a.shape))
        off += n + (-n) % LANES
    return out


def kernel(x, p, ffn1_norm, ffn1_w_gu, ffn1_w_down, mix_norm, ffn2_norm, ffn2_w_gu, ffn2_w_down, ple_norm, ple_w_gate, ple_w_proj, even_w_in, even_w_out, swa_sinks, rwkv_mu, rwkv_w0, rwkv_w2, rwkv_a0, rwkv_a2, rwkv_g2, rwkv_k_k, rwkv_k_a, rwkv_r_k, rwkv_ln_w, rwkv_ln_b, fox_w_in, fox_b_f, fox_w_out, final_norm, loss_target, m_ffn1_norm, m_ffn1_w_gu, m_ffn1_w_down, m_mix_norm, m_ffn2_norm, m_ffn2_w_gu, m_ffn2_w_down, m_ple_norm, m_ple_w_gate, m_ple_w_proj, m_even_w_in, m_even_w_out, m_swa_sinks, m_rwkv_mu, m_rwkv_w0, m_rwkv_w2, m_rwkv_a0, m_rwkv_a2, m_rwkv_g2, m_rwkv_k_k, m_rwkv_k_a, m_rwkv_r_k, m_rwkv_ln_w, m_rwkv_ln_b, m_fox_w_in, m_fox_b_f, m_fox_w_out, m_final_norm, v_ffn1_norm, v_ffn1_w_gu, v_ffn1_w_down, v_mix_norm, v_ffn2_norm, v_ffn2_w_gu, v_ffn2_w_down, v_ple_norm, v_ple_w_gate, v_ple_w_proj, v_even_w_in, v_even_w_out, v_swa_sinks, v_rwkv_mu, v_rwkv_w0, v_rwkv_w2, v_rwkv_a0, v_rwkv_a2, v_rwkv_g2, v_rwkv_k_k, v_rwkv_k_a, v_rwkv_r_k, v_rwkv_ln_w, v_rwkv_ln_b, v_fox_w_in, v_fox_b_f, v_fox_w_out, v_final_norm):
    args = locals()
    wts = {n: args[n] for n in WEIGHTS}
    mom = {n: args['m_' + n] for n in WEIGHTS}
    var = {n: args['v_' + n] for n in WEIGHTS}
    xs = x[0]
    tgt = loss_target[0]
    t, d = xs.shape
    depth = ffn1_norm.shape[0]
    rd = d // 2
    row = lambda a: a.reshape(1, -1)

    names = BIG + LORA
    chip = 2 * lax.axis_index("x") + lax.axis_index("y")
    cast = {n: _cast_bf16(wts[n], f"cast_{n}") for n in names}
    items = [(n, i) for n in names for i in range(wts[n].shape[0])]
    early = lambda n, i: (n in ('ffn1_w_gu', 'ffn1_w_down') and i == 0) or n in ('even_w_in', 'even_w_out') or n in LORA
    first = [it for it in items if early(*it)]
    later = [it for it in items if not early(*it)]
    full = {n: [None] * wts[n].shape[0] for n in names}

    def place(group, gathered):
        for (n, i), g in zip(group, gathered):
            g = lax.dynamic_update_index_in_dim(g, cast[n][i], chip, 0)
            full[n][i] = _assemble(g[:, None], SHARDED[n])[0]

    place(first, _gather_chips([cast[n][i] for n, i in first]))

    n_swa = d // (2 * HEAD)
    slopes = 2.0 ** (-8.0 * jnp.arange(1, n_swa + 1, dtype=F32) / n_swa)
    w2p, a2p = _lora_pad(full['rwkv_w2'][0].astype(F32), full['rwkv_a2'][0].astype(F32))
    rprm = {'mu': rwkv_mu, 'w0': rwkv_w0, 'a0': rwkv_a0, 'k_k': rwkv_k_k, 'k_a': rwkv_k_a, 'w2p': w2p, 'a2p': a2p,
            'g2': full['rwkv_g2'][0].astype(F32), 'ln_w': rwkv_ln_w, 'ln_b': rwkv_ln_b, 'r_k': rwkv_r_k.reshape(1, rd)}

    saved = []
    h = xs
    for i in range(depth):
        h, s1 = _ffn_fwd(h, row(ffn1_norm[i]), full['ffn1_w_gu'][i], full['ffn1_w_down'][i], f"l{i}_ffn1")
        if i % 2 == 0:
            h, sm, gathered = _even_fwd(h, row(mix_norm[i]), full['even_w_in'][i // 2], full['even_w_out'][i // 2],
                                        swa_sinks[i // 2], slopes, rprm, [cast[n][k] for n, k in later])
            place(later, gathered)
        else:
            h, sm = _odd_fwd(h, row(mix_norm[i]), full['fox_w_in'][i // 2], full['fox_w_out'][i // 2], row(fox_b_f[i // 2]))
        h, s2 = _ffn_fwd(h, row(ffn2_norm[i]), full['ffn2_w_gu'][i], full['ffn2_w_down'][i], f"l{i}_ffn2")
        h, sp = _ple_fwd(h, row(ple_norm[i]), full['ple_w_gate'][i], p[i, 0], full['ple_w_proj'][i], f"l{i}_ple")
        saved.append((s1, sm, s2, sp))
    dx, loss_tile, d_final = _final_loss(h, row(final_norm), tgt)

    core = lax.axis_index("c")
    gl = {n: [None] * depth for n in ['ffn1_norm', 'ffn1_w_gu', 'ffn1_w_down', 'mix_norm', 'ffn2_norm', 'ffn2_w_gu',
                                      'ffn2_w_down', 'ple_norm', 'ple_w_gate', 'ple_w_proj']}
    g1 = {}
    local = {}

    def settle(n):
        if n in gl:
            per_layer = gl[n]
            local[n] = jnp.stack(per_layer).reshape((depth,) + per_layer[0].shape[-2:]) if per_layer[0].shape[0] != 1 \
                else jnp.concatenate(per_layer, axis=0)
        else:
            g = g1[n]
            local[n] = g.reshape((1,) + g.shape) if g.ndim == 2 and wts[n].ndim == 3 else g

    def chip_partials(group):
        for n in group:
            settle(n)
        stacked = [_split_shards(local[n], SHARDED[n]) for n in group]
        stacked = [s.reshape(N_CHIPS, -1, s.shape[-1]) for s in stacked]
        from_sibling = _swap_halves(stacked)
        return [_pair(s, o, core, f"pair_{n}") for n, s, o in zip(group, stacked, from_sibling)]

    def own_halves(group, parts, received):
        mine = [lax.dynamic_index_in_dim(s, chip, axis=0, keepdims=False) for s in parts]
        return [_sum4(a, r, f"sum_{n}") for n, a, r in zip(group, mine, received)]

    early = [n for n in BIG if n.startswith(('ffn2_', 'ple_', 'fox_'))]
    late = [n for n in BIG + LORA if n not in early]
    for i in reversed(range(depth)):
        s1, sm, s2, sp = saved[i]
        dx, gl['ple_norm'][i], gl['ple_w_gate'][i], gl['ple_w_proj'][i] = _ple_bwd(
            dx, sp, row(ple_norm[i]), full['ple_w_gate'][i], p[i, 0], f"l{i}_ple")
        dx, gl['ffn2_norm'][i], gl['ffn2_w_gu'][i], gl['ffn2_w_down'][i] = _ffn_bwd(
            dx, s2, row(ffn2_norm[i]), full['ffn2_w_gu'][i], full['ffn2_w_down'][i], f"l{i}_ffn2")
        if i % 2 == 0:
            early_parts = chip_partials(early)
            dx, gl['mix_norm'][i], g1['even_w_in'], g1['even_w_out'], g1['swa_sinks'], rg, early_recv = _even_bwd(
                dx, sm, row(mix_norm[i]), full['even_w_in'][i // 2], full['even_w_out'][i // 2], swa_sinks[i // 2], slopes, rprm,
                early_parts)
            g1.update(rg)
        else:
            dx, gl['mix_norm'][i], g1['fox_w_in'], g1['fox_w_out'], g1['fox_b_f'] = _odd_bwd(
                dx, sm, row(mix_norm[i]), full['fox_w_in'][i // 2], full['fox_w_out'][i // 2], row(fox_b_f[i // 2]))
        dx, gl['ffn1_norm'][i], gl['ffn1_w_gu'][i], gl['ffn1_w_down'][i] = _ffn_bwd(
            dx, s1, row(ffn1_norm[i]), full['ffn1_w_gu'][i], full['ffn1_w_down'][i], f"l{i}_ffn1")
    grad_x = dx.reshape(x.shape)

    late_parts = chip_partials(late)
    halves = own_halves(early, early_parts, early_recv) + own_halves(late, late_parts, _scatter_chips(late_parts))
    others = _swap_cores(halves)
    core1 = core.astype(jnp.int32).reshape(1)
    out_g, out_d, out_m, out_v = {}, {}, {}, {}
    for n, a, o in zip(early + late, halves, others):
        out_g[n], out_d[n], out_m[n], out_v[n] = _adamw_big(wts[n], mom[n], var[n], a, o, core1, f"adamw_{n}")

    for n in SMALL:
        if n != 'final_norm':
            settle(n)
    local['final_norm'] = d_final
    small_like = [wts[n] for n in SMALL]
    packed = _pack([local[n] for n in SMALL] + [loss_tile[0, :1]])
    parts = _unpack(_allreduce_small(packed), small_like + [loss_tile[0, :1]])
    g_small = dict(zip(SMALL, parts[:len(SMALL)]))
    loss = parts[-1].reshape(())
    wp, gp, mp, vp = (_pack([src[n] for n in SMALL]) for src in (wts, g_small, mom, var))
    d_p, m_p, v_p = _rowwise(_adamw_math, [wp, gp, mp, vp], [], [(LANES, F32)] * 3, [], tm=512, name="adamw_small")
    out_g.update(g_small)
    out_d.update(zip(SMALL, _unpack(d_p, small_like)))
    out_m.update(zip(SMALL, _unpack(m_p, small_like)))
    out_v.update(zip(SMALL, _unpack(v_p, small_like)))

    fit = lambda dct: [dct[n].reshape(wts[n].shape) for n in WEIGHTS]
    return (loss, grad_x, *fit(out_g), *fit(out_d), *fit(out_m), *fit(out_v))
```

```python
import functools
import math

import jax
import jax.numpy as jnp
from jax import lax
from jax.experimental import pallas as pl
from jax.experimental.pallas import tpu as pltpu

F32 = jnp.float32
BF16 = jnp.bfloat16
MESH = pl.DeviceIdType.MESH

HEAD = 64
SWA_BLOCK = 128
SWA_GROUP = 4
DECAY_LORA = 64
ICLR_LORA = 64
GATE_LORA = 128
NORM_EPS = 1e-6
GN_EPS = 64e-5
L2_EPS = 1e-12
NEG = -1e30

ADAM_LR = 0.001
ADAM_B1 = 0.9
ADAM_B2 = 0.999
ADAM_EPS = 1e-08
ADAM_WD = 0.01
ADAM_STEP = 10

VMEM_LIMIT = 48 * 1024 * 1024
LANES = 128
SEG = 256
MM_TILE = 1408

WEIGHTS = ['ffn1_norm', 'ffn1_w_gu', 'ffn1_w_down', 'mix_norm', 'ffn2_norm', 'ffn2_w_gu', 'ffn2_w_down',
           'ple_norm', 'ple_w_gate', 'ple_w_proj', 'even_w_in', 'even_w_out', 'swa_sinks', 'rwkv_mu',
           'rwkv_w0', 'rwkv_w2', 'rwkv_a0', 'rwkv_a2', 'rwkv_g2', 'rwkv_k_k', 'rwkv_k_a', 'rwkv_r_k',
           'rwkv_ln_w', 'rwkv_ln_b', 'fox_w_in', 'fox_b_f', 'fox_w_out', 'final_norm']
SHARDED = {'ffn1_w_gu': 2, 'ffn1_w_down': 1, 'ffn2_w_gu': 2, 'ffn2_w_down': 1, 'ple_w_gate': 1,
           'ple_w_proj': 2, 'even_w_in': 2, 'even_w_out': 1, 'fox_w_in': 2, 'fox_w_out': 1,
           'rwkv_w2': 2, 'rwkv_a2': 2, 'rwkv_g2': 2}
LORA = ['rwkv_w2', 'rwkv_a2', 'rwkv_g2']
BIG = [n for n in WEIGHTS if n in SHARDED and n not in LORA]
SMALL = [n for n in WEIGHTS if n not in SHARDED]
N_CHIPS = 4
N_DEV = 8


def _tile(dim, target, align):
    best = None
    t = align
    while t <= min(dim, target):
        if dim % t == 0:
            best = t
        t += align
    return best if best is not None else dim


def _params(sem=None):
    return pltpu.CompilerParams(dimension_semantics=sem, vmem_limit_bytes=VMEM_LIMIT)


def _matmul(a, b, *, ta=False, tb=False, alpha=1.0, res=None, out_dtype=F32, name, norm=None, norm_bwd=None):
    if ta:
        kdim, m = a.shape
    else:
        m, kdim = a.shape
    if tb:
        n, kb = b.shape
    else:
        kb, n = b.shape
    assert kdim == kb, (a.shape, b.shape, ta, tb)
    tm = _tile(m, MM_TILE if norm_bwd is None else MM_TILE // 2, 128 if ta else 16)
    tn = _tile(n, MM_TILE, 128)
    tk = _tile(kdim, MM_TILE, 128)
    nk = kdim // tk
    if norm is not None:
        assert (tm == m) if ta else (tk == kdim), "the normalised tile must span whole feature rows"
    if norm_bwd is not None:
        assert tn == n and out_dtype == F32
    a_spec = pl.BlockSpec((tk, tm), lambda j, i, k: (k, i)) if ta else pl.BlockSpec((tm, tk), lambda j, i, k: (i, k))
    b_spec = pl.BlockSpec((tn, tk), lambda j, i, k: (j, k)) if tb else pl.BlockSpec((tk, tn), lambda j, i, k: (k, j))
    o_spec = pl.BlockSpec((tm, tn), lambda j, i, k: (i, j))
    whole = lambda arr: pl.BlockSpec(arr.shape, lambda j, i, k: (0, 0))
    dims = (((0 if ta else 1,), (1 if tb else 0,)), ((), ()))
    ins, in_specs = [a, b], [a_spec, b_spec]
    if norm is not None:
        ins.append(norm)
        in_specs.append(whole(norm))
    if res is not None:
        ins.append(res)
        in_specs.append(o_spec)
    if norm_bwd is not None:
        ins += list(norm_bwd)
        in_specs += [o_spec, whole(norm_bwd[1]), o_spec]
    n_in = len(ins)

    def body(*refs):
        a_ref, b_ref = refs[:2]
        rest = list(refs[2:n_in])
        outs = refs[n_in:]
        av = a_ref[...]
        if norm is not None:
            av = _rms_math(av, rest.pop(0)[...])
        prod = lax.dot_general(av.astype(BF16), b_ref[...].astype(BF16), dims, preferred_element_type=F32)

        def finish(acc):
            o = acc * alpha
            tail = list(rest)
            if res is not None:
                o = o + tail.pop(0)[...]
            if norm_bwd is None:
                outs[0][...] = o.astype(out_dtype)
                return
            x_ref, g_ref, dx_ref = tail
            dx, dg = _rms_bwd_math(x_ref[...], g_ref[...], o)
            outs[0][...] = dx_ref[...] + dx
            first = (pl.program_id(0) == 0) & (pl.program_id(1) == 0)

            @pl.when(first)
            def _():
                outs[1][...] = jnp.zeros_like(outs[1])
            outs[1][...] += dg

        if nk == 1:
            finish(prod)
        else:
            acc_ref = outs[-1]
            k = pl.program_id(2)

            @pl.when(k == 0)
            def _():
                acc_ref[...] = jnp.zeros_like(acc_ref)

            acc_ref[...] += prod
            pl.when(k == nk - 1)(lambda: finish(acc_ref[...]))

    out_specs, out_shape = [o_spec], [jax.ShapeDtypeStruct((m, n), out_dtype)]
    if norm_bwd is not None:
        out_specs.append(pl.BlockSpec((1, n), lambda j, i, k: (0, 0)))
        out_shape.append(jax.ShapeDtypeStruct((1, n), F32))
    sem = ("parallel", "parallel", "arbitrary") if norm_bwd is None else ("arbitrary",) * 3
    outs = pl.pallas_call(
        body, name=name, grid=(n // tn, m // tm, nk), in_specs=in_specs, out_specs=out_specs, out_shape=out_shape,
        scratch_shapes=[] if nk == 1 else [pltpu.VMEM((tm, tn), F32)],
        compiler_params=_params(sem),
    )(*ins)
    return outs[0] if norm_bwd is None else outs


def _rowwise(fn, tiled, full, tiled_out, acc_out, *, tm, name):
    rows = tiled[0].shape[0]
    tm = _tile(rows, tm, 16)
    nt, nf, no, na = len(tiled), len(full), len(tiled_out), len(acc_out)

    def body(*refs):
        ins = [r[...] for r in refs[:nt + nf]]
        outs = fn(*ins)
        if not isinstance(outs, (tuple, list)):
            outs = (outs,)
        assert len(outs) == no + na, (name, len(outs))
        for r, o in zip(refs[nt + nf:nt + nf + no], outs[:no]):
            r[...] = o.astype(r.dtype)
        if na:
            first = pl.program_id(0) == 0
            for r, o in zip(refs[nt + nf + no:], outs[no:]):
                @pl.when(first)
                def _(r=r):
                    r[...] = jnp.zeros_like(r)
                r[...] += o.astype(F32)

    def whole(shape):
        nd = len(shape)
        return pl.BlockSpec(tuple(shape), lambda i, nd=nd: (0,) * nd)

    in_specs = [pl.BlockSpec((tm, t.shape[1]), lambda i: (i, 0)) for t in tiled] + [whole(f.shape) for f in full]
    out_specs = [pl.BlockSpec((tm, w), lambda i: (i, 0)) for w, _ in tiled_out] + [whole(s) for s in acc_out]
    out_shape = [jax.ShapeDtypeStruct((rows, w), d) for w, d in tiled_out] + [jax.ShapeDtypeStruct(tuple(s), F32) for s in acc_out]
    res = pl.pallas_call(
        body, name=name, grid=(rows // tm,), in_specs=in_specs, out_specs=out_specs, out_shape=out_shape,
        compiler_params=_params(("arbitrary",) if na else ("parallel",)),
    )(*tiled, *full)
    return res


def _sigmoid(x):
    return 1.0 / (1.0 + jnp.exp(-x))


def _rms_math(x, g):
    return x * lax.rsqrt(jnp.mean(x * x, axis=-1, keepdims=True) + NORM_EPS) * g


def _rms_bwd_math(x, g, dh):
    rstd = lax.rsqrt(jnp.mean(x * x, axis=-1, keepdims=True) + NORM_EPS)
    xhat = x * rstd
    dxhat = dh * g
    dx = rstd * (dxhat - xhat * jnp.mean(dxhat * xhat, axis=-1, keepdims=True))
    dg = jnp.sum(dh * xhat, axis=0, keepdims=True)
    return dx, dg


def _swiglu_fwd(gu, name):
    f = gu.shape[1] // 2

    def fn(gu):
        g, u = gu[:, :f], gu[:, f:]
        return g * _sigmoid(g) * u
    return _rowwise(fn, [gu], [], [(f, BF16)], [], tm=256, name=name)[0]


def _swiglu_bwd(gu, dact, name):
    f = gu.shape[1] // 2

    def fn(gu, dact):
        g, u = gu[:, :f], gu[:, f:]
        dact = dact.astype(F32)
        s = _sigmoid(g)
        dg = dact * u * (s * (1.0 + g * (1.0 - s)))
        du = dact * (g * s)
        return jnp.concatenate([dg, du], axis=1)
    return _rowwise(fn, [gu, dact], [], [(2 * f, BF16)], [], tm=256, name=name)[0]


def _ffn_fwd(x, g, w_gu, w_down, tag):
    gu = _matmul(x, w_gu, norm=g, name=f"{tag}_gu")
    act = _swiglu_fwd(gu, f"{tag}_act")
    x2 = _matmul(act, w_down, alpha=0.5, res=x, name=f"{tag}_down")
    return x2, (x, gu, act)


def _ffn_bwd(dx2, saved, g, w_gu, w_down, tag):
    x, gu, act = saved
    dact = _matmul(dx2, w_down, tb=True, alpha=0.5, out_dtype=BF16, name=f"{tag}_dact")
    d_down = _matmul(act, dx2, ta=True, alpha=0.5, name=f"{tag}_dwdown")
    dgu = _swiglu_bwd(gu, dact, f"{tag}_dgu")
    d_gu = _matmul(x, dgu, ta=True, norm=g, name=f"{tag}_dwgu")
    dx, dg = _matmul(dgu, w_gu, tb=True, norm_bwd=(x, g, dx2), name=f"{tag}_dh")
    return dx, dg, d_gu, d_down


def _ple_fwd(x, g, w_gate, p, w_proj, tag):
    z = _matmul(x, w_gate, norm=g, name=f"{tag}_gate")
    pp = _matmul(p, w_proj, name=f"{tag}_proj")
    d = x.shape[1]
    x2 = _rowwise(lambda x, z, pp: x + _sigmoid(z) * pp, [x, z, pp], [], [(d, F32)], [], tm=512, name=f"{tag}_comb")[0]
    return x2, (x, z, pp)


def _ple_bwd(dx2, saved, g, w_gate, p, tag):
    x, z, pp = saved
    d = x.shape[1]

    def fn(dx2, z, pp):
        s = _sigmoid(z)
        return dx2 * pp * s * (1.0 - s), dx2 * s
    dz, dpp = _rowwise(fn, [dx2, z, pp], [], [(d, BF16), (d, BF16)], [], tm=512, name=f"{tag}_dcomb")
    d_gate = _matmul(x, dz, ta=True, norm=g, name=f"{tag}_dwgate")
    d_proj = _matmul(p, dpp, ta=True, name=f"{tag}_dwproj")
    dx, dg = _matmul(dz, w_gate, tb=True, norm_bwd=(x, g, dx2), name=f"{tag}_dh")
    return dx, dg, d_gate, d_proj


def _final_loss(x, g, tgt):
    d = x.shape[1]

    def fn(x, tgt, g):
        rstd = lax.rsqrt(jnp.mean(x * x, axis=-1, keepdims=True) + NORM_EPS)
        err = x * rstd * g - tgt
        loss = 0.5 * jnp.sum(jnp.mean(err * err, axis=-1, keepdims=True), axis=0, keepdims=True)
        dx, dg = _rms_bwd_math(x, g, err * (1.0 / d))
        return dx, jnp.zeros((8, LANES), F32) + loss, dg
    return _rowwise(fn, [x, tgt], [g], [(d, F32)], [(8, LANES), (1, d)], tm=256, name="final_loss")


def _swa_masks(n):
    qi = lax.broadcasted_iota(jnp.int32, (SWA_BLOCK, 2 * SWA_BLOCK), 0)
    ki = lax.broadcasted_iota(jnp.int32, (SWA_BLOCK, 2 * SWA_BLOCK), 1)
    dist = qi + SWA_BLOCK - ki
    valid = (dist >= 0) & (dist < SWA_BLOCK) & ((ki >= SWA_BLOCK) | (n > 0))
    return dist.astype(F32), valid


def _dot_nt(a, b):
    return lax.dot_general(a, b, (((1,), (1,)), ((), ())), preferred_element_type=F32)


def _dot_tn(a, b):
    return lax.dot_general(a, b, (((0,), (0,)), ((), ())), preferred_element_type=F32)


def _dot(a, b):
    return jnp.dot(a, b, preferred_element_type=F32)


def _swa_specs(kvh, t):
    nb = t // SWA_BLOCK
    q_spec = pl.BlockSpec((SWA_GROUP, SWA_BLOCK, HEAD), lambda h, n: (h, n, 0))
    cur = pl.BlockSpec((1, SWA_BLOCK, HEAD), lambda h, n: (h, n, 0))
    prev = pl.BlockSpec((1, SWA_BLOCK, HEAD), lambda h, n: (h, jnp.maximum(n - 1, 0), 0))
    smem = pl.BlockSpec(memory_space=pltpu.SMEM)
    stat = pl.BlockSpec((SWA_GROUP, SWA_BLOCK, 1), lambda h, n: (h, n, 0))
    return nb, q_spec, cur, prev, smem, stat


def _swa_fwd(q, k, v, sinks, slopes):
    nh, t, _ = q.shape
    kvh = nh // SWA_GROUP
    nb, q_spec, cur, prev, smem, stat = _swa_specs(kvh, t)
    scale = HEAD ** -0.5

    def body(q_ref, kp_ref, kc_ref, vp_ref, vc_ref, sink_ref, slope_ref, o_ref, lse_ref):
        hk, n = pl.program_id(0), pl.program_id(1)
        dist, valid = _swa_masks(n)
        kk = jnp.concatenate([kp_ref[0], kc_ref[0]], axis=0)
        vv = jnp.concatenate([vp_ref[0], vc_ref[0]], axis=0)
        for g in range(SWA_GROUP):
            h = hk * SWA_GROUP + g
            s = _dot_nt(q_ref[g], kk) * scale - slope_ref[h] * dist
            s = jnp.where(valid, s, NEG)
            m = jnp.maximum(jnp.max(s, axis=-1, keepdims=True), sink_ref[h])
            p = jnp.exp(s - m)
            den = jnp.sum(p, axis=-1, keepdims=True) + jnp.exp(sink_ref[h] - m)
            o_ref[g] = _dot(p.astype(BF16), vv) / den
            lse_ref[g] = m + jnp.log(den)

    return pl.pallas_call(
        body, name="swa_fwd", grid=(kvh, nb),
        in_specs=[q_spec, prev, cur, prev, cur, smem, smem],
        out_specs=[q_spec, stat],
        out_shape=[jax.ShapeDtypeStruct((nh, t, HEAD), F32), jax.ShapeDtypeStruct((nh, t, 1), F32)],
        compiler_params=_params(("parallel", "parallel")),
    )(q, k, k, v, v, sinks, slopes)


def _swa_bwd(q, k, v, sinks, slopes, o, lse, do):
    nh, t, _ = q.shape
    kvh = nh // SWA_GROUP
    nb, q_spec, cur, prev, smem, stat = _swa_specs(kvh, t)
    scale = HEAD ** -0.5
    kv2 = pl.BlockSpec((1, 1, 2 * SWA_BLOCK, HEAD), lambda h, n: (h, n, 0, 0))
    sk = pl.BlockSpec((1, 1, 8, LANES), lambda h, n: (h, n, 0, 0))

    def body(q_ref, kp_ref, kc_ref, vp_ref, vc_ref, sink_ref, slope_ref, o_ref, lse_ref, do_ref,
             dq_ref, dk_ref, dv_ref, ds_ref):
        hk, n = pl.program_id(0), pl.program_id(1)
        dist, valid = _swa_masks(n)
        kk = jnp.concatenate([kp_ref[0], kc_ref[0]], axis=0)
        vv = jnp.concatenate([vp_ref[0], vc_ref[0]], axis=0)
        dk = jnp.zeros((2 * SWA_BLOCK, HEAD), F32)
        dv = jnp.zeros((2 * SWA_BLOCK, HEAD), F32)
        row = lax.broadcasted_iota(jnp.int32, (8, LANES), 0)
        dsink = jnp.zeros((8, LANES), F32)
        for g in range(SWA_GROUP):
            h = hk * SWA_GROUP + g
            qg = q_ref[g]
            s = _dot_nt(qg, kk) * scale - slope_ref[h] * dist
            p = jnp.where(valid, jnp.exp(s - lse_ref[g]), 0.0)
            dog = do_ref[g]
            delta = jnp.sum(dog * o_ref[g], axis=-1, keepdims=True)
            dob = dog.astype(BF16)
            dv = dv + _dot_tn(p.astype(BF16), dob)
            dp = _dot_nt(dob, vv)
            dsc = (p * (dp - delta) * scale).astype(BF16)
            dq_ref[g] = _dot(dsc, kk)
            dk = dk + _dot_tn(dsc, qg)
            dsk = -jnp.sum(jnp.exp(sink_ref[h] - lse_ref[g]) * delta, axis=0, keepdims=True)
            dsink = dsink + jnp.where(row == g, dsk, 0.0)
        dk_ref[0, 0] = dk
        dv_ref[0, 0] = dv
        ds_ref[0, 0] = dsink

    return pl.pallas_call(
        body, name="swa_bwd", grid=(kvh, nb),
        in_specs=[q_spec, prev, cur, prev, cur, smem, smem, q_spec, stat, q_spec],
        out_specs=[q_spec, kv2, kv2, sk],
        out_shape=[jax.ShapeDtypeStruct((nh, t, HEAD), F32),
                   jax.ShapeDtypeStruct((kvh, nb, 2 * SWA_BLOCK, HEAD), F32),
                   jax.ShapeDtypeStruct((kvh, nb, 2 * SWA_BLOCK, HEAD), F32),
                   jax.ShapeDtypeStruct((kvh, nb, 8, LANES), F32)],
        compiler_params=_params(("parallel", "parallel")),
    )(q, k, k, v, v, sinks, slopes, o, lse, do)


def _heads(a):
    t, w = a.shape
    return a.reshape(t, w // HEAD, HEAD).transpose(1, 0, 2)


def _unheads(a):
    h, t, _ = a.shape
    return a.transpose(1, 0, 2).reshape(t, h * HEAD)


def _fold_kv(d2):
    kvh, nb = d2.shape[:2]
    own = d2[:, :, SWA_BLOCK:]
    prev = d2[:, :, :SWA_BLOCK]
    nxt = jnp.concatenate([prev[:, 1:], jnp.zeros_like(prev[:, :1])], axis=1)
    return (own + nxt).reshape(kvh, nb * SWA_BLOCK, HEAD)


FOX_BLOCK = 512
GATE_BLOCK = 256


def _tri3(tri, x):
    hi = x.astype(BF16)
    r1 = x - hi.astype(F32)
    mid = r1.astype(BF16)
    lo = (r1 - mid.astype(F32)).astype(BF16)
    return _dot(tri, hi) + _dot(tri, mid) + _dot(tri, lo)


def _fox_gate_fwd(fz, b_f):
    t, nh = fz.shape
    blk = _tile(t, GATE_BLOCK, 16)
    nblk = t // blk

    def body(fz_ref, b_ref, c_ref):
        ri = lax.broadcasted_iota(jnp.int32, (blk, blk), 0)
        ci = lax.broadcasted_iota(jnp.int32, (blk, blk), 1)
        tri = (ci <= ri).astype(BF16)

        def step(j, carry):
            rows = pl.ds(j * blk, blk)
            z = fz_ref[rows, :] + b_ref[...]
            lf = jnp.minimum(z, 0.0) - jnp.log(1.0 + jnp.exp(-jnp.abs(z)))
            c_ref[rows, :] = carry + _tri3(tri, lf)
            return carry + jnp.sum(lf, axis=0, keepdims=True)
        lax.fori_loop(0, nblk, step, jnp.zeros((1, nh), F32))

    return pl.pallas_call(body, name="fox_gate_fwd", out_shape=jax.ShapeDtypeStruct((t, nh), F32),
                          compiler_params=_params())(fz, b_f)


def _fox_gate_bwd(fz, b_f, dc):
    t, nh = fz.shape
    blk = _tile(t, GATE_BLOCK, 16)
    nblk = t // blk

    def body(fz_ref, b_ref, dc_ref, dfz_ref, db_ref):
        ri = lax.broadcasted_iota(jnp.int32, (blk, blk), 0)
        ci = lax.broadcasted_iota(jnp.int32, (blk, blk), 1)
        tri = (ci >= ri).astype(BF16)

        def step(i, carry):
            acc, db = carry
            rows = pl.ds((nblk - 1 - i) * blk, blk)
            d = dc_ref[rows, :]
            dlf = acc + _tri3(tri, d)
            z = fz_ref[rows, :] + b_ref[...]
            dz = dlf * _sigmoid(-z)
            dfz_ref[rows, :] = dz
            return acc + jnp.sum(d, axis=0, keepdims=True), db + jnp.sum(dz, axis=0, keepdims=True)
        _, db = lax.fori_loop(0, nblk, step, (jnp.zeros((1, nh), F32), jnp.zeros((1, nh), F32)))
        db_ref[...] = db

    return pl.pallas_call(body, name="fox_gate_bwd",
                          out_shape=[jax.ShapeDtypeStruct((t, nh), F32), jax.ShapeDtypeStruct((1, nh), F32)],
                          compiler_params=_params())(fz, b_f, dc)


LOG2E = 1.4426950408889634
FOX_QSCALE = HEAD ** -0.5 * LOG2E


def _lower_triangle(blk):
    return lax.broadcasted_iota(jnp.int32, (blk, blk), 1) <= lax.broadcasted_iota(jnp.int32, (blk, blk), 0)


def _fox_fwd(q2, k, v, c_row2):
    nh, t, _ = q2.shape
    blk = c_row2.shape[-1]
    nb = t // blk

    def body(q_ref, k_ref, v_ref, ck_ref, o_ref, lse_ref):
        qi = pl.program_id(1)
        q = q_ref[0]

        def step(j, carry, diagonal):
            m, l, acc = carry
            ks = pl.ds(j * blk, blk)
            s = _dot_nt(q, k_ref[0, ks, :]) - ck_ref[0, j]
            if diagonal:
                s = jnp.where(_lower_triangle(blk), s, NEG)
            m2 = jnp.maximum(m, jnp.max(s, axis=-1, keepdims=True))
            a = jnp.exp2(m - m2)
            p = jnp.exp2(s - m2)
            l = a * l + jnp.sum(p, axis=-1, keepdims=True)
            acc = a * acc + _dot(p.astype(BF16), v_ref[0, ks, :])
            return m2, l, acc
        init = (jnp.full((blk, 1), NEG, F32), jnp.zeros((blk, 1), F32), jnp.zeros((blk, HEAD), F32))
        carry = lax.fori_loop(0, qi, lambda j, c: step(j, c, False), init)
        m, l, acc = step(qi, carry, True)
        o_ref[0] = acc / l
        lse_ref[0] = m + jnp.log(l) * LOG2E

    qb = pl.BlockSpec((1, blk, HEAD), lambda h, i: (h, i, 0))
    full = pl.BlockSpec((1, t, HEAD), lambda h, i: (h, 0, 0))
    colb = pl.BlockSpec((1, blk, 1), lambda h, i: (h, i, 0))
    rowf = pl.BlockSpec((1, nb, 1, blk), lambda h, i: (h, 0, 0, 0))
    return pl.pallas_call(
        body, name="fox_fwd", grid=(nh, nb), in_specs=[qb, full, full, rowf], out_specs=[qb, colb],
        out_shape=[jax.ShapeDtypeStruct((nh, t, HEAD), F32), jax.ShapeDtypeStruct((nh, t, 1), F32)],
        compiler_params=_params(("parallel", "parallel")),
    )(q2, k, v, c_row2)


def _fox_bwd(q2, k, v, c_row2, lse2, o, do):
    nh, t, _ = q2.shape
    blk = c_row2.shape[-1]
    nb = t // blk
    scale = HEAD ** -0.5

    def body(q_ref, do_ref, lse_ref, o_ref, k_ref, v_ref, ck_ref, dq_ref, dk_ref, dv_ref, dc_ref, dcq_ref, dl_ref):
        kb = pl.program_id(1)

        @pl.when(kb == 0)
        def _():
            dq_ref[...] = jnp.zeros_like(dq_ref)
            dcq_ref[...] = jnp.zeros_like(dcq_ref)
            for i in range(nb):
                rs = pl.ds(i * blk, blk)
                dl_ref[rs, :] = jnp.sum(do_ref[0, rs, :].astype(F32) * o_ref[0, rs, :], axis=-1, keepdims=True)

        k = k_ref[0]
        v = v_ref[0]
        ck = ck_ref[0, 0]

        def step(i, carry, diagonal):
            dk, dv, dck = carry
            rs = pl.ds(i * blk, blk)
            q = q_ref[0, rs, :]
            do = do_ref[0, rs, :]
            p = jnp.exp2(_dot_nt(q, k) - ck - lse_ref[0, rs, :])
            if diagonal:
                p = jnp.where(_lower_triangle(blk), p, 0.0)
            dv = dv + _dot_tn(p.astype(BF16), do)
            ds = p * (_dot_nt(do, v) - dl_ref[rs, :])
            dck = dck - jnp.sum(ds, axis=0, keepdims=True)
            dcq_ref[0, rs, :] += jnp.sum(ds, axis=1, keepdims=True)
            dsb = ds.astype(BF16)
            dk = dk + _dot_tn(dsb, q)
            dq_ref[0, rs, :] += _dot(dsb, k) * scale
            return dk, dv, dck
        init = (jnp.zeros((blk, HEAD), F32), jnp.zeros((blk, HEAD), F32), jnp.zeros((1, blk), F32))
        carry = step(kb, init, True)
        dk, dv, dck = lax.fori_loop(kb + 1, nb, lambda i, c: step(i, c, False), carry)
        dk_ref[0] = dk * (1.0 / LOG2E)
        dv_ref[0] = dv
        dc_ref[0, 0] = dck

    full = pl.BlockSpec((1, t, HEAD), lambda h, j: (h, 0, 0))
    colf = pl.BlockSpec((1, t, 1), lambda h, j: (h, 0, 0))
    kb_spec = pl.BlockSpec((1, blk, HEAD), lambda h, j: (h, j, 0))
    rowb = pl.BlockSpec((1, 1, 1, blk), lambda h, j: (h, j, 0, 0))
    return pl.pallas_call(
        body, name="fox_bwd", grid=(nh, nb),
        in_specs=[full, full, colf, full, kb_spec, kb_spec, rowb],
        out_specs=[full, kb_spec, kb_spec, rowb, colf],
        scratch_shapes=[pltpu.VMEM((t, 1), F32)],
        out_shape=[jax.ShapeDtypeStruct((nh, t, HEAD), F32), jax.ShapeDtypeStruct((nh, t, HEAD), F32),
                   jax.ShapeDtypeStruct((nh, t, HEAD), F32), jax.ShapeDtypeStruct((nh, nb, 1, blk), F32),
                   jax.ShapeDtypeStruct((nh, t, 1), F32)],
        compiler_params=_params(("parallel", "arbitrary")),
    )(q2, do, lse2, o, k, v, c_row2)


def _split3(x):
    hi = x.astype(BF16)
    r1 = x - hi.astype(F32)
    mid = r1.astype(BF16)
    lo = (r1 - mid.astype(F32)).astype(BF16)
    return hi, mid, lo


def _segsum_raw(a, bm, parts=3):
    outs = []
    for s in range(a.shape[-1] // SEG):
        x = a[:, s * SEG:(s + 1) * SEG]
        if parts == 3:
            hi, mid, lo = _split3(x)
            outs.append(_dot(hi, bm) + _dot(mid, bm) + _dot(lo, bm))
        elif parts == 1:
            outs.append(_dot(x.astype(BF16), bm))
        else:
            hi = x.astype(BF16)
            lo = (x - hi.astype(F32)).astype(BF16)
            outs.append(_dot(hi, bm) + _dot(lo, bm))
    return outs[0] if len(outs) == 1 else jnp.concatenate(outs, axis=-1)


@jax.custom_vjp
def _segsum(a, bm):
    return _segsum_raw(a, bm)


def _segsum_f(a, bm):
    return _segsum_raw(a, bm), bm


def _segsum_b(bm, ct):
    return _segsum_raw(ct, bm), jnp.zeros_like(bm)


_segsum.defvjp(_segsum_f, _segsum_b)


@jax.custom_vjp
def _bdot(a, w):
    return _dot(a.astype(BF16), w.astype(BF16))


def _bdot_f(a, w):
    return _bdot(a, w), (a, w)


def _bdot_b(saved, ct):
    a, w = saved
    ctb = ct.astype(BF16)
    return _dot_nt(ctb, w.astype(BF16)), _dot_tn(a.astype(BF16), ctb)


_bdot.defvjp(_bdot_f, _bdot_b)


def _softplus(z):
    return jnp.maximum(z, 0.0) + jnp.log(1.0 + jnp.exp(-jnp.abs(z)))


def _rwkv_pre_math(hb, hbp, mu, w0, a0, k_k, k_a, w2p, a2p, g2, bm):
    rd = w0.shape[-1]
    m = hb + (hbp - hb) * mu
    r, k, v = m[:, :rd], m[:, rd:2 * rd], m[:, 2 * rd:3 * rd]
    xwa = m[:, 3 * rd:3 * rd + LANES]
    xg = m[:, 3 * rd + LANES:]
    wlog = -_softplus(-(w0 + _bdot(jnp.tanh(xwa), w2p))) - 0.5
    decay = jnp.exp(-jnp.exp(wlog))
    a = _sigmoid(a0 + _bdot(xwa, a2p))
    g = _bdot(_sigmoid(xg), g2)
    kk0 = k * k_k
    kk = kk0 / jnp.maximum(jnp.sqrt(_segsum(kk0 * kk0, bm)), L2_EPS)
    kp = k * (1.0 + (a - 1.0) * k_a)
    return r, decay, kp, v, kk, kk * a, g


def _rwkv_post_math(y, r, kp, v, g, ln_w, ln_b, r_k, bm):
    mean = _segsum(y, bm) * (1.0 / HEAD)
    yc = y - mean
    var = _segsum(yc * yc, bm) * (1.0 / HEAD)
    yn = yc * lax.rsqrt(var + GN_EPS) * ln_w + ln_b
    bonus = _segsum(r * kp * r_k, bm) * v
    return (yn + bonus) * g


def _rwkv_pre(hb, hbp, prm, bm):
    rd = prm[1].shape[-1]
    outs = [(rd, F32)] * 7
    return _rowwise(_rwkv_pre_math, [hb, hbp], list(prm) + [bm], outs, [], tm=256, name="rwkv_pre")


def _rwkv_pre_bwd(hb, hbp, cts, prm, bm):
    n_in = hb.shape[1]

    def fn(hb, hbp, *rest):
        ct, full = rest[:7], rest[7:]
        prm_v, bm_v = full[:-1], full[-1]
        _, vjp = jax.vjp(lambda hb, hbp, *p: _rwkv_pre_math(hb, hbp, *p, bm_v), hb, hbp, *prm_v)
        g = vjp(tuple(ct))
        return g
    acc = [p.shape for p in prm]
    res = _rowwise(fn, [hb, hbp] + list(cts), list(prm) + [bm], [(n_in, F32)] * 2, acc, tm=256, name="rwkv_pre_bwd")
    return res[:2], res[2:]


def _rwkv_post(y, r, kp, v, g, prm, bm):
    rd = y.shape[1]
    return _rowwise(_rwkv_post_math, [y, r, kp, v, g], list(prm) + [bm], [(rd, F32)], [], tm=256, name="rwkv_post")[0]


def _rwkv_post_bwd(y, r, kp, v, g, dout, prm, bm):
    rd = y.shape[1]

    def fn(y, r, kp, v, g, dout, ln_w, ln_b, r_k, bm_v):
        _, vjp = jax.vjp(lambda *a: _rwkv_post_math(*a, bm_v), y, r, kp, v, g, ln_w, ln_b, r_k)
        return vjp(dout)
    acc = [p.shape for p in prm]
    res = _rowwise(fn, [y, r, kp, v, g, dout], list(prm) + [bm], [(rd, F32)] * 5, acc, tm=256, name="rwkv_post_bwd")
    return res[:5], res[5:]


SCAN_FWD_CHUNK = 32
SCAN_BWD_CHUNK = 16
SCAN_PARTS = 2
READOUT_PARTS = 1
SPREAD_PARTS = 2


def _spread_rows(x, mk, bm):
    c, _, rd = x.shape
    hi = x.astype(BF16)
    keep = mk[None] != 0.0
    tile = lambda p: jnp.where(keep, p, jnp.zeros((), BF16)).reshape(c * HEAD, rd)
    parts = (tile(hi),) if SPREAD_PARTS == 1 else (tile(hi), tile((x - hi.astype(F32)).astype(BF16)))
    outs = [sum(_dot(p[:, s * SEG:(s + 1) * SEG], bm) for p in parts) for s in range(rd // SEG)]
    out = outs[0] if len(outs) == 1 else jnp.concatenate(outs, axis=-1)
    return out.reshape(c, HEAD, rd)


def _seg3d(x, bm, parts):
    c, n, rd = x.shape
    return _segsum_raw(x.reshape(c * n, rd), bm, parts).reshape(c, n, rd)


def _head_dots(x, bm):
    n, _, rd = x.shape
    y = _segsum_raw(jnp.broadcast_to(x, (n, 8, rd)).reshape(n * 8, rd), bm, 3).reshape(n, 8, rd)
    return jnp.sum(y, axis=1, keepdims=True) * 0.125


def _rwkv_scan_fwd(w, kk, b, k, v, r, bm, mk, side=()):
    t, _, rd = w.shape
    c = _tile(t, SCAN_FWD_CHUNK, 8)
    assert c % 2 == 0
    ns = len(side)
    steps = t // c

    def body(*refs):
        w_ref, kk_ref, b_ref, k_ref, v_ref, r_ref, bm_ref, mk_ref = refs[:8]
        side_in = refs[8:8 + ns]
        y_ref, s_ref = refs[8 + ns:10 + ns]
        side_out = refs[10 + ns:10 + 2 * ns]
        state, vb, beta, gamma = refs[10 + 2 * ns:14 + 2 * ns]
        if ns:
            start, forward, finish = _gather_phases(side, side_in, side_out, refs[14 + 2 * ns:])
            pl.when(pl.program_id(0) == 0)(start)
            pl.when(pl.program_id(0) == steps // 2)(forward)
            pl.when(pl.program_id(0) == steps - 1)(finish)

        @pl.when(pl.program_id(0) == 0)
        def _():
            state[...] = jnp.zeros_like(state)

        bmv = bm_ref[...]
        mkv = mk_ref[...]
        vb[...] = _spread_rows(v_ref[...], mkv, bmv)
        kk_next = kk_ref[pl.ds(1, c - 1)]
        beta[pl.ds(0, c - 1)] = _head_dots(b_ref[pl.ds(0, c - 1)] * kk_next, bmv)
        gamma[pl.ds(0, c - 1)] = _head_dots(k_ref[pl.ds(0, c - 1)] * kk_next, bmv)

        def pair(p, s):
            ia, ib = 2 * p, 2 * p + 1
            sk_a = _segsum_raw(s * kk_ref[ia], bmv, SCAN_PARTS)
            through = _segsum_raw(s * (w_ref[ia] * kk_ref[ib]), bmv, SCAN_PARTS)
            va = vb[ia]
            s = s * w_ref[ia] - sk_a * b_ref[ia] + va * k_ref[ia]
            s_ref[ia] = s
            sk_b = through - sk_a * beta[ia] + va * gamma[ia]
            s = s * w_ref[ib] - sk_b * b_ref[ib] + vb[ib] * k_ref[ib]
            s_ref[ib] = s
            return s
        state[...] = lax.fori_loop(0, c // 2, pair, state[...])
        yb = _seg3d(s_ref[...] * r_ref[...], bmv, READOUT_PARTS)
        y_ref[...] = jnp.sum(yb * mkv[None], axis=1, keepdims=True)

    vec = pl.BlockSpec((c, 1, rd), lambda i: (i, 0, 0))
    res = pl.pallas_call(
        body, name="rwkv_scan_fwd", grid=(steps,),
        in_specs=[vec] * 6 + [pl.BlockSpec((SEG, SEG), lambda i: (0, 0)), pl.BlockSpec((HEAD, rd), lambda i: (0, 0))]
        + [HBM_SPEC] * ns,
        out_specs=[vec, pl.BlockSpec((c, HEAD, rd), lambda i: (i, 0, 0))] + [HBM_SPEC] * ns,
        out_shape=[jax.ShapeDtypeStruct((t, 1, rd), F32), jax.ShapeDtypeStruct((t, HEAD, rd), F32)] + _gather_shapes(side),
        scratch_shapes=[pltpu.VMEM((HEAD, rd), F32), pltpu.VMEM((c, HEAD, rd), F32),
                        pltpu.VMEM((c, 1, rd), F32), pltpu.VMEM((c, 1, rd), F32)] + (_gather_sems(ns) if ns else []),
        compiler_params=_params(("arbitrary",)),
    )(w, kk, b, k, v, r, bm, mk, *side)
    return res[0], res[1], list(res[2:])


def _rwkv_scan_bwd(w, kk, b, k, v, r, dy, states, bm, mk, side=()):
    t, _, rd = w.shape
    c = _tile(t, SCAN_BWD_CHUNK, 8)
    nc = t // c
    assert c % 2 == 0
    ns = len(side)

    def body(*refs):
        w_ref, kk_ref, b_ref, k_ref, v_ref, r_ref, dy_ref, s_ref, sp_ref, bm_ref, mk_ref = refs[:11]
        dr_ref, dw_ref, dk_ref, dv_ref, dkk_ref, db_ref = refs[11 + ns:17 + ns]
        gstate, sp, vb, dyb, skb, gall, gball, delta, eps = refs[17 + 2 * ns:26 + 2 * ns]
        step_id = pl.program_id(0)
        if ns:
            start, finish = _scatter_phases(ns, refs[11:11 + ns], refs[17 + ns:17 + 2 * ns], refs[26 + 2 * ns:])
            pl.when(step_id == 0)(start)
            pl.when(step_id == nc - 1)(finish)

        @pl.when(step_id == 0)
        def _():
            gstate[...] = jnp.zeros_like(gstate)

        bmv = bm_ref[...]
        mkv = mk_ref[...]
        sp[0] = jnp.where(step_id == nc - 1, 0.0, sp_ref[0])
        sp[1:c] = s_ref[0:c - 1]
        vb[...] = _spread_rows(v_ref[...], mkv, bmv)
        dyb[...] = _spread_rows(dy_ref[...], mkv, bmv)
        skb[...] = _seg3d(sp[...] * kk_ref[...], bmv, READOUT_PARTS)
        dr_ref[...] = jnp.sum(s_ref[...] * dyb[...], axis=1, keepdims=True)
        delta[pl.ds(0, c - 1)] = _head_dots(kk_ref[pl.ds(1, c - 1)] * b_ref[pl.ds(0, c - 1)], bmv)
        eps[...] = _head_dots(r_ref[...] * b_ref[...], bmv)

        def pair(p, g):
            ib = c - 1 - 2 * p
            ia = ib - 1
            g = g + dyb[ib] * r_ref[ib]
            gall[ib] = g
            gb_b = _segsum_raw(g * b_ref[ib], bmv, SCAN_PARTS)
            through = _segsum_raw(g * (w_ref[ib] * b_ref[ia]), bmv, SCAN_PARTS)
            gball[ib] = gb_b
            dya = dyb[ia]
            g = g * w_ref[ib] - gb_b * kk_ref[ib] + dya * r_ref[ia]
            gall[ia] = g
            gb_a = through - gb_b * delta[ia] + dya * eps[ia]
            gball[ia] = gb_a
            return g * w_ref[ia] - gb_a * kk_ref[ia]
        gstate[...] = lax.fori_loop(0, c // 2, pair, gstate[...])
        ga = gall[...]
        dv_ref[...] = jnp.sum(_seg3d(ga * k_ref[...], bmv, READOUT_PARTS) * mkv[None], axis=1, keepdims=True)
        dk_ref[...] = jnp.sum(ga * vb[...], axis=1, keepdims=True)
        dw_ref[...] = jnp.sum(ga * sp[...], axis=1, keepdims=True)
        db_ref[...] = -jnp.sum(ga * skb[...], axis=1, keepdims=True)
        dkk_ref[...] = -jnp.sum(sp[...] * gball[...], axis=1, keepdims=True)

    vec = pl.BlockSpec((c, 1, rd), lambda i: (nc - 1 - i, 0, 0))
    st = pl.BlockSpec((c, HEAD, rd), lambda i: (nc - 1 - i, 0, 0))
    st_prev = pl.BlockSpec((1, HEAD, rd), lambda i: (jnp.maximum((nc - 1 - i) * c - 1, 0), 0, 0))
    big = pltpu.VMEM((c, HEAD, rd), F32)
    res = pl.pallas_call(
        body, name="rwkv_scan_bwd", grid=(nc,),
        in_specs=[vec] * 7 + [st, st_prev, pl.BlockSpec((SEG, SEG), lambda i: (0, 0)),
                              pl.BlockSpec((HEAD, rd), lambda i: (0, 0))] + [HBM_SPEC] * ns,
        out_specs=[vec] * 6 + [HBM_SPEC] * ns,
        out_shape=[jax.ShapeDtypeStruct((t, 1, rd), F32)] * 6 + _scatter_shapes(side),
        scratch_shapes=[pltpu.VMEM((HEAD, rd), F32), big, big, big, big, big, big,
                        pltpu.VMEM((c, 1, rd), F32), pltpu.VMEM((c, 1, rd), F32)] + (_scatter_sems(ns) if ns else []),
        compiler_params=_params(("arbitrary",)),
    )(w, kk, b, k, v, r, dy, states, states, bm, mk, *side)
    return res[:6], list(res[6:])


def _shift_down(a):
    return jnp.concatenate([jnp.zeros_like(a[:1]), a[:-1]], axis=0)


def _rwkv_consts(rd):
    i = jnp.arange(SEG) // HEAD
    bm = (i[:, None] == i[None, :]).astype(BF16)
    mk = (jnp.arange(HEAD)[:, None] == (jnp.arange(rd) % HEAD)[None, :]).astype(F32)
    return bm, mk


def _lora_pad(w2, a2):
    z = jnp.zeros_like(w2)
    return jnp.concatenate([w2, z], axis=0), jnp.concatenate([jnp.zeros_like(a2), a2], axis=0)


def _rwkv_fwd(hb, prm, side=()):
    rd = prm['w0'].shape[-1]
    t = hb.shape[0]
    bm, mk = _rwkv_consts(rd)
    hbp = _shift_down(hb)
    pre_prm = (prm['mu'], prm['w0'], prm['a0'], prm['k_k'], prm['k_a'], prm['w2p'], prm['a2p'], prm['g2'])
    r, w, kp, v, kk, b, g = _rwkv_pre(hb, hbp, pre_prm, bm)
    to3 = lambda a: a.reshape(t, 1, rd)
    y3, states, gathered = _rwkv_scan_fwd(to3(w), to3(kk), to3(b), to3(kp), to3(v), to3(r), bm, mk, side)
    y = y3.reshape(t, rd)
    post_prm = (prm['ln_w'], prm['ln_b'], prm['r_k'])
    out = _rwkv_post(y, r, kp, v, g, post_prm, bm)
    return out, (hb, hbp, r, w, kp, v, kk, b, g, y, states), gathered


def _rwkv_bwd(dout, saved, prm, side=()):
    hb, hbp, r, w, kp, v, kk, b, g, y, states = saved
    rd = prm['w0'].shape[-1]
    t = hb.shape[0]
    bm, mk = _rwkv_consts(rd)
    post_prm = (prm['ln_w'], prm['ln_b'], prm['r_k'])
    (dy, dr1, dkp1, dv1, dg), (d_ln_w, d_ln_b, d_r_k) = _rwkv_post_bwd(y, r, kp, v, g, dout, post_prm, bm)
    to3 = lambda a: a.reshape(t, 1, rd)
    (dr2, dw, dk2, dv2, dkk, db), received = _rwkv_scan_bwd(to3(w), to3(kk), to3(b), to3(kp), to3(v), to3(r), to3(dy), states,
                                                            bm, mk, side)
    to2 = lambda a: a.reshape(t, rd)
    cts = [dr1 + to2(dr2), to2(dw), dkp1 + to2(dk2), dv1 + to2(dv2), to2(dkk), to2(db), dg]
    pre_prm = (prm['mu'], prm['w0'], prm['a0'], prm['k_k'], prm['k_a'], prm['w2p'], prm['a2p'], prm['g2'])
    (dhb, dhbp), gp = _rwkv_pre_bwd(hb, hbp, cts, pre_prm, bm)
    dhb = dhb + jnp.concatenate([dhbp[1:], jnp.zeros_like(dhbp[:1])], axis=0)
    d_mu, d_w0, d_a0, d_k_k, d_k_a, d_w2p, d_a2p, d_g2 = gp
    grads = {'rwkv_mu': d_mu, 'rwkv_w0': d_w0, 'rwkv_a0': d_a0, 'rwkv_k_k': d_k_k, 'rwkv_k_a': d_k_a,
             'rwkv_w2': d_w2p[:DECAY_LORA], 'rwkv_a2': d_a2p[DECAY_LORA:], 'rwkv_g2': d_g2,
             'rwkv_ln_w': d_ln_w, 'rwkv_ln_b': d_ln_b, 'rwkv_r_k': d_r_k}
    return dhb, grads, received


def _even_fwd(x, g, w_in, w_out, sinks, slopes, rprm, side=()):
    d = x.shape[1]
    q_w, kv_w = d // 2, d // 8
    proj = _matmul(x, w_in, norm=g, name="even_in")
    qa, ka, va, hb = proj[:, :q_w], proj[:, q_w:q_w + kv_w], proj[:, q_w + kv_w:q_w + 2 * kv_w], proj[:, q_w + 2 * kv_w:]
    qh, kh, vh = _heads(qa).astype(BF16), _heads(ka).astype(BF16), _heads(va).astype(BF16)
    oa, lse = _swa_fwd(qh, kh, vh, sinks, slopes)
    yb, rsaved, gathered = _rwkv_fwd(hb, rprm, side)
    cat = jnp.concatenate([_unheads(oa), yb], axis=1)
    x2 = _matmul(cat, w_out, res=x, name="even_out")
    return x2, (x, qh, kh, vh, oa, lse, cat, rsaved), gathered


def _even_bwd(dx2, saved, g, w_in, w_out, sinks, slopes, rprm, side=()):
    x, qh, kh, vh, oa, lse, cat, rsaved = saved
    d = x.shape[1]
    dcat = _matmul(dx2, w_out, tb=True, name="even_dcat")
    d_out = _matmul(cat, dx2, ta=True, name="even_dwout")
    dya, dyb = dcat[:, :d // 2], dcat[:, d // 2:]
    dq, dk2, dv2, dsk = _swa_bwd(qh, kh, vh, sinks, slopes, oa, lse, _heads(dya))
    d_sinks = jnp.sum(dsk[:, :, :SWA_GROUP, 0], axis=1).reshape(1, -1)
    dhb, rgrads, received = _rwkv_bwd(dyb, rsaved, rprm, side)
    dproj = jnp.concatenate([_unheads(dq), _unheads(_fold_kv(dk2)), _unheads(_fold_kv(dv2)), dhb], axis=1)
    d_in = _matmul(x, dproj, ta=True, norm=g, name="even_dwin")
    dx, dg = _matmul(dproj, w_in, tb=True, norm_bwd=(x, g, dx2), name="even_dhn")
    return dx, dg, d_in, d_out, d_sinks, rgrads, received


def _odd_fwd(x, g, w_in, w_out, b_f):
    d = x.shape[1]
    t = x.shape[0]
    nh = d // HEAD
    qkv = _matmul(x, w_in[:, :3 * d], norm=g, name="odd_in")
    fz = _matmul(x, w_in[:, 3 * d:], norm=g, name="odd_fz")
    c = _fox_gate_fwd(fz, b_f)
    blk = _tile(t, FOX_BLOCK, 128)
    c_row = (c.T * LOG2E).reshape(nh, t // blk, 1, blk)
    qh = _heads(qkv[:, :d] * FOX_QSCALE).astype(BF16)
    kh, vh = (_heads(qkv[:, i * d:(i + 1) * d]).astype(BF16) for i in (1, 2))
    o, lse = _fox_fwd(qh, kh, vh, c_row)
    y = _unheads(o)
    x2 = _matmul(y, w_out, res=x, name="odd_out")
    return x2, (x, fz, qh, kh, vh, c_row, o, lse, y)


def _odd_bwd(dx2, saved, g, w_in, w_out, b_f):
    x, fz, qh, kh, vh, c_row, o, lse, y = saved
    d = x.shape[1]
    t = x.shape[0]
    nh = d // HEAD
    dy = _matmul(dx2, w_out, tb=True, name="odd_dy")
    d_out = _matmul(y, dx2, ta=True, name="odd_dwout")
    dq, dk, dv, dcr, dcc = _fox_bwd(qh, kh, vh, c_row, lse, o, _heads(dy).astype(BF16))
    dfz, d_bf = _fox_gate_bwd(fz, b_f, (dcr.reshape(nh, t) + dcc.reshape(nh, t)).T)
    dqkv = jnp.concatenate([_unheads(dq), _unheads(dk), _unheads(dv)], axis=1)
    d_in = jnp.concatenate([_matmul(x, dqkv, ta=True, norm=g, name="odd_dwin"),
                            _matmul(x, dfz, ta=True, norm=g, name="odd_dwin_fz")], axis=1)
    dhn_fz = _matmul(dfz, w_in[:, 3 * d:], tb=True, name="odd_dhn_fz")
    dx, dg = _matmul(dqkv, w_in[:, :3 * d], tb=True, res=dhn_fz, norm_bwd=(x, g, dx2), name="odd_dhn")
    return dx, dg, d_in, d_out, d_bf


def _place():
    return lax.axis_index("x"), lax.axis_index("y"), lax.axis_index("c")


def _other_chips(x, y):
    return [(1 - x, y), (x, 1 - y), (1 - x, 1 - y)]


HBM_SPEC = pl.BlockSpec(memory_space=pltpu.HBM)


def _rows(ref, which, h):
    return ref.at[pl.ds(which * h, h)]


def _gather_phases(shards, ins, outs, sems):
    ici_send, ici_recv, d2d_send, d2d_recv = sems
    x, y, c = _place()
    me = 2 * x + y
    sibling = (x, y, 1 - c)
    pairs = [(i, j, px, py) for i in range(len(shards)) for j, (px, py) in enumerate(_other_chips(x, y))]
    half = lambda i, ref, which: _rows(ref, which, shards[i].shape[0] // 2)

    def over_ici(i, j, px, py, slot):
        return pltpu.make_async_remote_copy(
            src_ref=half(i, ins[i], c), dst_ref=half(i, outs[i].at[slot], c), send_sem=ici_send.at[3 * i + j],
            recv_sem=ici_recv.at[3 * i + j], device_id=(px, py, c), device_id_type=MESH)

    def over_d2d(i, j, px, py, which):
        part = half(i, outs[i].at[2 * px + py], which)
        return pltpu.make_async_remote_copy(src_ref=part, dst_ref=part, send_sem=d2d_send.at[3 * i + j],
                                            recv_sem=d2d_recv.at[3 * i + j], device_id=sibling, device_id_type=MESH)

    def start():
        for i, j, px, py in pairs:
            over_ici(i, j, px, py, me).start()

    def forward():
        for i, j, px, py in pairs:
            over_ici(i, j, px, py, 2 * px + py).wait_recv()
            over_d2d(i, j, px, py, c).start()

    def finish():
        for i, j, px, py in pairs:
            over_d2d(i, j, px, py, 1 - c).wait_recv()
        for i, j, px, py in pairs:
            over_ici(i, j, px, py, me).wait_send()
            over_d2d(i, j, px, py, c).wait_send()

    return start, forward, finish


def _gather_sems(n):
    return [pltpu.SemaphoreType.DMA((3 * n,))] * 4


def _gather_shapes(shards):
    return [jax.ShapeDtypeStruct((N_CHIPS,) + s.shape, s.dtype) for s in shards]


def _gather_chips(shards):
    n = len(shards)

    def body(*refs):
        start, forward, finish = _gather_phases(shards, refs[:n], refs[n:2 * n], refs[2 * n:])
        start()
        forward()
        finish()

    return pl.pallas_call(
        body, name="gather_weights", in_specs=[HBM_SPEC] * n, out_specs=[HBM_SPEC] * n,
        out_shape=_gather_shapes(shards), scratch_shapes=_gather_sems(n),
    )(*shards)


def _swap_halves(stacked):
    n = len(stacked)
    halves = [s.shape[1] // 2 for s in stacked]

    def body(*refs):
        ins, outs = refs[:n], refs[n:2 * n]
        send_sems, recv_sems = refs[2 * n:]
        x, y, c = _place()
        sends = []
        for i in range(n):
            cp = pltpu.make_async_remote_copy(
                src_ref=ins[i].at[:, pl.ds((1 - c) * halves[i], halves[i])], dst_ref=outs[i], send_sem=send_sems.at[i],
                recv_sem=recv_sems.at[i], device_id=(x, y, 1 - c), device_id_type=MESH)
            cp.start()
            sends.append(cp)
        for cp in sends:
            cp.wait_recv()
        for cp in sends:
            cp.wait_send()

    return pl.pallas_call(
        body, name="swap_halves", in_specs=[HBM_SPEC] * n, out_specs=[HBM_SPEC] * n,
        out_shape=[jax.ShapeDtypeStruct((N_CHIPS, h) + s.shape[2:], s.dtype) for s, h in zip(stacked, halves)],
        scratch_shapes=[pltpu.SemaphoreType.DMA((n,)), pltpu.SemaphoreType.DMA((n,))],
    )(*stacked)


def _scatter_phases(n, ins, outs, sems):
    send_sems, recv_sems = sems
    x, y, c = _place()
    pairs = [(i, j, px, py) for i in range(n) for j, (px, py) in enumerate(_other_chips(x, y))]

    def copy(i, j, px, py):
        return pltpu.make_async_remote_copy(src_ref=ins[i].at[2 * px + py], dst_ref=outs[i].at[j], send_sem=send_sems.at[3 * i + j],
                                            recv_sem=recv_sems.at[3 * i + j], device_id=(px, py, c), device_id_type=MESH)

    def start():
        for p in pairs:
            copy(*p).start()

    def finish():
        for p in pairs:
            copy(*p).wait_recv()
        for p in pairs:
            copy(*p).wait_send()

    return start, finish


def _scatter_sems(n):
    return [pltpu.SemaphoreType.DMA((3 * n,))] * 2


def _scatter_shapes(stacked):
    return [jax.ShapeDtypeStruct((3,) + s.shape[1:], s.dtype) for s in stacked]


def _scatter_chips(stacked):
    n = len(stacked)

    def body(*refs):
        start, finish = _scatter_phases(n, refs[:n], refs[n:2 * n], refs[2 * n:])
        start()
        finish()

    return pl.pallas_call(
        body, name="scatter_grads", in_specs=[HBM_SPEC] * n, out_specs=[HBM_SPEC] * n,
        out_shape=_scatter_shapes(stacked), scratch_shapes=_scatter_sems(n),
    )(*stacked)


def _swap_cores(arrs):
    n = len(arrs)

    def body(*refs):
        ins, outs = refs[:n], refs[n:2 * n]
        send_sems, recv_sems = refs[2 * n:]
        x, y, c = _place()
        sends = []
        for i in range(n):
            cp = pltpu.make_async_remote_copy(src_ref=ins[i], dst_ref=outs[i], send_sem=send_sems.at[i], recv_sem=recv_sems.at[i],
                                              device_id=(x, y, 1 - c), device_id_type=MESH)
            cp.start()
            sends.append(cp)
        for cp in sends:
            cp.wait_recv()
        for cp in sends:
            cp.wait_send()

    sem = pltpu.SemaphoreType.DMA((n,))
    return pl.pallas_call(
        body, name="swap_cores", in_specs=[HBM_SPEC] * n, out_specs=[HBM_SPEC] * n,
        out_shape=[jax.ShapeDtypeStruct(s.shape, s.dtype) for s in arrs],
        scratch_shapes=[sem, sem],
    )(*arrs)


def _allreduce_small(buf):
    rows = buf.shape[0]

    def body(in_ref, out_ref, gat, send_sems, recv_sems):
        x, y, c = _place()
        me = 4 * x + 2 * y + c
        gat[me] = in_ref[...]
        sends = []
        for k in range(1, N_DEV):
            bx, by, bc = (k >> 2) & 1, (k >> 1) & 1, k & 1
            peer = (x ^ bx, y ^ by, c ^ bc)
            cp = pltpu.make_async_remote_copy(src_ref=in_ref, dst_ref=gat.at[me], send_sem=send_sems.at[k - 1],
                                              recv_sem=recv_sems.at[k - 1], device_id=peer, device_id_type=MESH)
            cp.start()
            sends.append((cp, 4 * peer[0] + 2 * peer[1] + peer[2]))
        for k, (cp, slot) in enumerate(sends):
            pltpu.make_async_remote_copy(src_ref=in_ref, dst_ref=gat.at[slot], send_sem=send_sems.at[k], recv_sem=recv_sems.at[k],
                                         device_id=(x, y, c), device_id_type=MESH).wait_recv()
        for cp, _ in sends:
            cp.wait_send()
        acc = gat[0]
        for k in range(1, N_DEV):
            acc = acc + gat[k]
        out_ref[...] = acc

    vm = pl.BlockSpec(memory_space=pltpu.VMEM)
    return pl.pallas_call(
        body, name="allreduce_small", in_specs=[vm], out_specs=vm, out_shape=jax.ShapeDtypeStruct(buf.shape, F32),
        scratch_shapes=[pltpu.VMEM((N_DEV, rows, LANES), F32), pltpu.SemaphoreType.DMA((N_DEV - 1,)), pltpu.SemaphoreType.DMA((N_DEV - 1,))],
        compiler_params=_params(),
    )(buf)


def _as2d(a):
    return a.reshape(-1, a.shape[-1])


def _cast_bf16(a, name):
    a2 = _as2d(a)
    out = _rowwise(lambda v: v, [a2], [], [(a2.shape[1], BF16)], [], tm=512, name=name)[0]
    return out.reshape(a.shape)


def _assemble(gathered, axis):
    _, l, r, c = gathered.shape
    if axis == 1:
        return gathered.transpose(1, 0, 2, 3).reshape(l, N_CHIPS * r, c)
    return gathered.transpose(1, 2, 0, 3).reshape(l, r, N_CHIPS * c)


def _split_shards(full, axis):
    l, r, c = full.shape
    if axis == 1:
        return full.reshape(l, N_CHIPS, r // N_CHIPS, c).transpose(1, 0, 2, 3)
    return full.reshape(l, r, N_CHIPS, c // N_CHIPS).transpose(2, 0, 1, 3)


def _adamw_math(w, g, m, v):
    m2 = ADAM_B1 * m + (1.0 - ADAM_B1) * g
    v2 = ADAM_B2 * v + (1.0 - ADAM_B2) * (g * g)
    m_hat = m2 / (1.0 - ADAM_B1 ** ADAM_STEP)
    v_hat = v2 / (1.0 - ADAM_B2 ** ADAM_STEP)
    delta = -ADAM_LR * (m_hat / (jnp.sqrt(v_hat) + ADAM_EPS) + ADAM_WD * w)
    return delta, m2, v2


def _adamw_big(w, m, v, mine, other, core, name):
    shape = w.shape
    wd = shape[-1]
    h = mine.shape[0]
    tm = _tile(h, 256, 16)
    nh = h // tm

    def body(core_ref, w_ref, m_ref, v_ref, a_ref, b_ref, g_ref, d_ref, mo_ref, vo_ref):
        g = jnp.where(pl.program_id(0) // nh == core_ref[0], a_ref[...], b_ref[...])
        g_ref[...] = g
        d_ref[...], mo_ref[...], vo_ref[...] = _adamw_math(w_ref[...], g, m_ref[...], v_ref[...])

    rows = pl.BlockSpec((tm, wd), lambda i: (i, 0))
    half = pl.BlockSpec((tm, wd), lambda i: (i % nh, 0))
    outs = pl.pallas_call(
        body, name=name, grid=(2 * nh,),
        in_specs=[pl.BlockSpec(memory_space=pltpu.SMEM), rows, rows, rows, half, half], out_specs=[rows] * 4,
        out_shape=[jax.ShapeDtypeStruct((2 * h, wd), F32)] * 4,
        compiler_params=_params(("parallel",)),
    )(core, _as2d(w), _as2d(m), _as2d(v), mine, other)
    return [o.reshape(shape) for o in outs]


def _pair_add(s, o, which, name):
    _, h, c = o.shape
    tm = _tile(h, 256, 16)
    nb = h // tm

    def body(s_ref, o_ref, out_ref):
        out_ref[...] = (s_ref[...] + o_ref[...]).astype(BF16)

    mine = pl.BlockSpec((None, tm, c), lambda q, i: (q, which * nb + i, 0))
    theirs = pl.BlockSpec((None, tm, c), lambda q, i: (q, i, 0))
    return pl.pallas_call(
        body, name=name, grid=(N_CHIPS, nb), in_specs=[mine, theirs], out_specs=theirs,
        out_shape=jax.ShapeDtypeStruct(o.shape, BF16), compiler_params=_params(("parallel", "parallel")),
    )(s, o)


def _pair(s, o, core, name):
    return lax.cond(core == 0, lambda: _pair_add(s, o, 0, name + "_south"), lambda: _pair_add(s, o, 1, name + "_north"))


def _sum4(mine, recv, name):
    wd = mine.shape[-1]
    up = lambda v: v.astype(F32)
    return _rowwise(lambda a, b, c, d: ((up(a) + up(b)) + up(c)) + up(d), [mine, recv[0], recv[1], recv[2]], [], [(wd, F32)], [],
                    tm=256, name=name)[0]


def _pack(arrs):
    parts = []
    for a in arrs:
        f = a.reshape(-1).astype(F32)
        parts.append(jnp.pad(f, (0, (-f.shape[0]) % LANES)))
    flat = jnp.concatenate(parts)
    flat = jnp.pad(flat, (0, (-flat.shape[0]) % (8 * LANES)))
    return flat.reshape(-1, LANES)


def _unpack(buf, like):
    flat = buf.reshape(-1)
    out, off = [], 0
    for a in like:
        n = math.prod(a.shape)
        out.append(flat[off:off + n].reshape(a.shape))
        off += n + (-n) % LANES
    return out


def kernel(x, p, ffn1_norm, ffn1_w_gu, ffn1_w_down, mix_norm, ffn2_norm, ffn2_w_gu, ffn2_w_down, ple_norm, ple_w_gate, ple_w_proj, even_w_in, even_w_out, swa_sinks, rwkv_mu, rwkv_w0, rwkv_w2, rwkv_a0, rwkv_a2, rwkv_g2, rwkv_k_k, rwkv_k_a, rwkv_r_k, rwkv_ln_w, rwkv_ln_b, fox_w_in, fox_b_f, fox_w_out, final_norm, loss_target, m_ffn1_norm, m_ffn1_w_gu, m_ffn1_w_down, m_mix_norm, m_ffn2_norm, m_ffn2_w_gu, m_ffn2_w_down, m_ple_norm, m_ple_w_gate, m_ple_w_proj, m_even_w_in, m_even_w_out, m_swa_sinks, m_rwkv_mu, m_rwkv_w0, m_rwkv_w2, m_rwkv_a0, m_rwkv_a2, m_rwkv_g2, m_rwkv_k_k, m_rwkv_k_a, m_rwkv_r_k, m_rwkv_ln_w, m_rwkv_ln_b, m_fox_w_in, m_fox_b_f, m_fox_w_out, m_final_norm, v_ffn1_norm, v_ffn1_w_gu, v_ffn1_w_down, v_mix_norm, v_ffn2_norm, v_ffn2_w_gu, v_ffn2_w_down, v_ple_norm, v_ple_w_gate, v_ple_w_proj, v_even_w_in, v_even_w_out, v_swa_sinks, v_rwkv_mu, v_rwkv_w0, v_rwkv_w2, v_rwkv_a0, v_rwkv_a2, v_rwkv_g2, v_rwkv_k_k, v_rwkv_k_a, v_rwkv_r_k, v_rwkv_ln_w, v_rwkv_ln_b, v_fox_w_in, v_fox_b_f, v_fox_w_out, v_final_norm):
    args = locals()
    wts = {n: args[n] for n in WEIGHTS}
    mom = {n: args['m_' + n] for n in WEIGHTS}
    var = {n: args['v_' + n] for n in WEIGHTS}
    xs = x[0]
    tgt = loss_target[0]
    t, d = xs.shape
    depth = ffn1_norm.shape[0]
    rd = d // 2
    row = lambda a: a.reshape(1, -1)

    names = BIG + LORA
    chip = 2 * lax.axis_index("x") + lax.axis_index("y")
    cast = {n: _cast_bf16(wts[n], f"cast_{n}") for n in names}
    items = [(n, i) for n in names for i in range(wts[n].shape[0])]
    early = lambda n, i: (n in ('ffn1_w_gu', 'ffn1_w_down') and i == 0) or n in ('even_w_in', 'even_w_out') or n in LORA
    first = [it for it in items if early(*it)]
    later = [it for it in items if not early(*it)]
    full = {n: [None] * wts[n].shape[0] for n in names}

    def place(group, gathered):
        for (n, i), g in zip(group, gathered):
            g = lax.dynamic_update_index_in_dim(g, cast[n][i], chip, 0)
            full[n][i] = _assemble(g[:, None], SHARDED[n])[0]

    place(first, _gather_chips([cast[n][i] for n, i in first]))

    n_swa = d // (2 * HEAD)
    slopes = 2.0 ** (-8.0 * jnp.arange(1, n_swa + 1, dtype=F32) / n_swa)
    w2p, a2p = _lora_pad(full['rwkv_w2'][0].astype(F32), full['rwkv_a2'][0].astype(F32))
    rprm = {'mu': rwkv_mu, 'w0': rwkv_w0, 'a0': rwkv_a0, 'k_k': rwkv_k_k, 'k_a': rwkv_k_a, 'w2p': w2p, 'a2p': a2p,
            'g2': full['rwkv_g2'][0].astype(F32), 'ln_w': rwkv_ln_w, 'ln_b': rwkv_ln_b, 'r_k': rwkv_r_k.reshape(1, rd)}

    saved = []
    h = xs
    for i in range(depth):
        h, s1 = _ffn_fwd(h, row(ffn1_norm[i]), full['ffn1_w_gu'][i], full['ffn1_w_down'][i], f"l{i}_ffn1")
        if i % 2 == 0:
            h, sm, gathered = _even_fwd(h, row(mix_norm[i]), full['even_w_in'][i // 2], full['even_w_out'][i // 2],
                                        swa_sinks[i // 2], slopes, rprm, [cast[n][k] for n, k in later])
            place(later, gathered)
        else:
            h, sm = _odd_fwd(h, row(mix_norm[i]), full['fox_w_in'][i // 2], full['fox_w_out'][i // 2], row(fox_b_f[i // 2]))
        h, s2 = _ffn_fwd(h, row(ffn2_norm[i]), full['ffn2_w_gu'][i], full['ffn2_w_down'][i], f"l{i}_ffn2")
        h, sp = _ple_fwd(h, row(ple_norm[i]), full['ple_w_gate'][i], p[i, 0], full['ple_w_proj'][i], f"l{i}_ple")
        saved.append((s1, sm, s2, sp))
    dx, loss_tile, d_final = _final_loss(h, row(final_norm), tgt)

    core = lax.axis_index("c")
    gl = {n: [None] * depth for n in ['ffn1_norm', 'ffn1_w_gu', 'ffn1_w_down', 'mix_norm', 'ffn2_norm', 'ffn2_w_gu',
                                      'ffn2_w_down', 'ple_norm', 'ple_w_gate', 'ple_w_proj']}
    g1 = {}
    local = {}

    def settle(n):
        if n in gl:
            per_layer = gl[n]
            local[n] = jnp.stack(per_layer).reshape((depth,) + per_layer[0].shape[-2:]) if per_layer[0].shape[0] != 1 \
                else jnp.concatenate(per_layer, axis=0)
        else:
            g = g1[n]
            local[n] = g.reshape((1,) + g.shape) if g.ndim == 2 and wts[n].ndim == 3 else g

    def chip_partials(group):
        for n in group:
            settle(n)
        stacked = [_split_shards(local[n], SHARDED[n]) for n in group]
        stacked = [s.reshape(N_CHIPS, -1, s.shape[-1]) for s in stacked]
        from_sibling = _swap_halves(stacked)
        return [_pair(s, o, core, f"pair_{n}") for n, s, o in zip(group, stacked, from_sibling)]

    def own_halves(group, parts, received):
        mine = [lax.dynamic_index_in_dim(s, chip, axis=0, keepdims=False) for s in parts]
        return [_sum4(a, r, f"sum_{n}") for n, a, r in zip(group, mine, received)]

    early = [n for n in BIG if n.startswith(('ffn2_', 'ple_', 'fox_'))]
    late = [n for n in BIG + LORA if n not in early]
    for i in reversed(range(depth)):
        s1, sm, s2, sp = saved[i]
        dx, gl['ple_norm'][i], gl['ple_w_gate'][i], gl['ple_w_proj'][i] = _ple_bwd(
            dx, sp, row(ple_norm[i]), full['ple_w_gate'][i], p[i, 0], f"l{i}_ple")
        dx, gl['ffn2_norm'][i], gl['ffn2_w_gu'][i], gl['ffn2_w_down'][i] = _ffn_bwd(
            dx, s2, row(ffn2_norm[i]), full['ffn2_w_gu'][i], full['ffn2_w_down'][i], f"l{i}_ffn2")
        if i % 2 == 0:
            early_parts = chip_partials(early)
            dx, gl['mix_norm'][i], g1['even_w_in'], g1['even_w_out'], g1['swa_sinks'], rg, early_recv = _even_bwd(
                dx, sm, row(mix_norm[i]), full['even_w_in'][i // 2], full['even_w_out'][i // 2], swa_sinks[i // 2], slopes, rprm,
                early_parts)
            g1.update(rg)
        else:
            dx, gl['mix_norm'][i], g1['fox_w_in'], g1['fox_w_out'], g1['fox_b_f'] = _odd_bwd(
                dx, sm, row(mix_norm[i]), full['fox_w_in'][i // 2], full['fox_w_out'][i // 2], row(fox_b_f[i // 2]))
        dx, gl['ffn1_norm'][i], gl['ffn1_w_gu'][i], gl['ffn1_w_down'][i] = _ffn_bwd(
            dx, s1, row(ffn1_norm[i]), full['ffn1_w_gu'][i], full['ffn1_w_down'][i], f"l{i}_ffn1")
    grad_x = dx.reshape(x.shape)

    late_parts = chip_partials(late)
    halves = own_halves(early, early_parts, early_recv) + own_halves(late, late_parts, _scatter_chips(late_parts))
    others = _swap_cores(halves)
    core1 = core.astype(jnp.int32).reshape(1)
    out_g, out_d, out_m, out_v = {}, {}, {}, {}
    for n, a, o in zip(early + late, halves, others):
        out_g[n], out_d[n], out_m[n], out_v[n] = _adamw_big(wts[n], mom[n], var[n], a, o, core1, f"adamw_{n}")

    for n in SMALL:
        if n != 'final_norm':
            settle(n)
    local['final_norm'] = d_final
    small_like = [wts[n] for n in SMALL]
    packed = _pack([local[n] for n in SMALL] + [loss_tile[0, :1]])
    parts = _unpack(_allreduce_small(packed), small_like + [loss_tile[0, :1]])
    g_small = dict(zip(SMALL, parts[:len(SMALL)]))
    loss = parts[-1].reshape(())
    wp, gp, mp, vp = (_pack([src[n] for n in SMALL]) for src in (wts, g_small, mom, var))
    d_p, m_p, v_p = _rowwise(_adamw_math, [wp, gp, mp, vp], [], [(LANES, F32)] * 3, [], tm=512, name="adamw_small")
    out_g.update(g_small)
    out_d.update(zip(SMALL, _unpack(d_p, small_like)))
    out_m.update(zip(SMALL, _unpack(m_p, small_like)))
    out_v.update(zip(SMALL, _unpack(v_p, small_like)))

    fit = lambda dct: [dct[n].reshape(wts[n].shape) for n in WEIGHTS]
    return (loss, grad_x, *fit(out_g), *fit(out_d), *fit(out_m), *fit(out_v))
```

```python
import functools
import math

import jax
import jax.numpy as jnp
from jax import lax
from jax.experimental import pallas as pl
from jax.experimental.pallas import tpu as pltpu

F32 = jnp.float32
BF16 = jnp.bfloat16
MESH = pl.DeviceIdType.MESH

HEAD = 64
SWA_BLOCK = 128
SWA_GROUP = 4
DECAY_LORA = 64
ICLR_LORA = 64
GATE_LORA = 128
NORM_EPS = 1e-6
GN_EPS = 64e-5
L2_EPS = 1e-12
NEG = -1e30

ADAM_LR = 0.001
ADAM_B1 = 0.9
ADAM_B2 = 0.999
ADAM_EPS = 1e-08
ADAM_WD = 0.01
ADAM_STEP = 10

VMEM_LIMIT = 48 * 1024 * 1024
LANES = 128
SEG = 256
MM_TILE = 1408

WEIGHTS = ['ffn1_norm', 'ffn1_w_gu', 'ffn1_w_down', 'mix_norm', 'ffn2_norm', 'ffn2_w_gu', 'ffn2_w_down',
           'ple_norm', 'ple_w_gate', 'ple_w_proj', 'even_w_in', 'even_w_out', 'swa_sinks', 'rwkv_mu',
           'rwkv_w0', 'rwkv_w2', 'rwkv_a0', 'rwkv_a2', 'rwkv_g2', 'rwkv_k_k', 'rwkv_k_a', 'rwkv_r_k',
           'rwkv_ln_w', 'rwkv_ln_b', 'fox_w_in', 'fox_b_f', 'fox_w_out', 'final_norm']
SHARDED = {'ffn1_w_gu': 2, 'ffn1_w_down': 1, 'ffn2_w_gu': 2, 'ffn2_w_down': 1, 'ple_w_gate': 1,
           'ple_w_proj': 2, 'even_w_in': 2, 'even_w_out': 1, 'fox_w_in': 2, 'fox_w_out': 1,
           'rwkv_w2': 2, 'rwkv_a2': 2, 'rwkv_g2': 2}
LORA = ['rwkv_w2', 'rwkv_a2', 'rwkv_g2']
BIG = [n for n in WEIGHTS if n in SHARDED and n not in LORA]
SMALL = [n for n in WEIGHTS if n not in SHARDED]
N_CHIPS = 4
N_DEV = 8


def _tile(dim, target, align):
    best = None
    t = align
    while t <= min(dim, target):
        if dim % t == 0:
            best = t
        t += align
    return best if best is not None else dim


def _params(sem=None):
    return pltpu.CompilerParams(dimension_semantics=sem, vmem_limit_bytes=VMEM_LIMIT)


def _matmul(a, b, *, ta=False, tb=False, alpha=1.0, res=None, out_dtype=F32, name, norm=None, norm_bwd=None):
    if ta:
        kdim, m = a.shape
    else:
        m, kdim = a.shape
    if tb:
        n, kb = b.shape
    else:
        kb, n = b.shape
    assert kdim == kb, (a.shape, b.shape, ta, tb)
    tm = _tile(m, MM_TILE if norm_bwd is None else MM_TILE // 2, 128 if ta else 16)
    tn = _tile(n, MM_TILE, 128)
    tk = _tile(kdim, MM_TILE, 128)
    nk = kdim // tk
    if norm is not None:
        assert (tm == m) if ta else (tk == kdim), "the normalised tile must span whole feature rows"
    if norm_bwd is not None:
        assert tn == n and out_dtype == F32
    a_spec = pl.BlockSpec((tk, tm), lambda j, i, k: (k, i)) if ta else pl.BlockSpec((tm, tk), lambda j, i, k: (i, k))
    b_spec = pl.BlockSpec((tn, tk), lambda j, i, k: (j, k)) if tb else pl.BlockSpec((tk, tn), lambda j, i, k: (k, j))
    o_spec = pl.BlockSpec((tm, tn), lambda j, i, k: (i, j))
    whole = lambda arr: pl.BlockSpec(arr.shape, lambda j, i, k: (0, 0))
    dims = (((0 if ta else 1,), (1 if tb else 0,)), ((), ()))
    ins, in_specs = [a, b], [a_spec, b_spec]
    if norm is not None:
        ins.append(norm)
        in_specs.append(whole(norm))
    if res is not None:
        ins.append(res)
        in_specs.append(o_spec)
    if norm_bwd is not None:
        ins += list(norm_bwd)
        in_specs += [o_spec, whole(norm_bwd[1]), o_spec]
    n_in = len(ins)

    def body(*refs):
        a_ref, b_ref = refs[:2]
        rest = list(refs[2:n_in])
        outs = refs[n_in:]
        av = a_ref[...]
        if norm is not None:
            av = _rms_math(av, rest.pop(0)[...])
        prod = lax.dot_general(av.astype(BF16), b_ref[...].astype(BF16), dims, preferred_element_type=F32)

        def finish(acc):
            o = acc * alpha
            tail = list(rest)
            if res is not None:
                o = o + tail.pop(0)[...]
            if norm_bwd is None:
                outs[0][...] = o.astype(out_dtype)
                return
            x_ref, g_ref, dx_ref = tail
            dx, dg = _rms_bwd_math(x_ref[...], g_ref[...], o)
            outs[0][...] = dx_ref[...] + dx
            first = (pl.program_id(0) == 0) & (pl.program_id(1) == 0)

            @pl.when(first)
            def _():
                outs[1][...] = jnp.zeros_like(outs[1])
            outs[1][...] += dg

        if nk == 1:
            finish(prod)
        else:
            acc_ref = outs[-1]
            k = pl.program_id(2)

            @pl.when(k == 0)
            def _():
                acc_ref[...] = jnp.zeros_like(acc_ref)

            acc_ref[...] += prod
            pl.when(k == nk - 1)(lambda: finish(acc_ref[...]))

    out_specs, out_shape = [o_spec], [jax.ShapeDtypeStruct((m, n), out_dtype)]
    if norm_bwd is not None:
        out_specs.append(pl.BlockSpec((1, n), lambda j, i, k: (0, 0)))
        out_shape.append(jax.ShapeDtypeStruct((1, n), F32))
    sem = ("parallel", "parallel", "arbitrary") if norm_bwd is None else ("arbitrary",) * 3
    outs = pl.pallas_call(
        body, name=name, grid=(n // tn, m // tm, nk), in_specs=in_specs, out_specs=out_specs, out_shape=out_shape,
        scratch_shapes=[] if nk == 1 else [pltpu.VMEM((tm, tn), F32)],
        compiler_params=_params(sem),
    )(*ins)
    return outs[0] if norm_bwd is None else outs


def _rowwise(fn, tiled, full, tiled_out, acc_out, *, tm, name):
    rows = tiled[0].shape[0]
    tm = _tile(rows, tm, 16)
    nt, nf, no, na = len(tiled), len(full), len(tiled_out), len(acc_out)

    def body(*refs):
        ins = [r[...] for r in refs[:nt + nf]]
        outs = fn(*ins)
        if not isinstance(outs, (tuple, list)):
            outs = (outs,)
        assert len(outs) == no + na, (name, len(outs))
        for r, o in zip(refs[nt + nf:nt + nf + no], outs[:no]):
            r[...] = o.astype(r.dtype)
        if na:
            first = pl.program_id(0) == 0
            for r, o in zip(refs[nt + nf + no:], outs[no:]):
                @pl.when(first)
                def _(r=r):
                    r[...] = jnp.zeros_like(r)
                r[...] += o.astype(F32)

    def whole(shape):
        nd = len(shape)
        return pl.BlockSpec(tuple(shape), lambda i, nd=nd: (0,) * nd)

    in_specs = [pl.BlockSpec((tm, t.shape[1]), lambda i: (i, 0)) for t in tiled] + [whole(f.shape) for f in full]
    out_specs = [pl.BlockSpec((tm, w), lambda i: (i, 0)) for w, _ in tiled_out] + [whole(s) for s in acc_out]
    out_shape = [jax.ShapeDtypeStruct((rows, w), d) for w, d in tiled_out] + [jax.ShapeDtypeStruct(tuple(s), F32) for s in acc_out]
    res = pl.pallas_call(
        body, name=name, grid=(rows // tm,), in_specs=in_specs, out_specs=out_specs, out_shape=out_shape,
        compiler_params=_params(("arbitrary",) if na else ("parallel",)),
    )(*tiled, *full)
    return res


def _sigmoid(x):
    return 1.0 / (1.0 + jnp.exp(-x))


def _rms_math(x, g):
    return x * lax.rsqrt(jnp.mean(x * x, axis=-1, keepdims=True) + NORM_EPS) * g


def _rms_bwd_math(x, g, dh):
    rstd = lax.rsqrt(jnp.mean(x * x, axis=-1, keepdims=True) + NORM_EPS)
    xhat = x * rstd
    dxhat = dh * g
    dx = rstd * (dxhat - xhat * jnp.mean(dxhat * xhat, axis=-1, keepdims=True))
    dg = jnp.sum(dh * xhat, axis=0, keepdims=True)
    return dx, dg


def _swiglu_fwd(gu, name):
    f = gu.shape[1] // 2

    def fn(gu):
        g, u = gu[:, :f], gu[:, f:]
        return g * _sigmoid(g) * u
    return _rowwise(fn, [gu], [], [(f, BF16)], [], tm=256, name=name)[0]


def _swiglu_bwd(gu, dact, name):
    f = gu.shape[1] // 2

    def fn(gu, dact):
        g, u = gu[:, :f], gu[:, f:]
        dact = dact.astype(F32)
        s = _sigmoid(g)
        dg = dact * u * (s * (1.0 + g * (1.0 - s)))
        du = dact * (g * s)
        return jnp.concatenate([dg, du], axis=1)
    return _rowwise(fn, [gu, dact], [], [(2 * f, BF16)], [], tm=256, name=name)[0]


def _ffn_fwd(x, g, w_gu, w_down, tag):
    gu = _matmul(x, w_gu, norm=g, name=f"{tag}_gu")
    act = _swiglu_fwd(gu, f"{tag}_act")
    x2 = _matmul(act, w_down, alpha=0.5, res=x, name=f"{tag}_down")
    return x2, (x, gu, act)


def _ffn_bwd(dx2, saved, g, w_gu, w_down, tag):
    x, gu, act = saved
    dact = _matmul(dx2, w_down, tb=True, alpha=0.5, out_dtype=BF16, name=f"{tag}_dact")
    d_down = _matmul(act, dx2, ta=True, alpha=0.5, name=f"{tag}_dwdown")
    dgu = _swiglu_bwd(gu, dact, f"{tag}_dgu")
    d_gu = _matmul(x, dgu, ta=True, norm=g, name=f"{tag}_dwgu")
    dx, dg = _matmul(dgu, w_gu, tb=True, norm_bwd=(x, g, dx2), name=f"{tag}_dh")
    return dx, dg, d_gu, d_down


def _ple_fwd(x, g, w_gate, p, w_proj, tag):
    z = _matmul(x, w_gate, norm=g, name=f"{tag}_gate")
    pp = _matmul(p, w_proj, name=f"{tag}_proj")
    d = x.shape[1]
    x2 = _rowwise(lambda x, z, pp: x + _sigmoid(z) * pp, [x, z, pp], [], [(d, F32)], [], tm=512, name=f"{tag}_comb")[0]
    return x2, (x, z, pp)


def _ple_bwd(dx2, saved, g, w_gate, p, tag):
    x, z, pp = saved
    d = x.shape[1]

    def fn(dx2, z, pp):
        s = _sigmoid(z)
        return dx2 * pp * s * (1.0 - s), dx2 * s
    dz, dpp = _rowwise(fn, [dx2, z, pp], [], [(d, BF16), (d, BF16)], [], tm=512, name=f"{tag}_dcomb")
    d_gate = _matmul(x, dz, ta=True, norm=g, name=f"{tag}_dwgate")
    d_proj = _matmul(p, dpp, ta=True, name=f"{tag}_dwproj")
    dx, dg = _matmul(dz, w_gate, tb=True, norm_bwd=(x, g, dx2), name=f"{tag}_dh")
    return dx, dg, d_gate, d_proj


def _final_loss(x, g, tgt):
    d = x.shape[1]

    def fn(x, tgt, g):
        rstd = lax.rsqrt(jnp.mean(x * x, axis=-1, keepdims=True) + NORM_EPS)
        err = x * rstd * g - tgt
        loss = 0.5 * jnp.sum(jnp.mean(err * err, axis=-1, keepdims=True), axis=0, keepdims=True)
        dx, dg = _rms_bwd_math(x, g, err * (1.0 / d))
        return dx, jnp.zeros((8, LANES), F32) + loss, dg
    return _rowwise(fn, [x, tgt], [g], [(d, F32)], [(8, LANES), (1, d)], tm=256, name="final_loss")


def _swa_masks(n):
    qi = lax.broadcasted_iota(jnp.int32, (SWA_BLOCK, 2 * SWA_BLOCK), 0)
    ki = lax.broadcasted_iota(jnp.int32, (SWA_BLOCK, 2 * SWA_BLOCK), 1)
    dist = qi + SWA_BLOCK - ki
    valid = (dist >= 0) & (dist < SWA_BLOCK) & ((ki >= SWA_BLOCK) | (n > 0))
    return dist.astype(F32), valid


def _dot_nt(a, b):
    return lax.dot_general(a, b, (((1,), (1,)), ((), ())), preferred_element_type=F32)


def _dot_tn(a, b):
    return lax.dot_general(a, b, (((0,), (0,)), ((), ())), preferred_element_type=F32)


def _dot(a, b):
    return jnp.dot(a, b, preferred_element_type=F32)


def _swa_specs(kvh, t):
    nb = t // SWA_BLOCK
    q_spec = pl.BlockSpec((SWA_GROUP, SWA_BLOCK, HEAD), lambda h, n: (h, n, 0))
    cur = pl.BlockSpec((1, SWA_BLOCK, HEAD), lambda h, n: (h, n, 0))
    prev = pl.BlockSpec((1, SWA_BLOCK, HEAD), lambda h, n: (h, jnp.maximum(n - 1, 0), 0))
    smem = pl.BlockSpec(memory_space=pltpu.SMEM)
    stat = pl.BlockSpec((SWA_GROUP, SWA_BLOCK, 1), lambda h, n: (h, n, 0))
    return nb, q_spec, cur, prev, smem, stat


def _swa_fwd(q, k, v, sinks, slopes):
    nh, t, _ = q.shape
    kvh = nh // SWA_GROUP
    nb, q_spec, cur, prev, smem, stat = _swa_specs(kvh, t)
    scale = HEAD ** -0.5

    def body(q_ref, kp_ref, kc_ref, vp_ref, vc_ref, sink_ref, slope_ref, o_ref, lse_ref):
        hk, n = pl.program_id(0), pl.program_id(1)
        dist, valid = _swa_masks(n)
        kk = jnp.concatenate([kp_ref[0], kc_ref[0]], axis=0)
        vv = jnp.concatenate([vp_ref[0], vc_ref[0]], axis=0)
        for g in range(SWA_GROUP):
            h = hk * SWA_GROUP + g
            s = _dot_nt(q_ref[g], kk) * scale - slope_ref[h] * dist
            s = jnp.where(valid, s, NEG)
            m = jnp.maximum(jnp.max(s, axis=-1, keepdims=True), sink_ref[h])
            p = jnp.exp(s - m)
            den = jnp.sum(p, axis=-1, keepdims=True) + jnp.exp(sink_ref[h] - m)
            o_ref[g] = _dot(p.astype(BF16), vv) / den
            lse_ref[g] = m + jnp.log(den)

    return pl.pallas_call(
        body, name="swa_fwd", grid=(kvh, nb),
        in_specs=[q_spec, prev, cur, prev, cur, smem, smem],
        out_specs=[q_spec, stat],
        out_shape=[jax.ShapeDtypeStruct((nh, t, HEAD), F32), jax.ShapeDtypeStruct((nh, t, 1), F32)],
        compiler_params=_params(("parallel", "parallel")),
    )(q, k, k, v, v, sinks, slopes)


def _swa_bwd(q, k, v, sinks, slopes, o, lse, do):
    nh, t, _ = q.shape
    kvh = nh // SWA_GROUP
    nb, q_spec, cur, prev, smem, stat = _swa_specs(kvh, t)
    scale = HEAD ** -0.5
    kv2 = pl.BlockSpec((1, 1, 2 * SWA_BLOCK, HEAD), lambda h, n: (h, n, 0, 0))
    sk = pl.BlockSpec((1, 1, 8, LANES), lambda h, n: (h, n, 0, 0))

    def body(q_ref, kp_ref, kc_ref, vp_ref, vc_ref, sink_ref, slope_ref, o_ref, lse_ref, do_ref,
             dq_ref, dk_ref, dv_ref, ds_ref):
        hk, n = pl.program_id(0), pl.program_id(1)
        dist, valid = _swa_masks(n)
        kk = jnp.concatenate([kp_ref[0], kc_ref[0]], axis=0)
        vv = jnp.concatenate([vp_ref[0], vc_ref[0]], axis=0)
        dk = jnp.zeros((2 * SWA_BLOCK, HEAD), F32)
        dv = jnp.zeros((2 * SWA_BLOCK, HEAD), F32)
        row = lax.broadcasted_iota(jnp.int32, (8, LANES), 0)
        dsink = jnp.zeros((8, LANES), F32)
        for g in range(SWA_GROUP):
            h = hk * SWA_GROUP + g
            qg = q_ref[g]
            s = _dot_nt(qg, kk) * scale - slope_ref[h] * dist
            p = jnp.where(valid, jnp.exp(s - lse_ref[g]), 0.0)
            dog = do_ref[g]
            delta = jnp.sum(dog * o_ref[g], axis=-1, keepdims=True)
            dob = dog.astype(BF16)
            dv = dv + _dot_tn(p.astype(BF16), dob)
            dp = _dot_nt(dob, vv)
            dsc = (p * (dp - delta) * scale).astype(BF16)
            dq_ref[g] = _dot(dsc, kk)
            dk = dk + _dot_tn(dsc, qg)
            dsk = -jnp.sum(jnp.exp(sink_ref[h] - lse_ref[g]) * delta, axis=0, keepdims=True)
            dsink = dsink + jnp.where(row == g, dsk, 0.0)
        dk_ref[0, 0] = dk
        dv_ref[0, 0] = dv
        ds_ref[0, 0] = dsink

    return pl.pallas_call(
        body, name="swa_bwd", grid=(kvh, nb),
        in_specs=[q_spec, prev, cur, prev, cur, smem, smem, q_spec, stat, q_spec],
        out_specs=[q_spec, kv2, kv2, sk],
        out_shape=[jax.ShapeDtypeStruct((nh, t, HEAD), F32),
                   jax.ShapeDtypeStruct((kvh, nb, 2 * SWA_BLOCK, HEAD), F32),
                   jax.ShapeDtypeStruct((kvh, nb, 2 * SWA_BLOCK, HEAD), F32),
                   jax.ShapeDtypeStruct((kvh, nb, 8, LANES), F32)],
        compiler_params=_params(("parallel", "parallel")),
    )(q, k, k, v, v, sinks, slopes, o, lse, do)


def _heads(a):
    t, w = a.shape
    return a.reshape(t, w // HEAD, HEAD).transpose(1, 0, 2)


def _unheads(a):
    h, t, _ = a.shape
    return a.transpose(1, 0, 2).reshape(t, h * HEAD)


def _fold_kv(d2):
    kvh, nb = d2.shape[:2]
    own = d2[:, :, SWA_BLOCK:]
    prev = d2[:, :, :SWA_BLOCK]
    nxt = jnp.concatenate([prev[:, 1:], jnp.zeros_like(prev[:, :1])], axis=1)
    return (own + nxt).reshape(kvh, nb * SWA_BLOCK, HEAD)


FOX_BLOCK = 512
GATE_BLOCK = 256


def _tri3(tri, x):
    hi = x.astype(BF16)
    r1 = x - hi.astype(F32)
    mid = r1.astype(BF16)
    lo = (r1 - mid.astype(F32)).astype(BF16)
    return _dot(tri, hi) + _dot(tri, mid) + _dot(tri, lo)


def _fox_gate_fwd(fz, b_f):
    t, nh = fz.shape
    blk = _tile(t, GATE_BLOCK, 16)
    nblk = t // blk

    def body(fz_ref, b_ref, c_ref):
        ri = lax.broadcasted_iota(jnp.int32, (blk, blk), 0)
        ci = lax.broadcasted_iota(jnp.int32, (blk, blk), 1)
        tri = (ci <= ri).astype(BF16)

        def step(j, carry):
            rows = pl.ds(j * blk, blk)
            z = fz_ref[rows, :] + b_ref[...]
            lf = jnp.minimum(z, 0.0) - jnp.log(1.0 + jnp.exp(-jnp.abs(z)))
            c_ref[rows, :] = carry + _tri3(tri, lf)
            return carry + jnp.sum(lf, axis=0, keepdims=True)
        lax.fori_loop(0, nblk, step, jnp.zeros((1, nh), F32))

    return pl.pallas_call(body, name="fox_gate_fwd", out_shape=jax.ShapeDtypeStruct((t, nh), F32),
                          compiler_params=_params())(fz, b_f)


def _fox_gate_bwd(fz, b_f, dc):
    t, nh = fz.shape
    blk = _tile(t, GATE_BLOCK, 16)
    nblk = t // blk

    def body(fz_ref, b_ref, dc_ref, dfz_ref, db_ref):
        ri = lax.broadcasted_iota(jnp.int32, (blk, blk), 0)
        ci = lax.broadcasted_iota(jnp.int32, (blk, blk), 1)
        tri = (ci >= ri).astype(BF16)

        def step(i, carry):
            acc, db = carry
            rows = pl.ds((nblk - 1 - i) * blk, blk)
            d = dc_ref[rows, :]
            dlf = acc + _tri3(tri, d)
            z = fz_ref[rows, :] + b_ref[...]
            dz = dlf * _sigmoid(-z)
            dfz_ref[rows, :] = dz
            return acc + jnp.sum(d, axis=0, keepdims=True), db + jnp.sum(dz, axis=0, keepdims=True)
        _, db = lax.fori_loop(0, nblk, step, (jnp.zeros((1, nh), F32), jnp.zeros((1, nh), F32)))
        db_ref[...] = db

    return pl.pallas_call(body, name="fox_gate_bwd",
                          out_shape=[jax.ShapeDtypeStruct((t, nh), F32), jax.ShapeDtypeStruct((1, nh), F32)],
                          compiler_params=_params())(fz, b_f, dc)


LOG2E = 1.4426950408889634
FOX_QSCALE = HEAD ** -0.5 * LOG2E


def _lower_triangle(blk):
    return lax.broadcasted_iota(jnp.int32, (blk, blk), 1) <= lax.broadcasted_iota(jnp.int32, (blk, blk), 0)


def _fox_fwd(q2, k, v, c_row2):
    nh, t, _ = q2.shape
    blk = c_row2.shape[-1]
    nb = t // blk

    def body(q_ref, k_ref, v_ref, ck_ref, o_ref, lse_ref):
        qi = pl.program_id(1)
        q = q_ref[0]

        def step(j, carry, diagonal):
            m, l, acc = carry
            ks = pl.ds(j * blk, blk)
            s = _dot_nt(q, k_ref[0, ks, :]) - ck_ref[0, j]
            if diagonal:
                s = jnp.where(_lower_triangle(blk), s, NEG)
            m2 = jnp.maximum(m, jnp.max(s, axis=-1, keepdims=True))
            a = jnp.exp2(m - m2)
            p = jnp.exp2(s - m2)
            l = a * l + jnp.sum(p, axis=-1, keepdims=True)
            acc = a * acc + _dot(p.astype(BF16), v_ref[0, ks, :])
            return m2, l, acc
        init = (jnp.full((blk, 1), NEG, F32), jnp.zeros((blk, 1), F32), jnp.zeros((blk, HEAD), F32))
        carry = lax.fori_loop(0, qi, lambda j, c: step(j, c, False), init)
        m, l, acc = step(qi, carry, True)
        o_ref[0] = acc / l
        lse_ref[0] = m + jnp.log(l) * LOG2E

    qb = pl.BlockSpec((1, blk, HEAD), lambda h, i: (h, i, 0))
    full = pl.BlockSpec((1, t, HEAD), lambda h, i: (h, 0, 0))
    colb = pl.BlockSpec((1, blk, 1), lambda h, i: (h, i, 0))
    rowf = pl.BlockSpec((1, nb, 1, blk), lambda h, i: (h, 0, 0, 0))
    return pl.pallas_call(
        body, name="fox_fwd", grid=(nh, nb), in_specs=[qb, full, full, rowf], out_specs=[qb, colb],
        out_shape=[jax.ShapeDtypeStruct((nh, t, HEAD), F32), jax.ShapeDtypeStruct((nh, t, 1), F32)],
        compiler_params=_params(("parallel", "parallel")),
    )(q2, k, v, c_row2)


def _fox_bwd(q2, k, v, c_row2, lse2, o, do):
    nh, t, _ = q2.shape
    blk = c_row2.shape[-1]
    nb = t // blk
    scale = HEAD ** -0.5

    def body(q_ref, do_ref, lse_ref, o_ref, k_ref, v_ref, ck_ref, dq_ref, dk_ref, dv_ref, dc_ref, dcq_ref, dl_ref):
        kb = pl.program_id(1)

        @pl.when(kb == 0)
        def _():
            dq_ref[...] = jnp.zeros_like(dq_ref)
            dcq_ref[...] = jnp.zeros_like(dcq_ref)
            for i in range(nb):
                rs = pl.ds(i * blk, blk)
                dl_ref[rs, :] = jnp.sum(do_ref[0, rs, :].astype(F32) * o_ref[0, rs, :], axis=-1, keepdims=True)

        k = k_ref[0]
        v = v_ref[0]
        ck = ck_ref[0, 0]

        def step(i, carry, diagonal):
            dk, dv, dck = carry
            rs = pl.ds(i * blk, blk)
            q = q_ref[0, rs, :]
            do = do_ref[0, rs, :]
            p = jnp.exp2(_dot_nt(q, k) - ck - lse_ref[0, rs, :])
            if diagonal:
                p = jnp.where(_lower_triangle(blk), p, 0.0)
            dv = dv + _dot_tn(p.astype(BF16), do)
            ds = p * (_dot_nt(do, v) - dl_ref[rs, :])
            dck = dck - jnp.sum(ds, axis=0, keepdims=True)
            dcq_ref[0, rs, :] += jnp.sum(ds, axis=1, keepdims=True)
            dsb = ds.astype(BF16)
            dk = dk + _dot_tn(dsb, q)
            dq_ref[0, rs, :] += _dot(dsb, k) * scale
            return dk, dv, dck
        init = (jnp.zeros((blk, HEAD), F32), jnp.zeros((blk, HEAD), F32), jnp.zeros((1, blk), F32))
        carry = step(kb, init, True)
        dk, dv, dck = lax.fori_loop(kb + 1, nb, lambda i, c: step(i, c, False), carry)
        dk_ref[0] = dk * (1.0 / LOG2E)
        dv_ref[0] = dv
        dc_ref[0, 0] = dck

    full = pl.BlockSpec((1, t, HEAD), lambda h, j: (h, 0, 0))
    colf = pl.BlockSpec((1, t, 1), lambda h, j: (h, 0, 0))
    kb_spec = pl.BlockSpec((1, blk, HEAD), lambda h, j: (h, j, 0))
    rowb = pl.BlockSpec((1, 1, 1, blk), lambda h, j: (h, j, 0, 0))
    return pl.pallas_call(
        body, name="fox_bwd", grid=(nh, nb),
        in_specs=[full, full, colf, full, kb_spec, kb_spec, rowb],
        out_specs=[full, kb_spec, kb_spec, rowb, colf],
        scratch_shapes=[pltpu.VMEM((t, 1), F32)],
        out_shape=[jax.ShapeDtypeStruct((nh, t, HEAD), F32), jax.ShapeDtypeStruct((nh, t, HEAD), F32),
                   jax.ShapeDtypeStruct((nh, t, HEAD), F32), jax.ShapeDtypeStruct((nh, nb, 1, blk), F32),
                   jax.ShapeDtypeStruct((nh, t, 1), F32)],
        compiler_params=_params(("parallel", "arbitrary")),
    )(q2, do, lse2, o, k, v, c_row2)


def _split3(x):
    hi = x.astype(BF16)
    r1 = x - hi.astype(F32)
    mid = r1.astype(BF16)
    lo = (r1 - mid.astype(F32)).astype(BF16)
    return hi, mid, lo


def _segsum_raw(a, bm, parts=3):
    outs = []
    for s in range(a.shape[-1] // SEG):
        x = a[:, s * SEG:(s + 1) * SEG]
        if parts == 3:
            hi, mid, lo = _split3(x)
            outs.append(_dot(hi, bm) + _dot(mid, bm) + _dot(lo, bm))
        elif parts == 1:
            outs.append(_dot(x.astype(BF16), bm))
        else:
            hi = x.astype(BF16)
            lo = (x - hi.astype(F32)).astype(BF16)
            outs.append(_dot(hi, bm) + _dot(lo, bm))
    return outs[0] if len(outs) == 1 else jnp.concatenate(outs, axis=-1)


@jax.custom_vjp
def _segsum(a, bm):
    return _segsum_raw(a, bm)


def _segsum_f(a, bm):
    return _segsum_raw(a, bm), bm


def _segsum_b(bm, ct):
    return _segsum_raw(ct, bm), jnp.zeros_like(bm)


_segsum.defvjp(_segsum_f, _segsum_b)


@jax.custom_vjp
def _bdot(a, w):
    return _dot(a.astype(BF16), w.astype(BF16))


def _bdot_f(a, w):
    return _bdot(a, w), (a, w)


def _bdot_b(saved, ct):
    a, w = saved
    ctb = ct.astype(BF16)
    return _dot_nt(ctb, w.astype(BF16)), _dot_tn(a.astype(BF16), ctb)


_bdot.defvjp(_bdot_f, _bdot_b)


def _softplus(z):
    return jnp.maximum(z, 0.0) + jnp.log(1.0 + jnp.exp(-jnp.abs(z)))


def _rwkv_pre_math(hb, hbp, mu, w0, a0, k_k, k_a, w2p, a2p, g2, bm):
    rd = w0.shape[-1]
    m = hb + (hbp - hb) * mu
    r, k, v = m[:, :rd], m[:, rd:2 * rd], m[:, 2 * rd:3 * rd]
    xwa = m[:, 3 * rd:3 * rd + LANES]
    xg = m[:, 3 * rd + LANES:]
    wlog = -_softplus(-(w0 + _bdot(jnp.tanh(xwa), w2p))) - 0.5
    decay = jnp.exp(-jnp.exp(wlog))
    a = _sigmoid(a0 + _bdot(xwa, a2p))
    g = _bdot(_sigmoid(xg), g2)
    kk0 = k * k_k
    kk = kk0 / jnp.maximum(jnp.sqrt(_segsum(kk0 * kk0, bm)), L2_EPS)
    kp = k * (1.0 + (a - 1.0) * k_a)
    return r, decay, kp, v, kk, kk * a, g


def _rwkv_post_math(y, r, kp, v, g, ln_w, ln_b, r_k, bm):
    mean = _segsum(y, bm) * (1.0 / HEAD)
    yc = y - mean
    var = _segsum(yc * yc, bm) * (1.0 / HEAD)
    yn = yc * lax.rsqrt(var + GN_EPS) * ln_w + ln_b
    bonus = _segsum(r * kp * r_k, bm) * v
    return (yn + bonus) * g


def _rwkv_pre(hb, hbp, prm, bm):
    rd = prm[1].shape[-1]
    outs = [(rd, F32)] * 7
    return _rowwise(_rwkv_pre_math, [hb, hbp], list(prm) + [bm], outs, [], tm=256, name="rwkv_pre")


def _rwkv_pre_bwd(hb, hbp, cts, prm, bm):
    n_in = hb.shape[1]

    def fn(hb, hbp, *rest):
        ct, full = rest[:7], rest[7:]
        prm_v, bm_v = full[:-1], full[-1]
        _, vjp = jax.vjp(lambda hb, hbp, *p: _rwkv_pre_math(hb, hbp, *p, bm_v), hb, hbp, *prm_v)
        g = vjp(tuple(ct))
        return g
    acc = [p.shape for p in prm]
    res = _rowwise(fn, [hb, hbp] + list(cts), list(prm) + [bm], [(n_in, F32)] * 2, acc, tm=256, name="rwkv_pre_bwd")
    return res[:2], res[2:]


def _rwkv_post(y, r, kp, v, g, prm, bm):
    rd = y.shape[1]
    return _rowwise(_rwkv_post_math, [y, r, kp, v, g], list(prm) + [bm], [(rd, F32)], [], tm=256, name="rwkv_post")[0]


def _rwkv_post_bwd(y, r, kp, v, g, dout, prm, bm):
    rd = y.shape[1]

    def fn(y, r, kp, v, g, dout, ln_w, ln_b, r_k, bm_v):
        _, vjp = jax.vjp(lambda *a: _rwkv_post_math(*a, bm_v), y, r, kp, v, g, ln_w, ln_b, r_k)
        return vjp(dout)
    acc = [p.shape for p in prm]
    res = _rowwise(fn, [y, r, kp, v, g, dout], list(prm) + [bm], [(rd, F32)] * 5, acc, tm=256, name="rwkv_post_bwd")
    return res[:5], res[5:]


SCAN_FWD_CHUNK = 32
SCAN_BWD_CHUNK = 16
SCAN_PARTS = 2
READOUT_PARTS = 1
SPREAD_PARTS = 2


def _spread_rows(x, mk, bm):
    c, _, rd = x.shape
    hi = x.astype(BF16)
    keep = mk[None] != 0.0
    tile = lambda p: jnp.where(keep, p, jnp.zeros((), BF16)).reshape(c * HEAD, rd)
    parts = (tile(hi),) if SPREAD_PARTS == 1 else (tile(hi), tile((x - hi.astype(F32)).astype(BF16)))
    outs = [sum(_dot(p[:, s * SEG:(s + 1) * SEG], bm) for p in parts) for s in range(rd // SEG)]
    out = outs[0] if len(outs) == 1 else jnp.concatenate(outs, axis=-1)
    return out.reshape(c, HEAD, rd)


def _seg3d(x, bm, parts):
    c, n, rd = x.shape
    return _segsum_raw(x.reshape(c * n, rd), bm, parts).reshape(c, n, rd)


def _head_dots(x, bm):
    n, _, rd = x.shape
    y = _segsum_raw(jnp.broadcast_to(x, (n, 8, rd)).reshape(n * 8, rd), bm, 3).reshape(n, 8, rd)
    return jnp.sum(y, axis=1, keepdims=True) * 0.125


def _rwkv_scan_fwd(w, kk, b, k, v, r, bm, mk, side=()):
    t, _, rd = w.shape
    c = _tile(t, SCAN_FWD_CHUNK, 8)
    assert c % 2 == 0
    ns = len(side)
    steps = t // c

    def body(*refs):
        w_ref, kk_ref, b_ref, k_ref, v_ref, r_ref, bm_ref, mk_ref = refs[:8]
        side_in = refs[8:8 + ns]
        y_ref, s_ref = refs[8 + ns:10 + ns]
        side_out = refs[10 + ns:10 + 2 * ns]
        state, vb, beta, gamma = refs[10 + 2 * ns:14 + 2 * ns]
        if ns:
            start, forward, finish = _gather_phases(side, side_in, side_out, refs[14 + 2 * ns:])
            pl.when(pl.program_id(0) == 0)(start)
            pl.when(pl.program_id(0) == steps // 2)(forward)
            pl.when(pl.program_id(0) == steps - 1)(finish)

        @pl.when(pl.program_id(0) == 0)
        def _():
            state[...] = jnp.zeros_like(state)

        bmv = bm_ref[...]
        mkv = mk_ref[...]
        vb[...] = _spread_rows(v_ref[...], mkv, bmv)
        kk_next = kk_ref[pl.ds(1, c - 1)]
        beta[pl.ds(0, c - 1)] = _head_dots(b_ref[pl.ds(0, c - 1)] * kk_next, bmv)
        gamma[pl.ds(0, c - 1)] = _head_dots(k_ref[pl.ds(0, c - 1)] * kk_next, bmv)

        def pair(p, s):
            ia, ib = 2 * p, 2 * p + 1
            sk_a = _segsum_raw(s * kk_ref[ia], bmv, SCAN_PARTS)
            through = _segsum_raw(s * (w_ref[ia] * kk_ref[ib]), bmv, SCAN_PARTS)
            va = vb[ia]
            s = s * w_ref[ia] - sk_a * b_ref[ia] + va * k_ref[ia]
            s_ref[ia] = s
            sk_b = through - sk_a * beta[ia] + va * gamma[ia]
            s = s * w_ref[ib] - sk_b * b_ref[ib] + vb[ib] * k_ref[ib]
            s_ref[ib] = s
            return s
        state[...] = lax.fori_loop(0, c // 2, pair, state[...])
        yb = _seg3d(s_ref[...] * r_ref[...], bmv, READOUT_PARTS)
        y_ref[...] = jnp.sum(yb * mkv[None], axis=1, keepdims=True)

    vec = pl.BlockSpec((c, 1, rd), lambda i: (i, 0, 0))
    res = pl.pallas_call(
        body, name="rwkv_scan_fwd", grid=(steps,),
        in_specs=[vec] * 6 + [pl.BlockSpec((SEG, SEG), lambda i: (0, 0)), pl.BlockSpec((HEAD, rd), lambda i: (0, 0))]
        + [HBM_SPEC] * ns,
        out_specs=[vec, pl.BlockSpec((c, HEAD, rd), lambda i: (i, 0, 0))] + [HBM_SPEC] * ns,
        out_shape=[jax.ShapeDtypeStruct((t, 1, rd), F32), jax.ShapeDtypeStruct((t, HEAD, rd), F32)] + _gather_shapes(side),
        scratch_shapes=[pltpu.VMEM((HEAD, rd), F32), pltpu.VMEM((c, HEAD, rd), F32),
                        pltpu.VMEM((c, 1, rd), F32), pltpu.VMEM((c, 1, rd), F32)] + (_gather_sems(ns) if ns else []),
        compiler_params=_params(("arbitrary",)),
    )(w, kk, b, k, v, r, bm, mk, *side)
    return res[0], res[1], list(res[2:])


def _rwkv_scan_bwd(w, kk, b, k, v, r, dy, states, bm, mk, side=()):
    t, _, rd = w.shape
    c = _tile(t, SCAN_BWD_CHUNK, 8)
    nc = t // c
    assert c % 2 == 0
    ns = len(side)

    def body(*refs):
        w_ref, kk_ref, b_ref, k_ref, v_ref, r_ref, dy_ref, s_ref, sp_ref, bm_ref, mk_ref = refs[:11]
        dr_ref, dw_ref, dk_ref, dv_ref, dkk_ref, db_ref = refs[11 + ns:17 + ns]
        gstate, sp, vb, dyb, skb, gall, gball, delta, eps = refs[17 + 2 * ns:26 + 2 * ns]
        step_id = pl.program_id(0)
        if ns:
            start, finish = _scatter_phases(ns, refs[11:11 + ns], refs[17 + ns:17 + 2 * ns], refs[26 + 2 * ns:])
            pl.when(step_id == 0)(start)
            pl.when(step_id == nc - 1)(finish)

        @pl.when(step_id == 0)
        def _():
            gstate[...] = jnp.zeros_like(gstate)

        bmv = bm_ref[...]
        mkv = mk_ref[...]
        sp[0] = jnp.where(step_id == nc - 1, 0.0, sp_ref[0])
        sp[1:c] = s_ref[0:c - 1]
        vb[...] = _spread_rows(v_ref[...], mkv, bmv)
        dyb[...] = _spread_rows(dy_ref[...], mkv, bmv)
        skb[...] = _seg3d(sp[...] * kk_ref[...], bmv, READOUT_PARTS)
        dr_ref[...] = jnp.sum(s_ref[...] * dyb[...], axis=1, keepdims=True)
        delta[pl.ds(0, c - 1)] = _head_dots(kk_ref[pl.ds(1, c - 1)] * b_ref[pl.ds(0, c - 1)], bmv)
        eps[...] = _head_dots(r_ref[...] * b_ref[...], bmv)

        def pair(p, g):
            ib = c - 1 - 2 * p
            ia = ib - 1
            g = g + dyb[ib] * r_ref[ib]
            gall[ib] = g
            gb_b = _segsum_raw(g * b_ref[ib], bmv, SCAN_PARTS)
            through = _segsum_raw(g * (w_ref[ib] * b_ref[ia]), bmv, SCAN_PARTS)
            gball[ib] = gb_b
            dya = dyb[ia]
            g = g * w_ref[ib] - gb_b * kk_ref[ib] + dya * r_ref[ia]
            gall[ia] = g
            gb_a = through - gb_b * delta[ia] + dya * eps[ia]
            gball[ia] = gb_a
            return g * w_ref[ia] - gb_a * kk_ref[ia]
        gstate[...] = lax.fori_loop(0, c // 2, pair, gstate[...])
        ga = gall[...]
        dv_ref[...] = jnp.sum(_seg3d(ga * k_ref[...], bmv, READOUT_PARTS) * mkv[None], axis=1, keepdims=True)
        dk_ref[...] = jnp.sum(ga * vb[...], axis=1, keepdims=True)
        dw_ref[...] = jnp.sum(ga * sp[...], axis=1, keepdims=True)
        db_ref[...] = -jnp.sum(ga * skb[...], axis=1, keepdims=True)
        dkk_ref[...] = -jnp.sum(sp[...] * gball[...], axis=1, keepdims=True)

    vec = pl.BlockSpec((c, 1, rd), lambda i: (nc - 1 - i, 0, 0))
    st = pl.BlockSpec((c, HEAD, rd), lambda i: (nc - 1 - i, 0, 0))
    st_prev = pl.BlockSpec((1, HEAD, rd), lambda i: (jnp.maximum((nc - 1 - i) * c - 1, 0), 0, 0))
    big = pltpu.VMEM((c, HEAD, rd), F32)
    res = pl.pallas_call(
        body, name="rwkv_scan_bwd", grid=(nc,),
        in_specs=[vec] * 7 + [st, st_prev, pl.BlockSpec((SEG, SEG), lambda i: (0, 0)),
                              pl.BlockSpec((HEAD, rd), lambda i: (0, 0))] + [HBM_SPEC] * ns,
        out_specs=[vec] * 6 + [HBM_SPEC] * ns,
        out_shape=[jax.ShapeDtypeStruct((t, 1, rd), F32)] * 6 + _scatter_shapes(side),
        scratch_shapes=[pltpu.VMEM((HEAD, rd), F32), big, big, big, big, big, big,
                        pltpu.VMEM((c, 1, rd), F32), pltpu.VMEM((c, 1, rd), F32)] + (_scatter_sems(ns) if ns else []),
        compiler_params=_params(("arbitrary",)),
    )(w, kk, b, k, v, r, dy, states, states, bm, mk, *side)
    return res[:6], list(res[6:])


def _shift_down(a):
    return jnp.concatenate([jnp.zeros_like(a[:1]), a[:-1]], axis=0)


def _rwkv_consts(rd):
    i = jnp.arange(SEG) // HEAD
    bm = (i[:, None] == i[None, :]).astype(BF16)
    mk = (jnp.arange(HEAD)[:, None] == (jnp.arange(rd) % HEAD)[None, :]).astype(F32)
    return bm, mk


def _lora_pad(w2, a2):
    z = jnp.zeros_like(w2)
    return jnp.concatenate([w2, z], axis=0), jnp.concatenate([jnp.zeros_like(a2), a2], axis=0)


def _rwkv_fwd(hb, prm, side=()):
    rd = prm['w0'].shape[-1]
    t = hb.shape[0]
    bm, mk = _rwkv_consts(rd)
    hbp = _shift_down(hb)
    pre_prm = (prm['mu'], prm['w0'], prm['a0'], prm['k_k'], prm['k_a'], prm['w2p'], prm['a2p'], prm['g2'])
    r, w, kp, v, kk, b, g = _rwkv_pre(hb, hbp, pre_prm, bm)
    to3 = lambda a: a.reshape(t, 1, rd)
    y3, states, gathered = _rwkv_scan_fwd(to3(w), to3(kk), to3(b), to3(kp), to3(v), to3(r), bm, mk, side)
    y = y3.reshape(t, rd)
    post_prm = (prm['ln_w'], prm['ln_b'], prm['r_k'])
    out = _rwkv_post(y, r, kp, v, g, post_prm, bm)
    return out, (hb, hbp, r, w, kp, v, kk, b, g, y, states), gathered


def _rwkv_bwd(dout, saved, prm, side=()):
    hb, hbp, r, w, kp, v, kk, b, g, y, states = saved
    rd = prm['w0'].shape[-1]
    t = hb.shape[0]
    bm, mk = _rwkv_consts(rd)
    post_prm = (prm['ln_w'], prm['ln_b'], prm['r_k'])
    (dy, dr1, dkp1, dv1, dg), (d_ln_w, d_ln_b, d_r_k) = _rwkv_post_bwd(y, r, kp, v, g, dout, post_prm, bm)
    to3 = lambda a: a.reshape(t, 1, rd)
    (dr2, dw, dk2, dv2, dkk, db), received = _rwkv_scan_bwd(to3(w), to3(kk), to3(b), to3(kp), to3(v), to3(r), to3(dy), states,
                                                            bm, mk, side)
    to2 = lambda a: a.reshape(t, rd)
    cts = [dr1 + to2(dr2), to2(dw), dkp1 + to2(dk2), dv1 + to2(dv2), to2(dkk), to2(db), dg]
    pre_prm = (prm['mu'], prm['w0'], prm['a0'], prm['k_k'], prm['k_a'], prm['w2p'], prm['a2p'], prm['g2'])
    (dhb, dhbp), gp = _rwkv_pre_bwd(hb, hbp, cts, pre_prm, bm)
    dhb = dhb + jnp.concatenate([dhbp[1:], jnp.zeros_like(dhbp[:1])], axis=0)
    d_mu, d_w0, d_a0, d_k_k, d_k_a, d_w2p, d_a2p, d_g2 = gp
    grads = {'rwkv_mu': d_mu, 'rwkv_w0': d_w0, 'rwkv_a0': d_a0, 'rwkv_k_k': d_k_k, 'rwkv_k_a': d_k_a,
             'rwkv_w2': d_w2p[:DECAY_LORA], 'rwkv_a2': d_a2p[DECAY_LORA:], 'rwkv_g2': d_g2,
             'rwkv_ln_w': d_ln_w, 'rwkv_ln_b': d_ln_b, 'rwkv_r_k': d_r_k}
    return dhb, grads, received


def _even_fwd(x, g, w_in, w_out, sinks, slopes, rprm, side=()):
    d = x.shape[1]
    q_w, kv_w = d // 2, d // 8
    proj = _matmul(x, w_in, norm=g, name="even_in")
    qa, ka, va, hb = proj[:, :q_w], proj[:, q_w:q_w + kv_w], proj[:, q_w + kv_w:q_w + 2 * kv_w], proj[:, q_w + 2 * kv_w:]
    qh, kh, vh = _heads(qa).astype(BF16), _heads(ka).astype(BF16), _heads(va).astype(BF16)
    oa, lse = _swa_fwd(qh, kh, vh, sinks, slopes)
    yb, rsaved, gathered = _rwkv_fwd(hb, rprm, side)
    if callable(w_out):
        w_out = w_out(gathered)
    cat = jnp.concatenate([_unheads(oa), yb], axis=1)
    x2 = _matmul(cat, w_out, res=x, name="even_out")
    return x2, (x, qh, kh, vh, oa, lse, cat, rsaved), gathered


def _even_bwd(dx2, saved, g, w_in, w_out, sinks, slopes, rprm, side=()):
    x, qh, kh, vh, oa, lse, cat, rsaved = saved
    d = x.shape[1]
    dcat = _matmul(dx2, w_out, tb=True, name="even_dcat")
    d_out = _matmul(cat, dx2, ta=True, name="even_dwout")
    dya, dyb = dcat[:, :d // 2], dcat[:, d // 2:]
    dq, dk2, dv2, dsk = _swa_bwd(qh, kh, vh, sinks, slopes, oa, lse, _heads(dya))
    d_sinks = jnp.sum(dsk[:, :, :SWA_GROUP, 0], axis=1).reshape(1, -1)
    dhb, rgrads, received = _rwkv_bwd(dyb, rsaved, rprm, side)
    dproj = jnp.concatenate([_unheads(dq), _unheads(_fold_kv(dk2)), _unheads(_fold_kv(dv2)), dhb], axis=1)
    d_in = _matmul(x, dproj, ta=True, norm=g, name="even_dwin")
    dx, dg = _matmul(dproj, w_in, tb=True, norm_bwd=(x, g, dx2), name="even_dhn")
    return dx, dg, d_in, d_out, d_sinks, rgrads, received


def _odd_fwd(x, g, w_in, w_out, b_f):
    d = x.shape[1]
    t = x.shape[0]
    nh = d // HEAD
    qkv = _matmul(x, w_in[:, :3 * d], norm=g, name="odd_in")
    fz = _matmul(x, w_in[:, 3 * d:], norm=g, name="odd_fz")
    c = _fox_gate_fwd(fz, b_f)
    blk = _tile(t, FOX_BLOCK, 128)
    c_row = (c.T * LOG2E).reshape(nh, t // blk, 1, blk)
    qh = _heads(qkv[:, :d] * FOX_QSCALE).astype(BF16)
    kh, vh = (_heads(qkv[:, i * d:(i + 1) * d]).astype(BF16) for i in (1, 2))
    o, lse = _fox_fwd(qh, kh, vh, c_row)
    y = _unheads(o)
    x2 = _matmul(y, w_out, res=x, name="odd_out")
    return x2, (x, fz, qh, kh, vh, c_row, o, lse, y)


def _odd_bwd(dx2, saved, g, w_in, w_out, b_f):
    x, fz, qh, kh, vh, c_row, o, lse, y = saved
    d = x.shape[1]
    t = x.shape[0]
    nh = d // HEAD
    dy = _matmul(dx2, w_out, tb=True, name="odd_dy")
    d_out = _matmul(y, dx2, ta=True, name="odd_dwout")
    dq, dk, dv, dcr, dcc = _fox_bwd(qh, kh, vh, c_row, lse, o, _heads(dy).astype(BF16))
    dfz, d_bf = _fox_gate_bwd(fz, b_f, (dcr.reshape(nh, t) + dcc.reshape(nh, t)).T)
    dqkv = jnp.concatenate([_unheads(dq), _unheads(dk), _unheads(dv)], axis=1)
    d_in = jnp.concatenate([_matmul(x, dqkv, ta=True, norm=g, name="odd_dwin"),
                            _matmul(x, dfz, ta=True, norm=g, name="odd_dwin_fz")], axis=1)
    dhn_fz = _matmul(dfz, w_in[:, 3 * d:], tb=True, name="odd_dhn_fz")
    dx, dg = _matmul(dqkv, w_in[:, :3 * d], tb=True, res=dhn_fz, norm_bwd=(x, g, dx2), name="odd_dhn")
    return dx, dg, d_in, d_out, d_bf


def _place():
    return lax.axis_index("x"), lax.axis_index("y"), lax.axis_index("c")


def _other_chips(x, y):
    return [(1 - x, y), (x, 1 - y), (1 - x, 1 - y)]


HBM_SPEC = pl.BlockSpec(memory_space=pltpu.HBM)


def _rows(ref, which, h):
    return ref.at[pl.ds(which * h, h)]


def _gather_phases(shards, ins, outs, sems):
    ici_send, ici_recv, d2d_send, d2d_recv = sems
    x, y, c = _place()
    me = 2 * x + y
    sibling = (x, y, 1 - c)
    pairs = [(i, j, px, py) for i in range(len(shards)) for j, (px, py) in enumerate(_other_chips(x, y))]
    half = lambda i, ref, which: _rows(ref, which, shards[i].shape[0] // 2)

    def over_ici(i, j, px, py, slot):
        return pltpu.make_async_remote_copy(
            src_ref=half(i, ins[i], c), dst_ref=half(i, outs[i].at[slot], c), send_sem=ici_send.at[3 * i + j],
            recv_sem=ici_recv.at[3 * i + j], device_id=(px, py, c), device_id_type=MESH)

    def over_d2d(i, j, px, py, which):
        part = half(i, outs[i].at[2 * px + py], which)
        return pltpu.make_async_remote_copy(src_ref=part, dst_ref=part, send_sem=d2d_send.at[3 * i + j],
                                            recv_sem=d2d_recv.at[3 * i + j], device_id=sibling, device_id_type=MESH)

    def start():
        for i, j, px, py in pairs:
            over_ici(i, j, px, py, me).start()

    def forward():
        for i, j, px, py in pairs:
            over_ici(i, j, px, py, 2 * px + py).wait_recv()
            over_d2d(i, j, px, py, c).start()

    def finish():
        for i, j, px, py in pairs:
            over_d2d(i, j, px, py, 1 - c).wait_recv()
        for i, j, px, py in pairs:
            over_ici(i, j, px, py, me).wait_send()
            over_d2d(i, j, px, py, c).wait_send()

    return start, forward, finish


def _gather_sems(n):
    return [pltpu.SemaphoreType.DMA((3 * n,))] * 4


def _gather_shapes(shards):
    return [jax.ShapeDtypeStruct((N_CHIPS,) + s.shape, s.dtype) for s in shards]


def _gather_chips(shards):
    n = len(shards)

    def body(*refs):
        start, forward, finish = _gather_phases(shards, refs[:n], refs[n:2 * n], refs[2 * n:])
        start()
        forward()
        finish()

    return pl.pallas_call(
        body, name="gather_weights", in_specs=[HBM_SPEC] * n, out_specs=[HBM_SPEC] * n,
        out_shape=_gather_shapes(shards), scratch_shapes=_gather_sems(n),
    )(*shards)


def _swap_halves(stacked):
    n = len(stacked)
    halves = [s.shape[1] // 2 for s in stacked]

    def body(*refs):
        ins, outs = refs[:n], refs[n:2 * n]
        send_sems, recv_sems = refs[2 * n:]
        x, y, c = _place()
        sends = []
        for i in range(n):
            cp = pltpu.make_async_remote_copy(
                src_ref=ins[i].at[:, pl.ds((1 - c) * halves[i], halves[i])], dst_ref=outs[i], send_sem=send_sems.at[i],
                recv_sem=recv_sems.at[i], device_id=(x, y, 1 - c), device_id_type=MESH)
            cp.start()
            sends.append(cp)
        for cp in sends:
            cp.wait_recv()
        for cp in sends:
            cp.wait_send()

    return pl.pallas_call(
        body, name="swap_halves", in_specs=[HBM_SPEC] * n, out_specs=[HBM_SPEC] * n,
        out_shape=[jax.ShapeDtypeStruct((N_CHIPS, h) + s.shape[2:], s.dtype) for s, h in zip(stacked, halves)],
        scratch_shapes=[pltpu.SemaphoreType.DMA((n,)), pltpu.SemaphoreType.DMA((n,))],
    )(*stacked)


def _scatter_phases(n, ins, outs, sems):
    send_sems, recv_sems = sems
    x, y, c = _place()
    pairs = [(i, j, px, py) for i in range(n) for j, (px, py) in enumerate(_other_chips(x, y))]

    def copy(i, j, px, py):
        return pltpu.make_async_remote_copy(src_ref=ins[i].at[2 * px + py], dst_ref=outs[i].at[j], send_sem=send_sems.at[3 * i + j],
                                            recv_sem=recv_sems.at[3 * i + j], device_id=(px, py, c), device_id_type=MESH)

    def start():
        for p in pairs:
            copy(*p).start()

    def finish():
        for p in pairs:
            copy(*p).wait_recv()
        for p in pairs:
            copy(*p).wait_send()

    return start, finish


def _scatter_sems(n):
    return [pltpu.SemaphoreType.DMA((3 * n,))] * 2


def _scatter_shapes(stacked):
    return [jax.ShapeDtypeStruct((3,) + s.shape[1:], s.dtype) for s in stacked]


def _scatter_chips(stacked):
    n = len(stacked)

    def body(*refs):
        start, finish = _scatter_phases(n, refs[:n], refs[n:2 * n], refs[2 * n:])
        start()
        finish()

    return pl.pallas_call(
        body, name="scatter_grads", in_specs=[HBM_SPEC] * n, out_specs=[HBM_SPEC] * n,
        out_shape=_scatter_shapes(stacked), scratch_shapes=_scatter_sems(n),
    )(*stacked)


def _swap_cores(arrs):
    n = len(arrs)

    def body(*refs):
        ins, outs = refs[:n], refs[n:2 * n]
        send_sems, recv_sems = refs[2 * n:]
        x, y, c = _place()
        sends = []
        for i in range(n):
            cp = pltpu.make_async_remote_copy(src_ref=ins[i], dst_ref=outs[i], send_sem=send_sems.at[i], recv_sem=recv_sems.at[i],
                                              device_id=(x, y, 1 - c), device_id_type=MESH)
            cp.start()
            sends.append(cp)
        for cp in sends:
            cp.wait_recv()
        for cp in sends:
            cp.wait_send()

    sem = pltpu.SemaphoreType.DMA((n,))
    return pl.pallas_call(
        body, name="swap_cores", in_specs=[HBM_SPEC] * n, out_specs=[HBM_SPEC] * n,
        out_shape=[jax.ShapeDtypeStruct(s.shape, s.dtype) for s in arrs],
        scratch_shapes=[sem, sem],
    )(*arrs)


def _allreduce_small(buf):
    rows = buf.shape[0]

    def body(in_ref, out_ref, gat, send_sems, recv_sems):
        x, y, c = _place()
        me = 4 * x + 2 * y + c
        gat[me] = in_ref[...]
        sends = []
        for k in range(1, N_DEV):
            bx, by, bc = (k >> 2) & 1, (k >> 1) & 1, k & 1
            peer = (x ^ bx, y ^ by, c ^ bc)
            cp = pltpu.make_async_remote_copy(src_ref=in_ref, dst_ref=gat.at[me], send_sem=send_sems.at[k - 1],
                                              recv_sem=recv_sems.at[k - 1], device_id=peer, device_id_type=MESH)
            cp.start()
            sends.append((cp, 4 * peer[0] + 2 * peer[1] + peer[2]))
        for k, (cp, slot) in enumerate(sends):
            pltpu.make_async_remote_copy(src_ref=in_ref, dst_ref=gat.at[slot], send_sem=send_sems.at[k], recv_sem=recv_sems.at[k],
                                         device_id=(x, y, c), device_id_type=MESH).wait_recv()
        for cp, _ in sends:
            cp.wait_send()
        acc = gat[0]
        for k in range(1, N_DEV):
            acc = acc + gat[k]
        out_ref[...] = acc

    vm = pl.BlockSpec(memory_space=pltpu.VMEM)
    return pl.pallas_call(
        body, name="allreduce_small", in_specs=[vm], out_specs=vm, out_shape=jax.ShapeDtypeStruct(buf.shape, F32),
        scratch_shapes=[pltpu.VMEM((N_DEV, rows, LANES), F32), pltpu.SemaphoreType.DMA((N_DEV - 1,)), pltpu.SemaphoreType.DMA((N_DEV - 1,))],
        compiler_params=_params(),
    )(buf)


def _as2d(a):
    return a.reshape(-1, a.shape[-1])


def _cast_bf16(a, name):
    a2 = _as2d(a)
    out = _rowwise(lambda v: v, [a2], [], [(a2.shape[1], BF16)], [], tm=512, name=name)[0]
    return out.reshape(a.shape)


def _assemble(gathered, axis):
    _, l, r, c = gathered.shape
    if axis == 1:
        return gathered.transpose(1, 0, 2, 3).reshape(l, N_CHIPS * r, c)
    return gathered.transpose(1, 2, 0, 3).reshape(l, r, N_CHIPS * c)


def _split_shards(full, axis):
    l, r, c = full.shape
    if axis == 1:
        return full.reshape(l, N_CHIPS, r // N_CHIPS, c).transpose(1, 0, 2, 3)
    return full.reshape(l, r, N_CHIPS, c // N_CHIPS).transpose(2, 0, 1, 3)


def _adamw_math(w, g, m, v):
    m2 = ADAM_B1 * m + (1.0 - ADAM_B1) * g
    v2 = ADAM_B2 * v + (1.0 - ADAM_B2) * (g * g)
    m_hat = m2 / (1.0 - ADAM_B1 ** ADAM_STEP)
    v_hat = v2 / (1.0 - ADAM_B2 ** ADAM_STEP)
    delta = -ADAM_LR * (m_hat / (jnp.sqrt(v_hat) + ADAM_EPS) + ADAM_WD * w)
    return delta, m2, v2


def _adamw_big(w, m, v, mine, other, core, name):
    shape = w.shape
    wd = shape[-1]
    h = mine.shape[0]
    tm = _tile(h, 256, 16)
    nh = h // tm

    def body(core_ref, w_ref, m_ref, v_ref, a_ref, b_ref, g_ref, d_ref, mo_ref, vo_ref):
        g = jnp.where(pl.program_id(0) // nh == core_ref[0], a_ref[...], b_ref[...])
        g_ref[...] = g
        d_ref[...], mo_ref[...], vo_ref[...] = _adamw_math(w_ref[...], g, m_ref[...], v_ref[...])

    rows = pl.BlockSpec((tm, wd), lambda i: (i, 0))
    half = pl.BlockSpec((tm, wd), lambda i: (i % nh, 0))
    outs = pl.pallas_call(
        body, name=name, grid=(2 * nh,),
        in_specs=[pl.BlockSpec(memory_space=pltpu.SMEM), rows, rows, rows, half, half], out_specs=[rows] * 4,
        out_shape=[jax.ShapeDtypeStruct((2 * h, wd), F32)] * 4,
        compiler_params=_params(("parallel",)),
    )(core, _as2d(w), _as2d(m), _as2d(v), mine, other)
    return [o.reshape(shape) for o in outs]


def _pair_add(s, o, which, name):
    _, h, c = o.shape
    tm = _tile(h, 256, 16)
    nb = h // tm

    def body(s_ref, o_ref, out_ref):
        out_ref[...] = (s_ref[...] + o_ref[...]).astype(BF16)

    mine = pl.BlockSpec((None, tm, c), lambda q, i: (q, which * nb + i, 0))
    theirs = pl.BlockSpec((None, tm, c), lambda q, i: (q, i, 0))
    return pl.pallas_call(
        body, name=name, grid=(N_CHIPS, nb), in_specs=[mine, theirs], out_specs=theirs,
        out_shape=jax.ShapeDtypeStruct(o.shape, BF16), compiler_params=_params(("parallel", "parallel")),
    )(s, o)


def _pair(s, o, core, name):
    return lax.cond(core == 0, lambda: _pair_add(s, o, 0, name + "_south"), lambda: _pair_add(s, o, 1, name + "_north"))


def _sum4(mine, recv, name):
    wd = mine.shape[-1]
    up = lambda v: v.astype(F32)
    return _rowwise(lambda a, b, c, d: ((up(a) + up(b)) + up(c)) + up(d), [mine, recv[0], recv[1], recv[2]], [], [(wd, F32)], [],
                    tm=256, name=name)[0]


def _pack(arrs):
    parts = []
    for a in arrs:
        f = a.reshape(-1).astype(F32)
        parts.append(jnp.pad(f, (0, (-f.shape[0]) % LANES)))
    flat = jnp.concatenate(parts)
    flat = jnp.pad(flat, (0, (-flat.shape[0]) % (8 * LANES)))
    return flat.reshape(-1, LANES)


def _unpack(buf, like):
    flat = buf.reshape(-1)
    out, off = [], 0
    for a in like:
        n = math.prod(a.shape)
        out.append(flat[off:off + n].reshape(a.shape))
        off += n + (-n) % LANES
    return out


def kernel(x, p, ffn1_norm, ffn1_w_gu, ffn1_w_down, mix_norm, ffn2_norm, ffn2_w_gu, ffn2_w_down, ple_norm, ple_w_gate, ple_w_proj, even_w_in, even_w_out, swa_sinks, rwkv_mu, rwkv_w0, rwkv_w2, rwkv_a0, rwkv_a2, rwkv_g2, rwkv_k_k, rwkv_k_a, rwkv_r_k, rwkv_ln_w, rwkv_ln_b, fox_w_in, fox_b_f, fox_w_out, final_norm, loss_target, m_ffn1_norm, m_ffn1_w_gu, m_ffn1_w_down, m_mix_norm, m_ffn2_norm, m_ffn2_w_gu, m_ffn2_w_down, m_ple_norm, m_ple_w_gate, m_ple_w_proj, m_even_w_in, m_even_w_out, m_swa_sinks, m_rwkv_mu, m_rwkv_w0, m_rwkv_w2, m_rwkv_a0, m_rwkv_a2, m_rwkv_g2, m_rwkv_k_k, m_rwkv_k_a, m_rwkv_r_k, m_rwkv_ln_w, m_rwkv_ln_b, m_fox_w_in, m_fox_b_f, m_fox_w_out, m_final_norm, v_ffn1_norm, v_ffn1_w_gu, v_ffn1_w_down, v_mix_norm, v_ffn2_norm, v_ffn2_w_gu, v_ffn2_w_down, v_ple_norm, v_ple_w_gate, v_ple_w_proj, v_even_w_in, v_even_w_out, v_swa_sinks, v_rwkv_mu, v_rwkv_w0, v_rwkv_w2, v_rwkv_a0, v_rwkv_a2, v_rwkv_g2, v_rwkv_k_k, v_rwkv_k_a, v_rwkv_r_k, v_rwkv_ln_w, v_rwkv_ln_b, v_fox_w_in, v_fox_b_f, v_fox_w_out, v_final_norm):
    args = locals()
    wts = {n: args[n] for n in WEIGHTS}
    mom = {n: args['m_' + n] for n in WEIGHTS}
    var = {n: args['v_' + n] for n in WEIGHTS}
    xs = x[0]
    tgt = loss_target[0]
    t, d = xs.shape
    depth = ffn1_norm.shape[0]
    rd = d // 2
    row = lambda a: a.reshape(1, -1)

    names = BIG + LORA
    chip = 2 * lax.axis_index("x") + lax.axis_index("y")
    cast = {n: _cast_bf16(wts[n], f"cast_{n}") for n in names}
    items = [(n, i) for n in names for i in range(wts[n].shape[0])]
    early = lambda n, i: (n in ('ffn1_w_gu', 'ffn1_w_down') and i == 0) or n == 'even_w_in' or n in LORA
    first = [it for it in items if early(*it)]
    later = [it for it in items if not early(*it)]
    full = {n: [None] * wts[n].shape[0] for n in names}

    def place(group, gathered):
        for (n, i), g in zip(group, gathered):
            g = lax.dynamic_update_index_in_dim(g, cast[n][i], chip, 0)
            full[n][i] = _assemble(g[:, None], SHARDED[n])[0]

    place(first, _gather_chips([cast[n][i] for n, i in first]))

    n_swa = d // (2 * HEAD)
    slopes = 2.0 ** (-8.0 * jnp.arange(1, n_swa + 1, dtype=F32) / n_swa)
    w2p, a2p = _lora_pad(full['rwkv_w2'][0].astype(F32), full['rwkv_a2'][0].astype(F32))
    rprm = {'mu': rwkv_mu, 'w0': rwkv_w0, 'a0': rwkv_a0, 'k_k': rwkv_k_k, 'k_a': rwkv_k_a, 'w2p': w2p, 'a2p': a2p,
            'g2': full['rwkv_g2'][0].astype(F32), 'ln_w': rwkv_ln_w, 'ln_b': rwkv_ln_b, 'r_k': rwkv_r_k.reshape(1, rd)}

    saved = []
    h = xs
    for i in range(depth):
        h, s1 = _ffn_fwd(h, row(ffn1_norm[i]), full['ffn1_w_gu'][i], full['ffn1_w_down'][i], f"l{i}_ffn1")
        if i % 2 == 0:
            def out_proj(gathered, j=i // 2):
                place(later, gathered)
                return full['even_w_out'][j]
            h, sm, _ = _even_fwd(h, row(mix_norm[i]), full['even_w_in'][i // 2], out_proj,
                                 swa_sinks[i // 2], slopes, rprm, [cast[n][k] for n, k in later])
        else:
            h, sm = _odd_fwd(h, row(mix_norm[i]), full['fox_w_in'][i // 2], full['fox_w_out'][i // 2], row(fox_b_f[i // 2]))
        h, s2 = _ffn_fwd(h, row(ffn2_norm[i]), full['ffn2_w_gu'][i], full['ffn2_w_down'][i], f"l{i}_ffn2")
        h, sp = _ple_fwd(h, row(ple_norm[i]), full['ple_w_gate'][i], p[i, 0], full['ple_w_proj'][i], f"l{i}_ple")
        saved.append((s1, sm, s2, sp))
    dx, loss_tile, d_final = _final_loss(h, row(final_norm), tgt)

    core = lax.axis_index("c")
    gl = {n: [None] * depth for n in ['ffn1_norm', 'ffn1_w_gu', 'ffn1_w_down', 'mix_norm', 'ffn2_norm', 'ffn2_w_gu',
                                      'ffn2_w_down', 'ple_norm', 'ple_w_gate', 'ple_w_proj']}
    g1 = {}
    local = {}

    def settle(n):
        if n in gl:
            per_layer = gl[n]
            local[n] = jnp.stack(per_layer).reshape((depth,) + per_layer[0].shape[-2:]) if per_layer[0].shape[0] != 1 \
                else jnp.concatenate(per_layer, axis=0)
        else:
            g = g1[n]
            local[n] = g.reshape((1,) + g.shape) if g.ndim == 2 and wts[n].ndim == 3 else g

    def chip_partials(group):
        for n in group:
            settle(n)
        stacked = [_split_shards(local[n], SHARDED[n]) for n in group]
        stacked = [s.reshape(N_CHIPS, -1, s.shape[-1]) for s in stacked]
        from_sibling = _swap_halves(stacked)
        return [_pair(s, o, core, f"pair_{n}") for n, s, o in zip(group, stacked, from_sibling)]

    def own_halves(group, parts, received):
        mine = [lax.dynamic_index_in_dim(s, chip, axis=0, keepdims=False) for s in parts]
        return [_sum4(a, r, f"sum_{n}") for n, a, r in zip(group, mine, received)]

    early = [n for n in BIG if n.startswith(('ffn2_', 'ple_', 'fox_'))]
    late = [n for n in BIG + LORA if n not in early]
    for i in reversed(range(depth)):
        s1, sm, s2, sp = saved[i]
        dx, gl['ple_norm'][i], gl['ple_w_gate'][i], gl['ple_w_proj'][i] = _ple_bwd(
            dx, sp, row(ple_norm[i]), full['ple_w_gate'][i], p[i, 0], f"l{i}_ple")
        dx, gl['ffn2_norm'][i], gl['ffn2_w_gu'][i], gl['ffn2_w_down'][i] = _ffn_bwd(
            dx, s2, row(ffn2_norm[i]), full['ffn2_w_gu'][i], full['ffn2_w_down'][i], f"l{i}_ffn2")
        if i % 2 == 0:
            early_parts = chip_partials(early)
            dx, gl['mix_norm'][i], g1['even_w_in'], g1['even_w_out'], g1['swa_sinks'], rg, early_recv = _even_bwd(
                dx, sm, row(mix_norm[i]), full['even_w_in'][i // 2], full['even_w_out'][i // 2], swa_sinks[i // 2], slopes, rprm,
                early_parts)
            g1.update(rg)
        else:
            dx, gl['mix_norm'][i], g1['fox_w_in'], g1['fox_w_out'], g1['fox_b_f'] = _odd_bwd(
                dx, sm, row(mix_norm[i]), full['fox_w_in'][i // 2], full['fox_w_out'][i // 2], row(fox_b_f[i // 2]))
        dx, gl['ffn1_norm'][i], gl['ffn1_w_gu'][i], gl['ffn1_w_down'][i] = _ffn_bwd(
            dx, s1, row(ffn1_norm[i]), full['ffn1_w_gu'][i], full['ffn1_w_down'][i], f"l{i}_ffn1")
    grad_x = dx.reshape(x.shape)

    late_parts = chip_partials(late)
    halves = own_halves(early, early_parts, early_recv) + own_halves(late, late_parts, _scatter_chips(late_parts))
    others = _swap_cores(halves)
    core1 = core.astype(jnp.int32).reshape(1)
    out_g, out_d, out_m, out_v = {}, {}, {}, {}
    for n, a, o in zip(early + late, halves, others):
        out_g[n], out_d[n], out_m[n], out_v[n] = _adamw_big(wts[n], mom[n], var[n], a, o, core1, f"adamw_{n}")

    for n in SMALL:
        if n != 'final_norm':
            settle(n)
    local['final_norm'] = d_final
    small_like = [wts[n] for n in SMALL]
    packed = _pack([local[n] for n in SMALL] + [loss_tile[0, :1]])
    parts = _unpack(_allreduce_small(packed), small_like + [loss_tile[0, :1]])
    g_small = dict(zip(SMALL, parts[:len(SMALL)]))
    loss = parts[-1].reshape(())
    wp, gp, mp, vp = (_pack([src[n] for n in SMALL]) for src in (wts, g_small, mom, var))
    d_p, m_p, v_p = _rowwise(_adamw_math, [wp, gp, mp, vp], [], [(LANES, F32)] * 3, [], tm=512, name="adamw_small")
    out_g.update(g_small)
    out_d.update(zip(SMALL, _unpack(d_p, small_like)))
    out_m.update(zip(SMALL, _unpack(m_p, small_like)))
    out_v.update(zip(SMALL, _unpack(v_p, small_like)))

    fit = lambda dct: [dct[n].reshape(wts[n].shape) for n in WEIGHTS]
    return (loss, grad_x, *fit(out_g), *fit(out_d), *fit(out_m), *fit(out_v))
```
